```python
import jax, jax.numpy as jnp
from jax import lax
import numpy as np

D_MODEL = 1024
BATCH = 16
SEQ = 2048
DEPTH = 4
DEC_BATCH = 8
DEC_SEQ = 64
PAST_LEN = 2048

CHUNK = 64
QBLOCK = 128
D_PLE = 256
A_HEADS = 8
A_HEAD_DIM = 64
A_WIDTH = A_HEADS * A_HEAD_DIM
A_LORA_W = 64
A_LORA_A = 64
A_SHIFT = 3 * A_WIDTH + A_LORA_W + A_LORA_A
B_HEADS = 4
B_HEAD_DIM = 128
B_WIDTH = B_HEADS * B_HEAD_DIM
B_CONV = 4
C_HEADS = 8
C_HEAD_DIM = 64
C_WIDTH = C_HEADS * C_HEAD_DIM
N_BRANCH = 3
BRANCH_WIDTH = 512
IN_SPLITS = (A_SHIFT, A_WIDTH,
             2 * B_WIDTH, B_WIDTH, B_HEADS, B_HEADS, B_WIDTH, B_WIDTH,
             C_WIDTH, C_WIDTH, C_WIDTH, C_WIDTH,
             N_BRANCH * D_MODEL)
N_IN = A_SHIFT + A_WIDTH + 5 * B_WIDTH + 2 * B_HEADS + 4 * C_WIDTH + N_BRANCH * D_MODEL
F_COL = A_SHIFT + A_WIDTH + 3 * B_WIDTH + B_HEADS
DEEPNORM_ALPHA = (2 * DEPTH) ** 0.25
DEEPNORM_BETA = (8 * DEPTH) ** -0.25
LN_EPS = 1e-5
GN_EPS_A = 64e-5
HN_EPS = 1e-6

kernel_name = 'rwkv7_mlstm_stickbreak_gated_stream_step'


def layer_norm(x, g, b):
    xf = x.astype(jnp.float32)
    mu = jnp.mean(xf, -1, keepdims=True)
    var = jnp.mean(jnp.square(xf - mu), -1, keepdims=True)
    return ((xf - mu) * lax.rsqrt(var + LN_EPS) * g + b).astype(x.dtype)


def head_norm(x, g, b, eps):
    xf = x.astype(jnp.float32)
    mu = jnp.mean(xf, -1, keepdims=True)
    var = jnp.mean(jnp.square(xf - mu), -1, keepdims=True)
    y = ((xf - mu) * lax.rsqrt(var + eps)).reshape(x.shape[0], x.shape[1], -1) * g
    return y if b is None else y + b


def token_shift(u, last, mu):
    prev = jnp.concatenate([last[:, None].astype(u.dtype), u[:, :-1]], axis=1)
    return u + (prev - u) * mu, u[:, -1]


def causal_conv(u, buf, w, b):
    T = u.shape[1]
    ext = jnp.concatenate([buf.astype(u.dtype), u], axis=1)
    out = b + sum(ext[:, j:j + T] * w[j] for j in range(B_CONV))
    return out, ext[:, T:]


def rwkv7_branch(cols, last, S0, mu, w0, w_up, a0, a_up, k_k, k_a, r_k, gn_g, gn_b):
    Bn, T, _ = cols.shape
    f32 = jnp.float32
    xs, new_last = token_shift(cols, last, mu)
    r, k, v, wd, ad = jnp.split(xs, [A_WIDTH, 2 * A_WIDTH, 3 * A_WIDTH, 3 * A_WIDTH + A_LORA_W], axis=-1)
    w_log = -jax.nn.softplus(-(w0 + jnp.tanh(wd) @ w_up).astype(f32)) - 0.5
    decay = jnp.exp(-jnp.exp(w_log))
    a = jax.nn.sigmoid((a0 + ad @ a_up).astype(f32))
    heads = lambda t: t.astype(f32).reshape(Bn, T, A_HEADS, A_HEAD_DIM)
    kk = heads(k * k_k)
    kk = kk / jnp.maximum(jnp.sqrt(jnp.sum(kk * kk, -1, keepdims=True)), 1e-12)
    kh = heads(k * (1.0 + (a - 1.0) * k_a))
    rh, vh, wh, ah = heads(r), heads(v), heads(decay), heads(a)

    def step(S, inp):
        r_t, w_t, k_t, v_t, kk_t, a_t = inp
        sa = jnp.einsum('bhvk,bhk->bhv', S, -kk_t)
        S = (S * w_t[:, :, None, :] + sa[..., None] * (kk_t * a_t)[:, :, None, :]
             + v_t[..., None] * k_t[:, :, None, :])
        return S, jnp.einsum('bhvk,bhk->bhv', S, r_t)

    tmaj = lambda t: jnp.moveaxis(t, 1, 0)
    S_T, y = lax.scan(step, S0.astype(f32), tuple(tmaj(t) for t in (rh, wh, kh, vh, kk, ah)))
    y = jnp.moveaxis(y, 0, 1)
    bonus = jnp.sum(rh * kh * r_k, -1, keepdims=True) * vh
    out = head_norm(y, gn_g, gn_b, GN_EPS_A) + bonus.reshape(Bn, T, A_WIDTH)
    return out.astype(cols.dtype), new_last, S_T.astype(S0.dtype)


def mlstm_branch(qk_raw, v, i_pre, f_pre, o_pre, buf, C0, n0, m0, conv_w, conv_b, hn_g):
    Bn, T, _ = v.shape
    f32 = jnp.float32
    qk, new_buf = causal_conv(qk_raw, buf, conv_w, conv_b)
    q, k = jnp.split(jax.nn.silu(qk), 2, axis=-1)
    L = min(CHUNK, T)
    NC = T // L

    def chunks(t):
        return t.astype(f32).reshape(Bn, NC, L, B_HEADS, B_HEAD_DIM).transpose(1, 0, 3, 2, 4)

    def gchunks(t):
        return t.astype(f32).reshape(Bn, NC, L, B_HEADS).transpose(1, 0, 3, 2)

    causal = jnp.tril(jnp.ones((L, L), dtype=bool))

    def step(carry, inp):
        C, n, m = carry
        q_c, k_c, v_c, i_c, lf_c = inp
        b = jnp.cumsum(lf_c, axis=-1)
        D = jnp.where(causal, b[..., :, None] - b[..., None, :] + i_c[..., None, :], -jnp.inf)
        inter = b + m[..., None]
        m_t = jnp.maximum(inter, jnp.max(D, axis=-1))
        W = jnp.exp(D - m_t[..., None]) * jnp.einsum('bhtd,bhsd->bhts', q_c, k_c)
        scale = jnp.exp(inter - m_t)
        num = jnp.einsum('bhts,bhsd->bhtd', W, v_c) + scale[..., None] * jnp.einsum('bhvk,bhtk->bhtv', C, q_c)
        den = jnp.sum(W, -1) + scale * jnp.einsum('bhk,bhtk->bht', n, q_c)
        h = num / jnp.maximum(jnp.abs(den), jnp.exp(-m_t))[..., None]
        m_new = m_t[..., -1]
        g = jnp.exp(D[..., -1, :] - m_new[..., None])
        c_scale = jnp.exp(inter[..., -1] - m_new)
        C = c_scale[..., None, None] * C + jnp.einsum('bhs,bhsv,bhsk->bhvk', g, v_c, k_c)
        n = c_scale[..., None] * n + jnp.einsum('bhs,bhsk->bhk', g, k_c)
        return (C, n, m_new), h

    xs = (chunks(q), chunks(k) * B_HEAD_DIM ** -0.5, chunks(v), gchunks(i_pre),
          jax.nn.log_sigmoid(gchunks(f_pre)))
    (C_T, n_T, m_T), h = lax.scan(step, (C0.astype(f32), n0.astype(f32), m0.astype(f32)), xs)
    h = h.transpose(1, 0, 3, 2, 4).reshape(Bn, T, B_HEADS, B_HEAD_DIM)
    out = jax.nn.sigmoid(o_pre.astype(f32)) * head_norm(h, hn_g, None, HN_EPS)
    return (out.astype(v.dtype), new_buf, C_T.astype(C0.dtype), n_T.astype(n0.dtype), m_T.astype(m0.dtype))


def stick_breaking_block(q_blk, q_pos, k, v, k_pos):
    z = jnp.einsum('bqhd,bshd->bhqs', q_blk, k).astype(jnp.float32) * C_HEAD_DIM ** -0.5
    mask = k_pos[None, :] < q_pos[:, None]
    log_keep = jnp.where(mask, jax.nn.log_sigmoid(-z), 0.0)
    between = lax.cumsum(log_keep, axis=3, reverse=True) - log_keep
    A = jnp.where(mask, jnp.exp(jax.nn.log_sigmoid(z) + between), 0.0)
    return jnp.einsum('bhqs,bshd->bqhd', A.astype(v.dtype), v)


def stick_breaking(q, k_new, v_new, k_past, v_past):
    Bn, T = q.shape[:2]
    P = k_past.shape[1]
    k = jnp.concatenate([k_past.astype(k_new.dtype), k_new], axis=1)
    v = jnp.concatenate([v_past.astype(v_new.dtype), v_new], axis=1)
    k_pos = jnp.arange(P + T)
    q_pos = P + jnp.arange(T)
    if T > QBLOCK:
        nb = T // QBLOCK
        qb = q.reshape(Bn, nb, QBLOCK, C_HEADS, C_HEAD_DIM).transpose(1, 0, 2, 3, 4)
        out = lax.map(lambda blk: stick_breaking_block(blk[0], blk[1], k, v, k_pos),
                      (qb, q_pos.reshape(nb, QBLOCK)))
        return out.transpose(1, 0, 2, 3, 4).reshape(Bn, T, C_WIDTH)
    return stick_breaking_block(q, q_pos, k, v, k_pos).reshape(Bn, T, C_WIDTH)


def trunk_layer(x, p, st, lw):
    shift0, wkv0, conv0, c0, n0, m0, k_past, v_past = st
    Bn, T, _ = x.shape
    cols = x @ lw['w_in'] + lw['b_in']
    (a_cols, a_z, b_qk, b_v, b_i, b_f, b_o, b_z,
     c_q, c_k, c_v, c_z, gates) = jnp.split(cols, np.cumsum(IN_SPLITS)[:-1], axis=-1)
    ya, shift1, wkv1 = rwkv7_branch(a_cols, shift0, wkv0, lw['mu_a'], lw['w0_a'], lw['w_decay_up'],
                                    lw['a0_a'], lw['w_iclr_up'], lw['k_k'], lw['k_a'], lw['r_k'],
                                    lw['gn_a_g'], lw['gn_a_b'])
    yb, conv1, c1, n1, m1 = mlstm_branch(b_qk, b_v, b_i, b_f, b_o, conv0, c0, n0, m0,
                                         lw['conv_b_w'], lw['conv_b_b'], lw['hn_b_g'])
    heads_c = lambda t: t.reshape(Bn, T, C_HEADS, C_HEAD_DIM)
    kc, vc = heads_c(c_k), heads_c(c_v)
    yc = stick_breaking(heads_c(c_q), kc, vc, k_past, v_past)
    ys = jnp.stack([ya * jax.nn.silu(a_z), yb * jax.nn.silu(b_z), yc * jax.nn.silu(c_z)], axis=2)
    br = jnp.einsum('btnc,ncd->btnd', ys, lw['w_branch'])
    g = jax.nn.sigmoid(gates.reshape(Bn, T, N_BRANCH, D_MODEL))
    mix = jnp.sum(g * br, axis=2) @ lw['w_out']
    x = layer_norm(DEEPNORM_ALPHA * x + mix, lw['ln_g'], lw['ln_b'])
    x = x + (p @ lw['w_ple']) * jax.nn.sigmoid(x @ lw['w_ple_gate'])
    return x, (shift1, wkv1, conv1, c1, n1, m1, kc, vc)


def run_group(x, p, init, ln_in_g, ln_in_b, weights):
    x = layer_norm(x, ln_in_g, ln_in_b)
    new = [[] for _ in init]
    for i in range(DEPTH):
        lw = {name: w[i] for name, w in weights.items()}
        x, st = trunk_layer(x, p[i], tuple(s[i] for s in init), lw)
        for lst, s in zip(new, st):
            lst.append(s)
    return x, [jnp.stack(lst) for lst in new]


def setup_inputs(seed: int = 0) -> dict:
    key = jax.random.key(seed)
    ks = iter(jax.random.split(key, 40))

    def nrm(shape, scale=1.0):
        return jax.random.normal(next(ks), shape, jnp.float32) * scale

    def unif(shape, lo, hi):
        return jax.random.uniform(next(ks), shape, jnp.float32, lo, hi)

    def gain(shape):
        return 1.0 + nrm(shape, 0.01)

    b_in = nrm((DEPTH, N_IN), 0.01).at[:, F_COL:F_COL + B_HEADS].add(jnp.linspace(3.0, 6.0, B_HEADS))
    return {
        'x_prompt': nrm((BATCH, SEQ, D_MODEL)),
        'x_sample': nrm((DEC_BATCH, DEC_SEQ, D_MODEL)),
        'state_shift_a': nrm((DEPTH, DEC_BATCH, A_SHIFT)),
        'state_wkv': nrm((DEPTH, DEC_BATCH, A_HEADS, A_HEAD_DIM, A_HEAD_DIM), 0.5),
        'state_conv_b': nrm((DEPTH, DEC_BATCH, B_CONV - 1, 2 * B_WIDTH)),
        'state_mlstm_c': nrm((DEPTH, DEC_BATCH, B_HEADS, B_HEAD_DIM, B_HEAD_DIM), 0.1),
        'state_mlstm_n': nrm((DEPTH, DEC_BATCH, B_HEADS, B_HEAD_DIM), 0.1),
        'state_mlstm_m': nrm((DEPTH, DEC_BATCH, B_HEADS)),
        'cache_sb_k': nrm((DEPTH, DEC_BATCH, PAST_LEN, C_HEADS, C_HEAD_DIM)),
        'cache_sb_v': nrm((DEPTH, DEC_BATCH, PAST_LEN, C_HEADS, C_HEAD_DIM)),
        'p_prompt': nrm((DEPTH, BATCH, SEQ, D_PLE)),
        'p_sample': nrm((DEPTH, DEC_BATCH, DEC_SEQ, D_PLE)),
        'ln_in_g': gain((D_MODEL,)),
        'ln_in_b': nrm((D_MODEL,), 0.01),
        'w_in': nrm((DEPTH, D_MODEL, N_IN), D_MODEL ** -0.5),
        'b_in': b_in,
        'mu_a': unif((DEPTH, A_SHIFT), 0.0, 1.0),
        'w0_a': unif((DEPTH, A_WIDTH), -6.0, 1.0),
        'w_decay_up': nrm((DEPTH, A_LORA_W, A_WIDTH), 0.5 * A_LORA_W ** -0.5),
        'a0_a': nrm((DEPTH, A_WIDTH), 0.1),
        'w_iclr_up': nrm((DEPTH, A_LORA_A, A_WIDTH), 0.5 * A_LORA_A ** -0.5),
        'k_k': 0.85 + nrm((DEPTH, A_WIDTH), 0.02),
        'k_a': gain((DEPTH, A_WIDTH)),
        'r_k': nrm((DEPTH, A_HEADS, A_HEAD_DIM), 0.1),
        'gn_a_g': gain((DEPTH, A_WIDTH)),
        'gn_a_b': nrm((DEPTH, A_WIDTH), 0.01),
        'conv_b_w': nrm((DEPTH, B_CONV, 2 * B_WIDTH), 0.5),
        'conv_b_b': nrm((DEPTH, 2 * B_WIDTH), 0.01),
        'hn_b_g': gain((DEPTH, B_WIDTH)),
        'w_branch': nrm((DEPTH, N_BRANCH, BRANCH_WIDTH, D_MODEL), DEEPNORM_BETA * BRANCH_WIDTH ** -0.5),
        'w_out': nrm((DEPTH, D_MODEL, D_MODEL), DEEPNORM_BETA * D_MODEL ** -0.5),
        'ln_g': gain((DEPTH, D_MODEL)),
        'ln_b': nrm((DEPTH, D_MODEL), 0.01),
        'w_ple': nrm((DEPTH, D_PLE, D_MODEL), D_PLE ** -0.5),
        'w_ple_gate': nrm((DEPTH, D_MODEL, D_MODEL), D_MODEL ** -0.5),
    }


def reference(x_prompt, x_sample, state_shift_a, state_wkv, state_conv_b, state_mlstm_c, state_mlstm_n,
              state_mlstm_m, cache_sb_k, cache_sb_v, p_prompt, p_sample, ln_in_g, ln_in_b, w_in, b_in,
              mu_a, w0_a, w_decay_up, a0_a, w_iclr_up, k_k, k_a, r_k, gn_a_g, gn_a_b, conv_b_w, conv_b_b,
              hn_b_g, w_branch, w_out, ln_g, ln_b, w_ple, w_ple_gate):
    weights = dict(w_in=w_in, b_in=b_in, mu_a=mu_a, w0_a=w0_a, w_decay_up=w_decay_up, a0_a=a0_a,
                   w_iclr_up=w_iclr_up, k_k=k_k, k_a=k_a, r_k=r_k, gn_a_g=gn_a_g, gn_a_b=gn_a_b,
                   conv_b_w=conv_b_w, conv_b_b=conv_b_b, hn_b_g=hn_b_g, w_branch=w_branch, w_out=w_out,
                   ln_g=ln_g, ln_b=ln_b, w_ple=w_ple, w_ple_gate=w_ple_gate)
    Bp = x_prompt.shape[0]
    dt = x_prompt.dtype
    init_prompt = (jnp.zeros((DEPTH, Bp, A_SHIFT), dt),
                   jnp.zeros((DEPTH, Bp, A_HEADS, A_HEAD_DIM, A_HEAD_DIM), dt),
                   jnp.zeros((DEPTH, Bp, B_CONV - 1, 2 * B_WIDTH), dt),
                   jnp.zeros((DEPTH, Bp, B_HEADS, B_HEAD_DIM, B_HEAD_DIM), dt),
                   jnp.zeros((DEPTH, Bp, B_HEADS, B_HEAD_DIM), dt),
                   jnp.zeros((DEPTH, Bp, B_HEADS), dt),
                   jnp.zeros((DEPTH, Bp, 0, C_HEADS, C_HEAD_DIM), dt),
                   jnp.zeros((DEPTH, Bp, 0, C_HEADS, C_HEAD_DIM), dt))
    init_sample = (state_shift_a, state_wkv, state_conv_b, state_mlstm_c, state_mlstm_n, state_mlstm_m,
                   cache_sb_k, cache_sb_v)
    y_prompt, sp = run_group(x_prompt, p_prompt, init_prompt, ln_in_g, ln_in_b, weights)
    y_sample, ss = run_group(x_sample, p_sample, init_sample, ln_in_g, ln_in_b, weights)
    return (y_prompt, y_sample, sp[0], ss[0], sp[1], ss[1], sp[2], ss[2], sp[3], ss[3],
            sp[4], ss[4], sp[5], ss[5], sp[6], ss[6], sp[7], ss[7])
```

```python
import functools

import jax
import jax.numpy as jnp
from jax import lax
from jax.experimental import pallas as pl
from jax.experimental.pallas import tpu as pltpu

F32 = jnp.float32
BF16 = jnp.bfloat16

D_MODEL = 1024
DEPTH = 4
CHUNK = 64
D_PLE = 256
A_HEADS = 8
A_HEAD_DIM = 64
A_WIDTH = A_HEADS * A_HEAD_DIM
A_LORA = 64
A_SHIFT = 3 * A_WIDTH + 2 * A_LORA
B_HEADS = 4
B_HEAD_DIM = 128
B_WIDTH = B_HEADS * B_HEAD_DIM
B_CONV = 4
C_HEADS = 8
C_HEAD_DIM = 64
C_WIDTH = C_HEADS * C_HEAD_DIM
N_BRANCH = 3
DEEPNORM_ALPHA = (2 * DEPTH) ** 0.25
LN_EPS = 1e-5
GN_EPS_A = 64e-5
HN_EPS = 1e-6

LANES = 128
SUBLANES = 8
VMEM_LIMIT_BYTES = 56 * 1024 * 1024

_OFF_A_COLS = 0
_OFF_A_Z = _OFF_A_COLS + A_SHIFT
_OFF_B_QK = _OFF_A_Z + A_WIDTH
_OFF_B_V = _OFF_B_QK + 2 * B_WIDTH
_OFF_B_I = _OFF_B_V + B_WIDTH
_OFF_B_F = _OFF_B_I + B_HEADS
_OFF_B_O = _OFF_B_F + B_HEADS
_OFF_B_Z = _OFF_B_O + B_WIDTH
_OFF_C_Q = _OFF_B_Z + B_WIDTH
_OFF_C_K = _OFF_C_Q + C_WIDTH
_OFF_C_V = _OFF_C_K + C_WIDTH
_OFF_C_Z = _OFF_C_V + C_WIDTH
_OFF_GATES = _OFF_C_Z + C_WIDTH
_N_IN = _OFF_GATES + N_BRANCH * D_MODEL

_MIX_WIDTHS = (A_SHIFT, 2 * B_WIDTH, B_WIDTH, B_WIDTH, C_WIDTH, C_WIDTH, C_WIDTH, LANES)

NN = ((1,), (0,))
NT = ((1,), (1,))
TN = ((0,), (0,))


def _dg(a, b, dims=NN):
    return lax.dot_general(a, b, (dims, ((), ())), preferred_element_type=F32)


def _mm(a, b, dims=NN):
    return _dg(a.astype(BF16), b.astype(BF16), dims)


def _split3(x):
    h1 = x.astype(BF16)
    r1 = x - h1.astype(F32)
    h2 = r1.astype(BF16)
    h3 = (r1 - h2.astype(F32)).astype(BF16)
    return h1, h2, h3


def _mm_exact_rhs(lhs_bf16, x):
    h1, h2, h3 = _split3(x)
    return _dg(lhs_bf16, h1) + (_dg(lhs_bf16, h2) + _dg(lhs_bf16, h3))


def _mm_exact_lhs(x, rhs_bf16):
    h1, h2, h3 = _split3(x)
    return _dg(h1, rhs_bf16) + (_dg(h2, rhs_bf16) + _dg(h3, rhs_bf16))


def _softplus(x):
    return jnp.maximum(x, 0.0) + jnp.log1p(jnp.exp(-jnp.abs(x)))


def _sigmoid(x):
    return jax.nn.sigmoid(x)


def _silu(x):
    return x * jax.nn.sigmoid(x)


def _norm_rows(x, eps):
    mu = jnp.mean(x, axis=-1, keepdims=True)
    xc = x - mu
    var = jnp.mean(xc * xc, axis=-1, keepdims=True)
    return xc * lax.rsqrt(var + eps)


def _tile(n, pref):
    if n <= pref:
        return n
    t = pref - pref % SUBLANES
    while n % t:
        t -= SUBLANES
    return t


def _params(*sem):
    return pltpu.CompilerParams(dimension_semantics=sem, vmem_limit_bytes=VMEM_LIMIT_BYTES)


def _const_spec(shape):
    nd = len(shape)
    return pl.BlockSpec(shape, lambda *_: (0,) * nd)


def _ln_kernel(x_ref, g_ref, b_ref, o_ref):
    o_ref[...] = _norm_rows(x_ref[...], LN_EPS) * g_ref[...] + b_ref[...]


def _layer_norm_call(x, g, b):
    n, d = x.shape
    tm = _tile(n, 512)
    return pl.pallas_call(
        _ln_kernel,
        out_shape=jax.ShapeDtypeStruct((n, d), F32),
        grid=(n // tm,),
        in_specs=[pl.BlockSpec((tm, d), lambda i: (i, 0)), _const_spec((1, d)), _const_spec((1, d))],
        out_specs=pl.BlockSpec((tm, d), lambda i: (i, 0)),
        compiler_params=_params("parallel"),
        name="ln_in",
    )(x, g.reshape(1, d), b.reshape(1, d))


def _inproj_kernel(x_ref, w_ref, b_ref, *out_refs):
    xb = x_ref[...].astype(BF16)
    off = 0
    for o_ref, width in zip(out_refs, _MIX_WIDTHS):
        o_ref[...] = _dg(xb, w_ref[:, off:off + width]) + b_ref[:, off:off + width]
        off += width


def _inproj_call(x, w, b):
    n, d = x.shape
    tm = _tile(n, 256)
    wtot = sum(_MIX_WIDTHS)
    return pl.pallas_call(
        _inproj_kernel,
        out_shape=tuple(jax.ShapeDtypeStruct((n, wd), F32) for wd in _MIX_WIDTHS),
        grid=(n // tm,),
        in_specs=[pl.BlockSpec((tm, d), lambda i: (i, 0)), _const_spec((d, wtot)), _const_spec((1, wtot))],
        out_specs=tuple(pl.BlockSpec((tm, wd), lambda i: (i, 0)) for wd in _MIX_WIDTHS),
        compiler_params=_params("parallel"),
        name="inproj",
    )(x, w, b)


def _rwkv_pre_kernel(cols_ref, last_ref, mu_ref, w0_ref, wup_ref, a0_ref, aup_ref, kk_ref, ka_ref, hsum_ref,
                     r_o, lw_o, kh_o, v_o, kkn_o, kka_o, last_o, prev_scr):
    t = pl.program_id(1)

    @pl.when(t == 0)
    def _():
        prev_scr[...] = last_ref[0]

    u = cols_ref[0]
    tt = u.shape[0]
    row = lax.broadcasted_iota(jnp.int32, u.shape, 0)
    prev = jnp.where(row == 0, prev_scr[...], pltpu.roll(u, 1, 0))
    prev_scr[...] = u[tt - 1:tt, :]
    last_o[0] = u[tt - 1:tt, :]
    xs = u + (prev - u) * mu_ref[...]
    r = xs[:, 0:A_WIDTH]
    k = xs[:, A_WIDTH:2 * A_WIDTH]
    v = xs[:, 2 * A_WIDTH:3 * A_WIDTH]
    lora = xs[:, 3 * A_WIDTH:]
    w_log = -_softplus(-(w0_ref[...] + _mm(jnp.tanh(lora), wup_ref[...]))) - 0.5
    a = _sigmoid(a0_ref[...] + _mm(lora, aup_ref[...]))
    kk = k * kk_ref[...]
    ss = _mm_exact_lhs(kk * kk, hsum_ref[...])
    kk = kk / jnp.maximum(jnp.sqrt(ss), 1e-12)
    r_o[0] = r
    lw_o[0] = -jnp.exp(w_log)
    kh_o[0] = k * (1.0 + (a - 1.0) * ka_ref[...])
    v_o[0] = v
    kkn_o[0] = kk
    kka_o[0] = kk * a


def _rwkv_pre_call(cols, last, lw):
    bsz, t, _ = cols.shape
    tt = _tile(t, 256)
    blk = lambda w: pl.BlockSpec((1, tt, w), lambda b, i: (b, i, 0))
    row = lambda w: pl.BlockSpec((1, 1, w), lambda b, i: (b, 0, 0))
    outs = pl.pallas_call(
        _rwkv_pre_kernel,
        out_shape=tuple(jax.ShapeDtypeStruct((bsz, t, A_WIDTH), F32) for _ in range(6))
        + (jax.ShapeDtypeStruct((bsz, 1, A_SHIFT), F32),),
        grid=(bsz, t // tt),
        in_specs=[blk(A_SHIFT), row(A_SHIFT), _const_spec((1, A_SHIFT)), _const_spec((1, A_WIDTH)),
                  _const_spec((2 * A_LORA, A_WIDTH)), _const_spec((1, A_WIDTH)), _const_spec((2 * A_LORA, A_WIDTH)),
                  _const_spec((1, A_WIDTH)), _const_spec((1, A_WIDTH)), _const_spec((A_WIDTH, A_WIDTH))],
        out_specs=tuple(blk(A_WIDTH) for _ in range(6)) + (row(A_SHIFT),),
        scratch_shapes=[pltpu.VMEM((1, A_SHIFT), F32)],
        compiler_params=_params("parallel", "arbitrary"),
        name="rwkv_pre",
    )(cols, last.reshape(bsz, 1, A_SHIFT), lw["mu_a"], lw["w0_a"], lw["wup_pad"], lw["a0_a"], lw["aup_pad"],
      lw["k_k"], lw["k_a"], lw["head_sum"])
    return outs


def _rwkv_rec_kernel(r_ref, lw_ref, kh_ref, v_ref, kk_ref, kka_ref, s0_ref, rk_ref, gng_ref, gnb_ref,
                     y_o, s_o, s_scr):
    t = pl.program_id(1)
    L = CHUNK
    tt = r_ref.shape[1]
    n_pairs = A_HEADS // 2

    @pl.when(t == 0)
    def _():
        s_scr[...] = s0_ref[0]

    ri = lax.broadcasted_iota(jnp.int32, (L, L), 0)
    ci = lax.broadcasted_iota(jnp.int32, (L, L), 1)
    strict = ri > ci
    incl = ri >= ci
    tri = jnp.where(incl, 1.0, 0.0).astype(BF16)
    eye = jnp.where(ri == ci, 1.0, 0.0)
    lane = lax.broadcasted_iota(jnp.int32, (1, LANES), 1)
    head_masks = (jnp.where(lane < A_HEAD_DIM, 1.0, 0.0), jnp.where(lane >= A_HEAD_DIM, 1.0, 0.0))
    inv_d = 1.0 / A_HEAD_DIM

    def chunk(c, carry):
        rows = pl.ds(pl.multiple_of(c * L, L), L)
        lw = lw_ref[0, rows, :]
        cin = _mm_exact_rhs(tri, lw)
        ein = jnp.exp(cin)
        eex = jnp.exp(cin - lw)
        einv = jnp.exp(-cin)
        r = r_ref[0, rows, :]
        kh = kh_ref[0, rows, :]
        v = v_ref[0, rows, :]
        at = -kk_ref[0, rows, :] * eex
        bt = kka_ref[0, rows, :] * einv
        kt = kh * einv
        rt = r * ein
        wl = ein[L - 1:L, :]
        btw = bt * wl
        ktw = kt * wl
        bonus_w = r * kh * rk_ref[...]
        for p in range(n_pairs):
            sl = slice(p * LANES, (p + 1) * LANES)
            s_pair = s_scr[p]
            s_pair_b = s_pair.astype(BF16)
            y_pair = jnp.zeros((L, LANES), F32)
            upd_l = []
            upd_r = []
            for hm in head_masks:
                at_h = at[:, sl] * hm
                rt_h = rt[:, sl] * hm
                v_h = v[:, sl] * hm
                lhs = jnp.concatenate([at_h, rt_h], axis=0).astype(BF16)
                sc_b = _dg(lhs, bt[:, sl].astype(BF16), NT)
                sc_k = _dg(lhs, kt[:, sl].astype(BF16), NT)
                a_ab = jnp.where(strict, sc_b[:L], 0.0)
                a_ak = jnp.where(strict, sc_k[:L], 0.0)
                m_rb = jnp.where(incl, sc_b[L:], 0.0)
                m_rk = jnp.where(incl, sc_k[L:], 0.0)
                tm = eye + a_ab
                ap = a_ab
                for _ in range(5):
                    ap = _mm(ap, ap)
                    tm = tm + _mm(tm, ap)
                x1 = _mm(jnp.concatenate([a_ak, m_rk], axis=0), v_h)
                tmb = tm.astype(BF16)
                w_h = _dg(tmb, at_h.astype(BF16))
                u0 = _dg(tmb, x1[:L].astype(BF16))
                st = _dg(jnp.concatenate([w_h, rt_h], axis=0).astype(BF16), s_pair_b, NT)
                u_h = u0 + st[:L]
                y_pair = y_pair + (st[L:] + x1[L:] + _mm(m_rb, u_h))
                upd_l += [u_h, v_h]
                upd_r += [btw[:, sl] * hm, ktw[:, sl] * hm]
            upd = _mm(jnp.concatenate(upd_l, axis=0), jnp.concatenate(upd_r, axis=0), TN)
            s_scr[p] = s_pair * wl[:, sl] + upd
            normed = jnp.zeros((L, LANES), F32)
            bonus = jnp.zeros((L, LANES), F32)
            for hm in head_masks:
                mu = jnp.sum(y_pair * hm, axis=-1, keepdims=True) * inv_d
                yc = (y_pair - mu) * hm
                var = jnp.sum(yc * yc, axis=-1, keepdims=True) * inv_d
                normed = normed + yc * lax.rsqrt(var + GN_EPS_A)
                bonus = bonus + jnp.sum(bonus_w[:, sl] * hm, axis=-1, keepdims=True) * (v[:, sl] * hm)
            y_o[0, rows, sl] = (normed * gng_ref[:, sl] + gnb_ref[:, sl]) + bonus
        return carry

    lax.fori_loop(0, tt // L, chunk, 0)
    s_o[0] = s_scr[...]


def _rwkv_rec_call(r, lwd, kh, v, kk, kka, s0_pairs, lw):
    bsz, t, _ = r.shape
    tt = _tile(t, 512)
    blk = pl.BlockSpec((1, tt, A_WIDTH), lambda b, i: (b, i, 0))
    st = pl.BlockSpec((1, A_HEADS // 2, LANES, LANES), lambda b, i: (b, 0, 0, 0))
    return pl.pallas_call(
        _rwkv_rec_kernel,
        out_shape=(jax.ShapeDtypeStruct((bsz, t, A_WIDTH), F32),
                   jax.ShapeDtypeStruct((bsz, A_HEADS // 2, LANES, LANES), F32)),
        grid=(bsz, t // tt),
        in_specs=[blk] * 6 + [st, _const_spec((1, A_WIDTH)), _const_spec((1, A_WIDTH)), _const_spec((1, A_WIDTH))],
        out_specs=(blk, st),
        scratch_shapes=[pltpu.VMEM((A_HEADS // 2, LANES, LANES), F32)],
        compiler_params=_params("parallel", "arbitrary"),
        name="rwkv_rec",
    )(r, lwd, kh, v, kk, kka, s0_pairs, lw["r_k"], lw["gn_a_g"], lw["gn_a_b"])


def _wkv_to_pairs(s):
    bsz = s.shape[0]
    s = s.reshape(bsz, A_HEADS // 2, 2, A_HEAD_DIM, A_HEAD_DIM)
    z = jnp.zeros_like(s[:, :, 0])
    top = jnp.concatenate([s[:, :, 0], z], axis=-1)
    bot = jnp.concatenate([z, s[:, :, 1]], axis=-1)
    return jnp.concatenate([top, bot], axis=-2)


def _wkv_from_pairs(sp):
    bsz = sp.shape[0]
    d = A_HEAD_DIM
    s = jnp.stack([sp[:, :, :d, :d], sp[:, :, d:, d:]], axis=2)
    return s.reshape(bsz, A_HEADS, d, d)


def _mlstm_kernel(qk_ref, v_ref, if_ref, o_ref, buf0_ref, c0_ref, n0_ref, m0_ref, cw_ref, cb_ref, hng_ref,
                  y_o, buf_o, c_o, n_o, m_o, ext_scr, c_scr, n_scr, m_scr):
    t = pl.program_id(1)
    L = CHUNK
    tt = qk_ref.shape[1]
    pad = SUBLANES

    @pl.when(t == 0)
    def _():
        ext_scr[0:pad, :] = buf0_ref[0]
        c_scr[...] = c0_ref[0]
        n_scr[...] = n0_ref[0]
        m_scr[...] = m0_ref[0]

    ext_scr[pad:pad + tt, :] = qk_ref[0]
    ri = lax.broadcasted_iota(jnp.int32, (L, L), 0)
    ci = lax.broadcasted_iota(jnp.int32, (L, L), 1)
    incl = ri >= ci
    eye = ri == ci
    tri = jnp.where(incl, 1.0, 0.0).astype(BF16)
    k_scale = B_HEAD_DIM ** -0.5

    def chunk(c, carry):
        base = pl.multiple_of(c * L, L)
        rows = pl.ds(base, L)
        x = ext_scr[pl.ds(base, L + pad), :]
        conv = cb_ref[...] + x[pad:, :] * cw_ref[B_CONV - 1:B_CONV, :]
        for j in range(B_CONV - 1):
            conv = conv + pltpu.roll(x, B_CONV - 1 - j, 0)[pad:, :] * cw_ref[j:j + 1, :]
        qk = _silu(conv)
        ifg = if_ref[0, rows, :]
        lf = -_softplus(-ifg)
        bcum = _mm_exact_rhs(tri, lf)
        for h in range(B_HEADS):
            sl = slice(h * B_HEAD_DIM, (h + 1) * B_HEAD_DIM)
            q_h = qk[:, sl]
            k_h = qk[:, B_WIDTH + h * B_HEAD_DIM:B_WIDTH + (h + 1) * B_HEAD_DIM] * k_scale
            v_h = v_ref[0, rows, sl]
            icol = ifg[:, h:h + 1]
            bcol = bcum[:, B_HEADS + h:B_HEADS + h + 1]
            dcol = icol - bcol
            drow = jnp.sum(jnp.where(eye, dcol, 0.0), axis=0, keepdims=True)
            dmat = jnp.where(incl, bcol + drow, -jnp.inf)
            m_prev = m_scr[h][:, 0:1]
            inter = bcol + m_prev
            m_t = jnp.maximum(inter, jnp.max(dmat, axis=-1, keepdims=True))
            wmat = jnp.exp(dmat - m_t) * _mm(q_h, k_h, NT)
            scale = jnp.exp(inter - m_t)
            c_h = c_scr[h]
            n_h = n_scr[h]
            num = _mm(wmat, v_h) + scale * _mm(q_h, c_h, NT)
            den = jnp.sum(wmat, axis=-1, keepdims=True) + scale * jnp.sum(q_h * n_h, axis=-1, keepdims=True)
            hid = num / jnp.maximum(jnp.abs(den), jnp.exp(-m_t))
            m_new = m_t[L - 1:L, :]
            gcol = jnp.exp(bcol[L - 1:L, :] + dcol - m_new)
            c_sc = jnp.exp(inter[L - 1:L, :] - m_new)
            c_scr[h] = c_sc * c_h + _mm(gcol * v_h, k_h, TN)
            n_scr[h] = c_sc * n_h + jnp.sum(gcol * k_h, axis=0, keepdims=True)
            m_scr[h] = jnp.broadcast_to(m_new, (1, LANES))
            y_o[0, rows, sl] = _sigmoid(o_ref[0, rows, sl]) * (_norm_rows(hid, HN_EPS) * hng_ref[:, sl])
        return carry

    lax.fori_loop(0, tt // L, chunk, 0)
    last = ext_scr[tt:tt + pad, :]
    ext_scr[0:pad, :] = last
    buf_o[0] = last
    c_o[0] = c_scr[...]
    n_o[0] = n_scr[...]
    m_o[0] = m_scr[...]


def _mlstm_call(qk, v, ifg, o, buf0, c0, n0, m0, lw):
    bsz, t, _ = v.shape
    tt = _tile(t, 512)
    blk = lambda w: pl.BlockSpec((1, tt, w), lambda b, i: (b, i, 0))
    fix = lambda *s: pl.BlockSpec((1,) + s, lambda b, i: (b,) + (0,) * len(s))
    return pl.pallas_call(
        _mlstm_kernel,
        out_shape=(jax.ShapeDtypeStruct((bsz, t, B_WIDTH), F32),
                   jax.ShapeDtypeStruct((bsz, SUBLANES, 2 * B_WIDTH), F32),
                   jax.ShapeDtypeStruct((bsz, B_HEADS, B_HEAD_DIM, B_HEAD_DIM), F32),
                   jax.ShapeDtypeStruct((bsz, B_HEADS, 1, B_HEAD_DIM), F32),
                   jax.ShapeDtypeStruct((bsz, B_HEADS, 1, LANES), F32)),
        grid=(bsz, t // tt),
        in_specs=[blk(2 * B_WIDTH), blk(B_WIDTH), blk(LANES), blk(B_WIDTH),
                  fix(SUBLANES, 2 * B_WIDTH), fix(B_HEADS, B_HEAD_DIM, B_HEAD_DIM), fix(B_HEADS, 1, B_HEAD_DIM),
                  fix(B_HEADS, 1, LANES),
                  _const_spec((B_CONV, 2 * B_WIDTH)), _const_spec((1, 2 * B_WIDTH)), _const_spec((1, B_WIDTH))],
        out_specs=(blk(B_WIDTH), fix(SUBLANES, 2 * B_WIDTH), fix(B_HEADS, B_HEAD_DIM, B_HEAD_DIM),
                   fix(B_HEADS, 1, B_HEAD_DIM), fix(B_HEADS, 1, LANES)),
        scratch_shapes=[pltpu.VMEM((tt + SUBLANES, 2 * B_WIDTH), F32),
                        pltpu.VMEM((B_HEADS, B_HEAD_DIM, B_HEAD_DIM), F32),
                        pltpu.VMEM((B_HEADS, 1, B_HEAD_DIM), F32),
                        pltpu.VMEM((B_HEADS, 1, LANES), F32)],
        compiler_params=_params("parallel", "arbitrary"),
        name="mlstm",
    )(qk, v, ifg, o, buf0, c0, n0, m0, lw["conv_b_w"], lw["conv_b_b"], lw["hn_b_g"])


def _sb_block(q_m, k_blk, v_blk, c, acc, cum_mat, mask):
    z = _dg(q_m, k_blk, NT)
    sp = _softplus(z)
    lk = -sp
    if mask is not None:
        lk = jnp.where(mask, lk, 0.0)
    h1 = lk.astype(BF16)
    h2 = (lk - h1.astype(F32)).astype(BF16)
    cs = _dg(h1, cum_mat) + _dg(h2, cum_mat)
    a = jnp.exp((z - sp) + cs[:, :LANES] + c)
    if mask is not None:
        a = jnp.where(mask, a, 0.0)
    acc = acc + _dg(a.astype(BF16), v_blk)
    return c + cs[:, LANES:], acc


def _sb_kernel(*refs, n_past_blocks):
    if n_past_blocks:
        q_ref, k_ref, v_ref, kp_ref, vp_ref, o_ref, c_scr, acc_scr = refs
    else:
        q_ref, k_ref, v_ref, o_ref, c_scr, acc_scr = refs
    qi = pl.program_id(1)
    qb = q_ref.shape[1]
    kb = LANES
    n_pairs = C_HEADS // 2

    rj = lax.broadcasted_iota(jnp.int32, (kb, 2 * kb), 0)
    cj = lax.broadcasted_iota(jnp.int32, (kb, 2 * kb), 1)
    cum_mat = jnp.where((cj >= kb) | (rj > cj), 1.0, 0.0).astype(BF16)
    qpos = lax.broadcasted_iota(jnp.int32, (qb, kb), 0)
    kpos = lax.broadcasted_iota(jnp.int32, (qb, kb), 1)
    diag_mask = kpos < qpos
    lane = lax.broadcasted_iota(jnp.int32, (1, LANES), 1)
    head_masks = (lane < C_HEAD_DIM, lane >= C_HEAD_DIM)
    q = q_ref[0] * (C_HEAD_DIM ** -0.5)

    def q_heads(p):
        qs = q[:, p * LANES:(p + 1) * LANES]
        return [jnp.where(hm, qs, 0.0).astype(BF16) for hm in head_masks]

    def pad_rows(x):
        if qb == kb:
            return x
        return jnp.concatenate([x, jnp.zeros((kb - qb, x.shape[1]), x.dtype)], axis=0)

    row0 = pl.multiple_of(qi * qb, qb)
    for p in range(n_pairs):
        sl = slice(p * LANES, (p + 1) * LANES)
        k_blk = pad_rows(k_ref[0, pl.ds(row0, qb), sl]).astype(BF16)
        v_blk = pad_rows(v_ref[0, pl.ds(row0, qb), sl]).astype(BF16)
        for hh, q_m in enumerate(q_heads(p)):
            zero = jnp.zeros((qb, LANES), F32)
            c, acc = _sb_block(q_m, k_blk, v_blk, zero, zero, cum_mat, diag_mask)
            c_scr[2 * p + hh] = c
            acc_scr[2 * p + hh] = acc

    def make_step(kr, vr, first_block):
        def step(j, carry):
            blk = first_block - j
            rows = pl.ds(pl.multiple_of(blk * kb, kb), kb)
            for p in range(n_pairs):
                sl = slice(p * LANES, (p + 1) * LANES)
                k_blk = kr[0, rows, sl].astype(BF16)
                v_blk = vr[0, rows, sl].astype(BF16)
                for hh, q_m in enumerate(q_heads(p)):
                    c, acc = _sb_block(q_m, k_blk, v_blk, c_scr[2 * p + hh], acc_scr[2 * p + hh], cum_mat, None)
                    c_scr[2 * p + hh] = c
                    acc_scr[2 * p + hh] = acc
            return carry
        return step

    if qb == kb:
        lax.fori_loop(0, qi, make_step(k_ref, v_ref, qi - 1), 0)
    if n_past_blocks:
        lax.fori_loop(0, n_past_blocks, make_step(kp_ref, vp_ref, n_past_blocks - 1), 0)

    for p in range(n_pairs):
        o_ref[0, :, p * LANES:(p + 1) * LANES] = jnp.where(head_masks[0], acc_scr[2 * p], acc_scr[2 * p + 1])


def _sb_call(q, k, v, k_past=None, v_past=None):
    bsz, t, _ = q.shape
    qb = min(LANES, t)
    assert t % qb == 0 and (qb == LANES or t == qb)
    n_past = 0 if k_past is None else k_past.shape[1]
    assert n_past % LANES == 0
    qblk = pl.BlockSpec((1, qb, C_WIDTH), lambda b, i: (b, i, 0))
    full = lambda n: pl.BlockSpec((1, n, C_WIDTH), lambda b, i: (b, 0, 0))
    in_specs = [qblk, full(t), full(t)]
    args = [q, k, v]
    if n_past:
        in_specs += [full(n_past), full(n_past)]
        args += [k_past, v_past]
    return pl.pallas_call(
        functools.partial(_sb_kernel, n_past_blocks=n_past // LANES),
        out_shape=jax.ShapeDtypeStruct((bsz, t, C_WIDTH), F32),
        grid=(bsz, t // qb),
        in_specs=in_specs,
        out_specs=qblk,
        scratch_shapes=[pltpu.VMEM((C_HEADS, qb, LANES), F32), pltpu.VMEM((C_HEADS, qb, LANES), F32)],
        compiler_params=_params("parallel", "arbitrary"),
        name="stick_breaking",
    )(*args)


def _merge_kernel(x_ref, ya_ref, yb_ref, yc_ref, p_ref, wz_ref, bz_ref, wg_ref, bg_ref, wbr_ref, wout_ref,
                  lng_ref, lnb_ref, wple_ref, wpg_ref, o_ref):
    x = x_ref[...]
    xb = x.astype(BF16)
    width = A_WIDTH
    mix_pre = jnp.zeros(x.shape, F32)
    for n, y_ref in enumerate((ya_ref, yb_ref, yc_ref)):
        z = _dg(xb, wz_ref[:, n * width:(n + 1) * width]) + bz_ref[:, n * width:(n + 1) * width]
        ys = y_ref[...] * _silu(z)
        br = _mm(ys, wbr_ref[n])
        gate = _dg(xb, wg_ref[:, n * D_MODEL:(n + 1) * D_MODEL]) + bg_ref[:, n * D_MODEL:(n + 1) * D_MODEL]
        mix_pre = mix_pre + _sigmoid(gate) * br
    mix = _mm(mix_pre, wout_ref[...])
    x1 = _norm_rows(DEEPNORM_ALPHA * x + mix, LN_EPS) * lng_ref[...] + lnb_ref[...]
    o_ref[...] = x1 + _mm(p_ref[...], wple_ref[...]) * _sigmoid(_mm(x1, wpg_ref[...]))


def _merge_call(x, ya, yb, yc, p, lw):
    n, d = x.shape
    tm = _tile(n, 256)
    rows = lambda w: pl.BlockSpec((tm, w), lambda i: (i, 0))
    return pl.pallas_call(
        _merge_kernel,
        out_shape=jax.ShapeDtypeStruct((n, d), F32),
        grid=(n // tm,),
        in_specs=[rows(d), rows(A_WIDTH), rows(B_WIDTH), rows(C_WIDTH), rows(D_PLE),
                  _const_spec((d, N_BRANCH * A_WIDTH)), _const_spec((1, N_BRANCH * A_WIDTH)),
                  _const_spec((d, N_BRANCH * d)), _const_spec((1, N_BRANCH * d)),
                  _const_spec((N_BRANCH, A_WIDTH, d)), _const_spec((d, d)),
                  _const_spec((1, d)), _const_spec((1, d)), _const_spec((D_PLE, d)), _const_spec((d, d))],
        out_specs=rows(d),
        compiler_params=_params("parallel"),
        name="merge",
    )(x, ya, yb, yc, p, lw["w_z"], lw["b_z"], lw["w_gates"], lw["b_gates"], lw["w_branch"], lw["w_out"],
      lw["ln_g"], lw["ln_b"], lw["w_ple"], lw["w_ple_gate"])


def _cols(w, *ranges):
    return jnp.concatenate([w[..., a:b] for a, b in ranges], axis=-1)


def _layer_params(i, w_in, b_in, prm):
    w = w_in[i]
    b = b_in[i]
    mix_ranges = ((_OFF_A_COLS, _OFF_A_Z), (_OFF_B_QK, _OFF_B_V), (_OFF_B_V, _OFF_B_I), (_OFF_B_O, _OFF_B_Z),
                  (_OFF_C_Q, _OFF_C_K), (_OFF_C_K, _OFF_C_V), (_OFF_C_V, _OFF_C_Z), (_OFF_B_I, _OFF_B_O))
    pad = LANES - 2 * B_HEADS
    w_mix = jnp.pad(_cols(w, *mix_ranges), ((0, 0), (0, pad)))
    b_mix = jnp.pad(_cols(b, *mix_ranges), ((0, pad),))
    z_ranges = ((_OFF_A_Z, _OFF_B_QK), (_OFF_B_Z, _OFF_C_Q), (_OFF_C_Z, _OFF_GATES))
    zeros_lora = jnp.zeros((A_LORA, A_WIDTH), F32)
    hid = jnp.arange(A_WIDTH) // A_HEAD_DIM
    row = lambda name: prm[name][i].reshape(1, -1)
    return {
        "w_mix": w_mix.astype(BF16), "b_mix": b_mix.reshape(1, -1),
        "w_z": _cols(w, *z_ranges).astype(BF16), "b_z": _cols(b, *z_ranges).reshape(1, -1),
        "w_gates": w[:, _OFF_GATES:].astype(BF16), "b_gates": b[_OFF_GATES:].reshape(1, -1),
        "mu_a": row("mu_a"), "w0_a": row("w0_a"), "a0_a": row("a0_a"), "k_k": row("k_k"), "k_a": row("k_a"),
        "wup_pad": jnp.concatenate([prm["w_decay_up"][i], zeros_lora], axis=0).astype(BF16),
        "aup_pad": jnp.concatenate([zeros_lora, prm["w_iclr_up"][i]], axis=0).astype(BF16),
        "head_sum": (hid[:, None] == hid[None, :]).astype(BF16),
        "r_k": row("r_k"), "gn_a_g": row("gn_a_g"), "gn_a_b": row("gn_a_b"),
        "conv_b_w": prm["conv_b_w"][i], "conv_b_b": row("conv_b_b"), "hn_b_g": row("hn_b_g"),
        "w_branch": prm["w_branch"][i].astype(BF16), "w_out": prm["w_out"][i].astype(BF16),
        "ln_g": row("ln_g"), "ln_b": row("ln_b"),
        "w_ple": prm["w_ple"][i].astype(BF16), "w_ple_gate": prm["w_ple_gate"][i].astype(BF16),
    }


def _run_group(x, p, init, ln_in_g, ln_in_b, layers):
    bsz, t, d = x.shape
    n = bsz * t
    shift0, wkv0, conv0, c0, n0, m0, k_past, v_past = init
    xf = _layer_norm_call(x.reshape(n, d), ln_in_g, ln_in_b)
    new = [[] for _ in range(8)]
    for i, lw in enumerate(layers):
        a_cols, b_qk, b_v, b_o, c_q, c_k, c_v, b_if = _inproj_call(xf, lw["w_mix"], lw["b_mix"])
        seq = lambda a: a.reshape(bsz, t, a.shape[-1])
        r, lwd, kh, v, kk, kka, last = _rwkv_pre_call(seq(a_cols), shift0[i], lw)
        ya, s_pairs = _rwkv_rec_call(r, lwd, kh, v, kk, kka, _wkv_to_pairs(wkv0[i]), lw)
        buf0 = jnp.pad(conv0[i], ((0, 0), (SUBLANES - (B_CONV - 1), 0), (0, 0)))
        m0b = jnp.broadcast_to(m0[i][:, :, None, None], (bsz, B_HEADS, 1, LANES))
        yb, buf, c1, n1, m1 = _mlstm_call(seq(b_qk), seq(b_v), seq(b_if), seq(b_o), buf0, c0[i],
                                          n0[i][:, :, None, :], m0b, lw)
        if k_past is None:
            yc = _sb_call(seq(c_q), seq(c_k), seq(c_v))
        else:
            past = lambda a: a.reshape(bsz, a.shape[1], C_WIDTH)
            yc = _sb_call(seq(c_q), seq(c_k), seq(c_v), past(k_past[i]), past(v_past[i]))
        xf = _merge_call(xf, ya.reshape(n, A_WIDTH), yb.reshape(n, B_WIDTH), yc.reshape(n, C_WIDTH),
                         p[i].reshape(n, D_PLE), lw)
        states = (last[:, 0], _wkv_from_pairs(s_pairs), buf[:, SUBLANES - (B_CONV - 1):], c1, n1[:, :, 0],
                  m1[:, :, 0, 0], c_k.reshape(bsz, t, C_HEADS, C_HEAD_DIM), c_v.reshape(bsz, t, C_HEADS, C_HEAD_DIM))
        for lst, s in zip(new, states):
            lst.append(s)
    return xf.reshape(bsz, t, d), [jnp.stack(lst) for lst in new]


def kernel(x_prompt, x_sample, state_shift_a, state_wkv, state_conv_b, state_mlstm_c, state_mlstm_n, state_mlstm_m, cache_sb_k, cache_sb_v, p_prompt, p_sample, ln_in_g, ln_in_b, w_in, b_in, mu_a, w0_a, w_decay_up, a0_a, w_iclr_up, k_k, k_a, r_k, gn_a_g, gn_a_b, conv_b_w, conv_b_b, hn_b_g, w_branch, w_out, ln_g, ln_b, w_ple, w_ple_gate):
    prm = dict(mu_a=mu_a, w0_a=w0_a, w_decay_up=w_decay_up, a0_a=a0_a, w_iclr_up=w_iclr_up, k_k=k_k, k_a=k_a,
               r_k=r_k, gn_a_g=gn_a_g, gn_a_b=gn_a_b, conv_b_w=conv_b_w, conv_b_b=conv_b_b, hn_b_g=hn_b_g,
               w_branch=w_branch, w_out=w_out, ln_g=ln_g, ln_b=ln_b, w_ple=w_ple, w_ple_gate=w_ple_gate)
    layers = [_layer_params(i, w_in, b_in, prm) for i in range(DEPTH)]
    bp = x_prompt.shape[0]
    zeros = lambda *s: jnp.zeros((DEPTH, bp) + s, F32)
    init_prompt = (zeros(A_SHIFT), zeros(A_HEADS, A_HEAD_DIM, A_HEAD_DIM), zeros(B_CONV - 1, 2 * B_WIDTH),
                   zeros(B_HEADS, B_HEAD_DIM, B_HEAD_DIM), zeros(B_HEADS, B_HEAD_DIM), zeros(B_HEADS), None, None)
    init_sample = (state_shift_a, state_wkv, state_conv_b, state_mlstm_c, state_mlstm_n, state_mlstm_m,
                   cache_sb_k, cache_sb_v)
    y_prompt, sp = _run_group(x_prompt, p_prompt, init_prompt, ln_in_g, ln_in_b, layers)
    y_sample, ss = _run_group(x_sample, p_sample, init_sample, ln_in_g, ln_in_b, layers)
    return (y_prompt, y_sample, sp[0], ss[0], sp[1], ss[1], sp[2], ss[2], sp[3], ss[3],
            sp[4], ss[4], sp[5], ss[5], sp[6], ss[6], sp[7], ss[7])
```

```python
import functools

import jax
import jax.numpy as jnp
from jax import lax
from jax.experimental import pallas as pl
from jax.experimental.pallas import tpu as pltpu

F32 = jnp.float32
BF16 = jnp.bfloat16

D_MODEL = 1024
DEPTH = 4
CHUNK = 64
D_PLE = 256
A_HEADS = 8
A_HEAD_DIM = 64
A_WIDTH = A_HEADS * A_HEAD_DIM
A_LORA = 64
A_SHIFT = 3 * A_WIDTH + 2 * A_LORA
B_HEADS = 4
B_HEAD_DIM = 128
B_WIDTH = B_HEADS * B_HEAD_DIM
B_CONV = 4
C_HEADS = 8
C_HEAD_DIM = 64
C_WIDTH = C_HEADS * C_HEAD_DIM
N_BRANCH = 3
DEEPNORM_ALPHA = (2 * DEPTH) ** 0.25
LN_EPS = 1e-5
GN_EPS_A = 64e-5
HN_EPS = 1e-6

LANES = 128
SUBLANES = 8
VMEM_LIMIT_BYTES = 56 * 1024 * 1024

_OFF_A_COLS = 0
_OFF_A_Z = _OFF_A_COLS + A_SHIFT
_OFF_B_QK = _OFF_A_Z + A_WIDTH
_OFF_B_V = _OFF_B_QK + 2 * B_WIDTH
_OFF_B_I = _OFF_B_V + B_WIDTH
_OFF_B_F = _OFF_B_I + B_HEADS
_OFF_B_O = _OFF_B_F + B_HEADS
_OFF_B_Z = _OFF_B_O + B_WIDTH
_OFF_C_Q = _OFF_B_Z + B_WIDTH
_OFF_C_K = _OFF_C_Q + C_WIDTH
_OFF_C_V = _OFF_C_K + C_WIDTH
_OFF_C_Z = _OFF_C_V + C_WIDTH
_OFF_GATES = _OFF_C_Z + C_WIDTH
_N_IN = _OFF_GATES + N_BRANCH * D_MODEL

_MIX_WIDTHS = (A_SHIFT, 2 * B_WIDTH, B_WIDTH, B_WIDTH, C_WIDTH, C_WIDTH, C_WIDTH, LANES)

NN = ((1,), (0,))
NT = ((1,), (1,))
TN = ((0,), (0,))


def _dg(a, b, dims=NN):
    return lax.dot_general(a, b, (dims, ((), ())), preferred_element_type=F32)


def _mm(a, b, dims=NN):
    return _dg(a.astype(BF16), b.astype(BF16), dims)


def _split3(x):
    h1 = x.astype(BF16)
    r1 = x - h1.astype(F32)
    h2 = r1.astype(BF16)
    h3 = (r1 - h2.astype(F32)).astype(BF16)
    return h1, h2, h3


def _mm_exact_rhs(lhs_bf16, x):
    h1, h2, h3 = _split3(x)
    return _dg(lhs_bf16, h1) + (_dg(lhs_bf16, h2) + _dg(lhs_bf16, h3))


def _mm_exact_lhs(x, rhs_bf16):
    h1, h2, h3 = _split3(x)
    return _dg(h1, rhs_bf16) + (_dg(h2, rhs_bf16) + _dg(h3, rhs_bf16))


def _softplus(x):
    return jnp.maximum(x, 0.0) + jnp.log1p(jnp.exp(-jnp.abs(x)))


def _sigmoid(x):
    return jax.nn.sigmoid(x)


def _silu(x):
    return x * jax.nn.sigmoid(x)


def _norm_rows(x, eps):
    mu = jnp.mean(x, axis=-1, keepdims=True)
    xc = x - mu
    var = jnp.mean(xc * xc, axis=-1, keepdims=True)
    return xc * lax.rsqrt(var + eps)


def _tile(n, pref):
    if n <= pref:
        return n
    t = pref - pref % SUBLANES
    while n % t:
        t -= SUBLANES
    return t


def _params(*sem):
    return pltpu.CompilerParams(dimension_semantics=sem, vmem_limit_bytes=VMEM_LIMIT_BYTES)


def _const_spec(shape):
    nd = len(shape)
    return pl.BlockSpec(shape, lambda *_: (0,) * nd)


def _ln_kernel(x_ref, g_ref, b_ref, o_ref):
    o_ref[...] = _norm_rows(x_ref[...], LN_EPS) * g_ref[...] + b_ref[...]


def _layer_norm_call(x, g, b):
    n, d = x.shape
    tm = _tile(n, 512)
    return pl.pallas_call(
        _ln_kernel,
        out_shape=jax.ShapeDtypeStruct((n, d), F32),
        grid=(n // tm,),
        in_specs=[pl.BlockSpec((tm, d), lambda i: (i, 0)), _const_spec((1, d)), _const_spec((1, d))],
        out_specs=pl.BlockSpec((tm, d), lambda i: (i, 0)),
        compiler_params=_params("parallel"),
        name="ln_in",
    )(x, g.reshape(1, d), b.reshape(1, d))


def _inproj_kernel(x_ref, w_ref, b_ref, *out_refs):
    xb = x_ref[...].astype(BF16)
    off = 0
    for o_ref, width in zip(out_refs, _MIX_WIDTHS):
        o_ref[...] = _dg(xb, w_ref[:, off:off + width]) + b_ref[:, off:off + width]
        off += width


def _inproj_call(x, w, b):
    n, d = x.shape
    tm = _tile(n, 256)
    wtot = sum(_MIX_WIDTHS)
    return pl.pallas_call(
        _inproj_kernel,
        out_shape=tuple(jax.ShapeDtypeStruct((n, wd), F32) for wd in _MIX_WIDTHS),
        grid=(n // tm,),
        in_specs=[pl.BlockSpec((tm, d), lambda i: (i, 0)), _const_spec((d, wtot)), _const_spec((1, wtot))],
        out_specs=tuple(pl.BlockSpec((tm, wd), lambda i: (i, 0)) for wd in _MIX_WIDTHS),
        compiler_params=_params("parallel"),
        name="inproj",
    )(x, w, b)


def _rwkv_pre_kernel(cols_ref, last_ref, mu_ref, w0_ref, wup_ref, a0_ref, aup_ref, kk_ref, ka_ref, hsum_ref,
                     r_o, lw_o, kh_o, v_o, kkn_o, kka_o, last_o, prev_scr):
    t = pl.program_id(1)

    @pl.when(t == 0)
    def _():
        prev_scr[...] = last_ref[0]

    u = cols_ref[0]
    tt = u.shape[0]
    row = lax.broadcasted_iota(jnp.int32, u.shape, 0)
    prev = jnp.where(row == 0, prev_scr[...], pltpu.roll(u, 1, 0))
    prev_scr[...] = u[tt - 1:tt, :]
    last_o[0] = u[tt - 1:tt, :]
    xs = u + (prev - u) * mu_ref[...]
    r = xs[:, 0:A_WIDTH]
    k = xs[:, A_WIDTH:2 * A_WIDTH]
    v = xs[:, 2 * A_WIDTH:3 * A_WIDTH]
    lora = xs[:, 3 * A_WIDTH:]
    w_log = -_softplus(-(w0_ref[...] + _mm(jnp.tanh(lora), wup_ref[...]))) - 0.5
    a = _sigmoid(a0_ref[...] + _mm(lora, aup_ref[...]))
    kk = k * kk_ref[...]
    ss = _mm_exact_lhs(kk * kk, hsum_ref[...])
    kk = kk / jnp.maximum(jnp.sqrt(ss), 1e-12)
    r_o[0] = r
    lw_o[0] = -jnp.exp(w_log)
    kh_o[0] = k * (1.0 + (a - 1.0) * ka_ref[...])
    v_o[0] = v
    kkn_o[0] = kk
    kka_o[0] = kk * a


def _rwkv_pre_call(cols, last, lw):
    bsz, t, _ = cols.shape
    tt = _tile(t, 256)
    blk = lambda w: pl.BlockSpec((1, tt, w), lambda b, i: (b, i, 0))
    row = lambda w: pl.BlockSpec((1, 1, w), lambda b, i: (b, 0, 0))
    outs = pl.pallas_call(
        _rwkv_pre_kernel,
        out_shape=tuple(jax.ShapeDtypeStruct((bsz, t, A_WIDTH), F32) for _ in range(6))
        + (jax.ShapeDtypeStruct((bsz, 1, A_SHIFT), F32),),
        grid=(bsz, t // tt),
        in_specs=[blk(A_SHIFT), row(A_SHIFT), _const_spec((1, A_SHIFT)), _const_spec((1, A_WIDTH)),
                  _const_spec((2 * A_LORA, A_WIDTH)), _const_spec((1, A_WIDTH)), _const_spec((2 * A_LORA, A_WIDTH)),
                  _const_spec((1, A_WIDTH)), _const_spec((1, A_WIDTH)), _const_spec((A_WIDTH, A_WIDTH))],
        out_specs=tuple(blk(A_WIDTH) for _ in range(6)) + (row(A_SHIFT),),
        scratch_shapes=[pltpu.VMEM((1, A_SHIFT), F32)],
        compiler_params=_params("parallel", "arbitrary"),
        name="rwkv_pre",
    )(cols, last.reshape(bsz, 1, A_SHIFT), lw["mu_a"], lw["w0_a"], lw["wup_pad"], lw["a0_a"], lw["aup_pad"],
      lw["k_k"], lw["k_a"], lw["head_sum"])
    return outs


def _rwkv_rec_kernel(r_ref, lw_ref, kh_ref, v_ref, kk_ref, kka_ref, s0_ref, rk_ref, gng_ref, gnb_ref,
                     y_o, s_o, s_scr):
    t = pl.program_id(1)
    L = CHUNK
    tt = r_ref.shape[1]
    n_pairs = A_HEADS // 2

    @pl.when(t == 0)
    def _():
        s_scr[...] = s0_ref[0]

    ri = lax.broadcasted_iota(jnp.int32, (L, L), 0)
    ci = lax.broadcasted_iota(jnp.int32, (L, L), 1)
    strict = ri > ci
    incl = ri >= ci
    tri = jnp.where(incl, 1.0, 0.0).astype(BF16)
    eye = jnp.where(ri == ci, 1.0, 0.0)
    lane = lax.broadcasted_iota(jnp.int32, (1, LANES), 1)
    head_masks = (jnp.where(lane < A_HEAD_DIM, 1.0, 0.0), jnp.where(lane >= A_HEAD_DIM, 1.0, 0.0))
    inv_d = 1.0 / A_HEAD_DIM

    def chunk(c, carry):
        rows = pl.ds(pl.multiple_of(c * L, L), L)
        lw = lw_ref[0, rows, :]
        cin = _mm_exact_rhs(tri, lw)
        ein = jnp.exp(cin)
        eex = jnp.exp(cin - lw)
        einv = jnp.exp(-cin)
        r = r_ref[0, rows, :]
        kh = kh_ref[0, rows, :]
        v = v_ref[0, rows, :]
        at = -kk_ref[0, rows, :] * eex
        bt = kka_ref[0, rows, :] * einv
        kt = kh * einv
        rt = r * ein
        wl = ein[L - 1:L, :]
        btw = bt * wl
        ktw = kt * wl
        bonus_w = r * kh * rk_ref[...]
        heads = [(p, hm) for p in range(n_pairs) for hm in head_masks]
        sls = [slice(p * LANES, (p + 1) * LANES) for p, _ in heads]
        s_pairs = [s_scr[p] for p in range(n_pairs)]
        s_pairs_b = [s.astype(BF16) for s in s_pairs]
        at_h = [at[:, sl] * hm for sl, (_, hm) in zip(sls, heads)]
        rt_h = [rt[:, sl] * hm for sl, (_, hm) in zip(sls, heads)]
        v_h = [v[:, sl] * hm for sl, (_, hm) in zip(sls, heads)]
        lhs = [jnp.concatenate([a, b], axis=0).astype(BF16) for a, b in zip(at_h, rt_h)]
        sc_b = [_dg(x, bt[:, sl].astype(BF16), NT) for x, sl in zip(lhs, sls)]
        sc_k = [_dg(x, kt[:, sl].astype(BF16), NT) for x, sl in zip(lhs, sls)]
        a_ab = [jnp.where(strict, s[:L], 0.0) for s in sc_b]
        a_ak = [jnp.where(strict, s[:L], 0.0) for s in sc_k]
        m_rb = [jnp.where(incl, s[L:], 0.0) for s in sc_b]
        m_rk = [jnp.where(incl, s[L:], 0.0) for s in sc_k]
        tm = [eye + a for a in a_ab]
        ap = a_ab
        for _ in range(5):
            ap = [_mm(a, a) for a in ap]
            tm = [x + _mm(x, a) for x, a in zip(tm, ap)]
        x1 = [_mm(jnp.concatenate([a, m], axis=0), vh) for a, m, vh in zip(a_ak, m_rk, v_h)]
        tmb = [x.astype(BF16) for x in tm]
        w_h = [_dg(x, a.astype(BF16)) for x, a in zip(tmb, at_h)]
        u0 = [_dg(x, y[:L].astype(BF16)) for x, y in zip(tmb, x1)]
        st = [_dg(jnp.concatenate([w, x], axis=0).astype(BF16), s_pairs_b[p], NT)
              for w, x, (p, _) in zip(w_h, rt_h, heads)]
        u_h = [u + s[:L] for u, s in zip(u0, st)]
        y_h = [s[L:] + x[L:] + _mm(m, u) for s, x, m, u in zip(st, x1, m_rb, u_h)]
        for p in range(n_pairs):
            sl = slice(p * LANES, (p + 1) * LANES)
            i0, i1 = 2 * p, 2 * p + 1
            upd = _mm(jnp.concatenate([u_h[i0], v_h[i0], u_h[i1], v_h[i1]], axis=0),
                      jnp.concatenate([btw[:, sl] * head_masks[0], ktw[:, sl] * head_masks[0],
                                       btw[:, sl] * head_masks[1], ktw[:, sl] * head_masks[1]], axis=0), TN)
            s_scr[p] = s_pairs[p] * wl[:, sl] + upd
            y_pair = y_h[i0] + y_h[i1]
            normed = jnp.zeros((L, LANES), F32)
            bonus = jnp.zeros((L, LANES), F32)
            for hm in head_masks:
                mu = jnp.sum(y_pair * hm, axis=-1, keepdims=True) * inv_d
                yc = (y_pair - mu) * hm
                var = jnp.sum(yc * yc, axis=-1, keepdims=True) * inv_d
                normed = normed + yc * lax.rsqrt(var + GN_EPS_A)
                bonus = bonus + jnp.sum(bonus_w[:, sl] * hm, axis=-1, keepdims=True) * (v[:, sl] * hm)
            y_o[0, rows, sl] = (normed * gng_ref[:, sl] + gnb_ref[:, sl]) + bonus
        return carry

    lax.fori_loop(0, tt // L, chunk, 0)
    s_o[0] = s_scr[...]


def _rwkv_rec_call(r, lwd, kh, v, kk, kka, s0_pairs, lw):
    bsz, t, _ = r.shape
    tt = _tile(t, 512)
    blk = pl.BlockSpec((1, tt, A_WIDTH), lambda b, i: (b, i, 0))
    st = pl.BlockSpec((1, A_HEADS // 2, LANES, LANES), lambda b, i: (b, 0, 0, 0))
    return pl.pallas_call(
        _rwkv_rec_kernel,
        out_shape=(jax.ShapeDtypeStruct((bsz, t, A_WIDTH), F32),
                   jax.ShapeDtypeStruct((bsz, A_HEADS // 2, LANES, LANES), F32)),
        grid=(bsz, t // tt),
        in_specs=[blk] * 6 + [st, _const_spec((1, A_WIDTH)), _const_spec((1, A_WIDTH)), _const_spec((1, A_WIDTH))],
        out_specs=(blk, st),
        scratch_shapes=[pltpu.VMEM((A_HEADS // 2, LANES, LANES), F32)],
        compiler_params=_params("parallel", "arbitrary"),
        name="rwkv_rec",
    )(r, lwd, kh, v, kk, kka, s0_pairs, lw["r_k"], lw["gn_a_g"], lw["gn_a_b"])


def _wkv_to_pairs(s):
    bsz = s.shape[0]
    s = s.reshape(bsz, A_HEADS // 2, 2, A_HEAD_DIM, A_HEAD_DIM)
    z = jnp.zeros_like(s[:, :, 0])
    top = jnp.concatenate([s[:, :, 0], z], axis=-1)
    bot = jnp.concatenate([z, s[:, :, 1]], axis=-1)
    return jnp.concatenate([top, bot], axis=-2)


def _wkv_from_pairs(sp):
    bsz = sp.shape[0]
    d = A_HEAD_DIM
    s = jnp.stack([sp[:, :, :d, :d], sp[:, :, d:, d:]], axis=2)
    return s.reshape(bsz, A_HEADS, d, d)


def _mlstm_kernel(qk_ref, v_ref, if_ref, o_ref, buf0_ref, c0_ref, n0_ref, m0_ref, cw_ref, cb_ref, hng_ref,
                  y_o, buf_o, c_o, n_o, m_o, ext_scr, c_scr, n_scr, m_scr):
    t = pl.program_id(1)
    L = CHUNK
    tt = qk_ref.shape[1]
    pad = SUBLANES

    @pl.when(t == 0)
    def _():
        ext_scr[0:pad, :] = buf0_ref[0]
        c_scr[...] = c0_ref[0]
        n_scr[...] = n0_ref[0]
        m_scr[...] = m0_ref[0]

    ext_scr[pad:pad + tt, :] = qk_ref[0]
    ri = lax.broadcasted_iota(jnp.int32, (L, L), 0)
    ci = lax.broadcasted_iota(jnp.int32, (L, L), 1)
    incl = ri >= ci
    eye = ri == ci
    tri = jnp.where(incl, 1.0, 0.0).astype(BF16)
    k_scale = B_HEAD_DIM ** -0.5

    def chunk(c, carry):
        base = pl.multiple_of(c * L, L)
        rows = pl.ds(base, L)
        x = ext_scr[pl.ds(base, L + pad), :]
        conv = cb_ref[...] + x[pad:, :] * cw_ref[B_CONV - 1:B_CONV, :]
        for j in range(B_CONV - 1):
            conv = conv + pltpu.roll(x, B_CONV - 1 - j, 0)[pad:, :] * cw_ref[j:j + 1, :]
        qk = _silu(conv)
        ifg = if_ref[0, rows, :]
        lf = -_softplus(-ifg)
        bcum = _mm_exact_rhs(tri, lf)
        hs = range(B_HEADS)
        sls = [slice(h * B_HEAD_DIM, (h + 1) * B_HEAD_DIM) for h in hs]
        q_h = [qk[:, sl] for sl in sls]
        k_h = [qk[:, B_WIDTH + h * B_HEAD_DIM:B_WIDTH + (h + 1) * B_HEAD_DIM] * k_scale for h in hs]
        v_h = [v_ref[0, rows, sl] for sl in sls]
        c_h = [c_scr[h] for h in hs]
        n_h = [n_scr[h] for h in hs]
        qk_t = [_mm(q, k, NT) for q, k in zip(q_h, k_h)]
        qc = [_mm(q, c, NT) for q, c in zip(q_h, c_h)]
        icol = [ifg[:, h:h + 1] for h in hs]
        bcol = [bcum[:, B_HEADS + h:B_HEADS + h + 1] for h in hs]
        dcol = [i - b for i, b in zip(icol, bcol)]
        drow = [jnp.sum(jnp.where(eye, d, 0.0), axis=0, keepdims=True) for d in dcol]
        dmat = [jnp.where(incl, b + d, -jnp.inf) for b, d in zip(bcol, drow)]
        inter = [b + m_scr[h][:, 0:1] for h, b in zip(hs, bcol)]
        m_t = [jnp.maximum(i, jnp.max(d, axis=-1, keepdims=True)) for i, d in zip(inter, dmat)]
        wmat = [jnp.exp(d - m) * s for d, m, s in zip(dmat, m_t, qk_t)]
        scale = [jnp.exp(i - m) for i, m in zip(inter, m_t)]
        wv = [_mm(w, v) for w, v in zip(wmat, v_h)]
        m_new = [m[L - 1:L, :] for m in m_t]
        gcol = [jnp.exp(b[L - 1:L, :] + d - m) for b, d, m in zip(bcol, dcol, m_new)]
        c_sc = [jnp.exp(i[L - 1:L, :] - m) for i, m in zip(inter, m_new)]
        gvk = [_mm(g * v, k, TN) for g, v, k in zip(gcol, v_h, k_h)]
        for h in hs:
            num = wv[h] + scale[h] * qc[h]
            den = (jnp.sum(wmat[h], axis=-1, keepdims=True)
                   + scale[h] * jnp.sum(q_h[h] * n_h[h], axis=-1, keepdims=True))
            hid = num / jnp.maximum(jnp.abs(den), jnp.exp(-m_t[h]))
            c_scr[h] = c_sc[h] * c_h[h] + gvk[h]
            n_scr[h] = c_sc[h] * n_h[h] + jnp.sum(gcol[h] * k_h[h], axis=0, keepdims=True)
            m_scr[h] = jnp.broadcast_to(m_new[h], (1, LANES))
            y_o[0, rows, sls[h]] = (_sigmoid(o_ref[0, rows, sls[h]])
                                    * (_norm_rows(hid, HN_EPS) * hng_ref[:, sls[h]]))
        return carry

    lax.fori_loop(0, tt // L, chunk, 0)
    last = ext_scr[tt:tt + pad, :]
    ext_scr[0:pad, :] = last
    buf_o[0] = last
    c_o[0] = c_scr[...]
    n_o[0] = n_scr[...]
    m_o[0] = m_scr[...]


def _mlstm_call(qk, v, ifg, o, buf0, c0, n0, m0, lw):
    bsz, t, _ = v.shape
    tt = _tile(t, 512)
    blk = lambda w: pl.BlockSpec((1, tt, w), lambda b, i: (b, i, 0))
    fix = lambda *s: pl.BlockSpec((1,) + s, lambda b, i: (b,) + (0,) * len(s))
    return pl.pallas_call(
        _mlstm_kernel,
        out_shape=(jax.ShapeDtypeStruct((bsz, t, B_WIDTH), F32),
                   jax.ShapeDtypeStruct((bsz, SUBLANES, 2 * B_WIDTH), F32),
                   jax.ShapeDtypeStruct((bsz, B_HEADS, B_HEAD_DIM, B_HEAD_DIM), F32),
                   jax.ShapeDtypeStruct((bsz, B_HEADS, 1, B_HEAD_DIM), F32),
                   jax.ShapeDtypeStruct((bsz, B_HEADS, 1, LANES), F32)),
        grid=(bsz, t // tt),
        in_specs=[blk(2 * B_WIDTH), blk(B_WIDTH), blk(LANES), blk(B_WIDTH),
                  fix(SUBLANES, 2 * B_WIDTH), fix(B_HEADS, B_HEAD_DIM, B_HEAD_DIM), fix(B_HEADS, 1, B_HEAD_DIM),
                  fix(B_HEADS, 1, LANES),
                  _const_spec((B_CONV, 2 * B_WIDTH)), _const_spec((1, 2 * B_WIDTH)), _const_spec((1, B_WIDTH))],
        out_specs=(blk(B_WIDTH), fix(SUBLANES, 2 * B_WIDTH), fix(B_HEADS, B_HEAD_DIM, B_HEAD_DIM),
                   fix(B_HEADS, 1, B_HEAD_DIM), fix(B_HEADS, 1, LANES)),
        scratch_shapes=[pltpu.VMEM((tt + SUBLANES, 2 * B_WIDTH), F32),
                        pltpu.VMEM((B_HEADS, B_HEAD_DIM, B_HEAD_DIM), F32),
                        pltpu.VMEM((B_HEADS, 1, B_HEAD_DIM), F32),
                        pltpu.VMEM((B_HEADS, 1, LANES), F32)],
        compiler_params=_params("parallel", "arbitrary"),
        name="mlstm",
    )(qk, v, ifg, o, buf0, c0, n0, m0, lw["conv_b_w"], lw["conv_b_b"], lw["hn_b_g"])


SB_SKIP_LOG = -105.0


def _sb_kernel(*refs, n_past_blocks):
    if n_past_blocks:
        q_ref, k_ref, v_ref, kp_ref, vp_ref, o_ref, q2_scr, c_scr, acc_scr = refs
    else:
        q_ref, k_ref, v_ref, o_ref, q2_scr, c_scr, acc_scr = refs
    qi = pl.program_id(1)
    qb = q_ref.shape[1]
    kb = LANES
    n_pairs = C_HEADS // 2
    n_rows = C_HEADS * qb

    rj = lax.broadcasted_iota(jnp.int32, (2 * kb, 2 * kb), 0) & (kb - 1)
    cj = lax.broadcasted_iota(jnp.int32, (2 * kb, 2 * kb), 1)
    cum2 = jnp.where((cj >= kb) | (rj > cj), 1.0, 0.0).astype(BF16)
    qpos = lax.broadcasted_iota(jnp.int32, (n_rows, kb), 0) & (qb - 1)
    kpos = lax.broadcasted_iota(jnp.int32, (n_rows, kb), 1)
    diag_mask = kpos < qpos
    lane = lax.broadcasted_iota(jnp.int32, (1, LANES), 1)
    lo_lanes = lane < C_HEAD_DIM
    q = q_ref[0] * (C_HEAD_DIM ** -0.5)
    for p in range(n_pairs):
        qs = q[:, p * LANES:(p + 1) * LANES]
        q2_scr[p] = jnp.concatenate([jnp.where(lo_lanes, qs, 0.0), jnp.where(lo_lanes, 0.0, qs)], axis=0).astype(BF16)

    def pad_rows(x):
        if x.shape[0] == kb:
            return x
        return jnp.concatenate([x, jnp.zeros((kb - x.shape[0], x.shape[1]), x.dtype)], axis=0)

    def block(kr, vr, rows, first):
        cols = [slice(p * LANES, (p + 1) * LANES) for p in range(n_pairs)]
        z = jnp.concatenate([_dg(q2_scr[p], pad_rows(kr[0, rows, cols[p]]).astype(BF16), NT)
                             for p in range(n_pairs)], axis=0)
        sp = _softplus(z)
        lk = jnp.where(diag_mask, -sp, 0.0) if first else -sp
        h1 = lk.astype(BF16)
        h2 = (lk - h1.astype(F32)).astype(BF16)
        cs = _dg(jnp.concatenate([h1, h2], axis=1), cum2)
        if first:
            a = jnp.where(diag_mask, jnp.exp((z - sp) + cs[:, :kb]), 0.0)
            c_new = cs[:, kb:]
        else:
            c_old = c_scr[...]
            a = jnp.exp((z - sp) + cs[:, :kb] + c_old)
            c_new = c_old + cs[:, kb:]
        c_scr[...] = c_new
        ab = a.astype(BF16)
        for p in range(n_pairs):
            rs = slice(p * 2 * qb, (p + 1) * 2 * qb)
            upd = _dg(ab[rs], pad_rows(vr[0, rows, cols[p]]).astype(BF16))
            acc_scr[rs] = upd if first else acc_scr[rs] + upd
        return jnp.max(c_new) > SB_SKIP_LOG

    go = block(k_ref, v_ref, pl.ds(pl.multiple_of(qi * qb, qb), qb), True)

    def sweep(kr, vr, n_blocks, go):
        def cond(carry):
            j, more = carry
            return jnp.logical_and(j < n_blocks, more)

        def body(carry):
            j, _ = carry
            rows = pl.ds(pl.multiple_of((n_blocks - 1 - j) * kb, kb), kb)
            return j + 1, block(kr, vr, rows, False)

        return lax.while_loop(cond, body, (jnp.int32(0), go))[1]

    if qb == kb:
        go = sweep(k_ref, v_ref, qi, go)
    if n_past_blocks:
        go = sweep(kp_ref, vp_ref, n_past_blocks, go)

    for p in range(n_pairs):
        r0 = p * 2 * qb
        o_ref[0, :, p * LANES:(p + 1) * LANES] = jnp.where(lo_lanes, acc_scr[r0:r0 + qb], acc_scr[r0 + qb:r0 + 2 * qb])


def _sb_call(q, k, v, k_past=None, v_past=None):
    bsz, t, _ = q.shape
    qb = min(LANES, t)
    assert t % qb == 0 and (qb == LANES or t == qb)
    n_past = 0 if k_past is None else k_past.shape[1]
    assert n_past % LANES == 0
    qblk = pl.BlockSpec((1, qb, C_WIDTH), lambda b, i: (b, i, 0))
    full = lambda n: pl.BlockSpec((1, n, C_WIDTH), lambda b, i: (b, 0, 0))
    in_specs = [qblk, full(t), full(t)]
    args = [q, k, v]
    if n_past:
        in_specs += [full(n_past), full(n_past)]
        args += [k_past, v_past]
    return pl.pallas_call(
        functools.partial(_sb_kernel, n_past_blocks=n_past // LANES),
        out_shape=jax.ShapeDtypeStruct((bsz, t, C_WIDTH), F32),
        grid=(bsz, t // qb),
        in_specs=in_specs,
        out_specs=qblk,
        scratch_shapes=[pltpu.VMEM((C_HEADS // 2, 2 * qb, LANES), BF16),
                        pltpu.VMEM((C_HEADS * qb, LANES), F32), pltpu.VMEM((C_HEADS * qb, LANES), F32)],
        compiler_params=_params("parallel", "arbitrary"),
        name="stick_breaking",
    )(*args)


def _merge_kernel(x_ref, ya_ref, yb_ref, yc_ref, p_ref, wz_ref, bz_ref, wg_ref, bg_ref, wbr_ref, wout_ref,
                  lng_ref, lnb_ref, wple_ref, wpg_ref, o_ref):
    x = x_ref[...]
    xb = x.astype(BF16)
    width = A_WIDTH
    mix_pre = jnp.zeros(x.shape, F32)
    for n, y_ref in enumerate((ya_ref, yb_ref, yc_ref)):
        z = _dg(xb, wz_ref[:, n * width:(n + 1) * width]) + bz_ref[:, n * width:(n + 1) * width]
        ys = y_ref[...] * _silu(z)
        br = _mm(ys, wbr_ref[n])
        gate = _dg(xb, wg_ref[:, n * D_MODEL:(n + 1) * D_MODEL]) + bg_ref[:, n * D_MODEL:(n + 1) * D_MODEL]
        mix_pre = mix_pre + _sigmoid(gate) * br
    mix = _mm(mix_pre, wout_ref[...])
    x1 = _norm_rows(DEEPNORM_ALPHA * x + mix, LN_EPS) * lng_ref[...] + lnb_ref[...]
    o_ref[...] = x1 + _mm(p_ref[...], wple_ref[...]) * _sigmoid(_mm(x1, wpg_ref[...]))


def _merge_call(x, ya, yb, yc, p, lw):
    n, d = x.shape
    tm = _tile(n, 256)
    rows = lambda w: pl.BlockSpec((tm, w), lambda i: (i, 0))
    return pl.pallas_call(
        _merge_kernel,
        out_shape=jax.ShapeDtypeStruct((n, d), F32),
        grid=(n // tm,),
        in_specs=[rows(d), rows(A_WIDTH), rows(B_WIDTH), rows(C_WIDTH), rows(D_PLE),
                  _const_spec((d, N_BRANCH * A_WIDTH)), _const_spec((1, N_BRANCH * A_WIDTH)),
                  _const_spec((d, N_BRANCH * d)), _const_spec((1, N_BRANCH * d)),
                  _const_spec((N_BRANCH, A_WIDTH, d)), _const_spec((d, d)),
                  _const_spec((1, d)), _const_spec((1, d)), _const_spec((D_PLE, d)), _const_spec((d, d))],
        out_specs=rows(d),
        compiler_params=_params("parallel"),
        name="merge",
    )(x, ya, yb, yc, p, lw["w_z"], lw["b_z"], lw["w_gates"], lw["b_gates"], lw["w_branch"], lw["w_out"],
      lw["ln_g"], lw["ln_b"], lw["w_ple"], lw["w_ple_gate"])


def _cols(w, *ranges):
    return jnp.concatenate([w[..., a:b] for a, b in ranges], axis=-1)


def _layer_params(i, w_in, b_in, prm):
    w = w_in[i]
    b = b_in[i]
    mix_ranges = ((_OFF_A_COLS, _OFF_A_Z), (_OFF_B_QK, _OFF_B_V), (_OFF_B_V, _OFF_B_I), (_OFF_B_O, _OFF_B_Z),
                  (_OFF_C_Q, _OFF_C_K), (_OFF_C_K, _OFF_C_V), (_OFF_C_V, _OFF_C_Z), (_OFF_B_I, _OFF_B_O))
    pad = LANES - 2 * B_HEADS
    w_mix = jnp.pad(_cols(w, *mix_ranges), ((0, 0), (0, pad)))
    b_mix = jnp.pad(_cols(b, *mix_ranges), ((0, pad),))
    z_ranges = ((_OFF_A_Z, _OFF_B_QK), (_OFF_B_Z, _OFF_C_Q), (_OFF_C_Z, _OFF_GATES))
    zeros_lora = jnp.zeros((A_LORA, A_WIDTH), F32)
    hid = jnp.arange(A_WIDTH) // A_HEAD_DIM
    row = lambda name: prm[name][i].reshape(1, -1)
    return {
        "w_mix": w_mix.astype(BF16), "b_mix": b_mix.reshape(1, -1),
        "w_z": _cols(w, *z_ranges).astype(BF16), "b_z": _cols(b, *z_ranges).reshape(1, -1),
        "w_gates": w[:, _OFF_GATES:].astype(BF16), "b_gates": b[_OFF_GATES:].reshape(1, -1),
        "mu_a": row("mu_a"), "w0_a": row("w0_a"), "a0_a": row("a0_a"), "k_k": row("k_k"), "k_a": row("k_a"),
        "wup_pad": jnp.concatenate([prm["w_decay_up"][i], zeros_lora], axis=0).astype(BF16),
        "aup_pad": jnp.concatenate([zeros_lora, prm["w_iclr_up"][i]], axis=0).astype(BF16),
        "head_sum": (hid[:, None] == hid[None, :]).astype(BF16),
        "r_k": row("r_k"), "gn_a_g": row("gn_a_g"), "gn_a_b": row("gn_a_b"),
        "conv_b_w": prm["conv_b_w"][i], "conv_b_b": row("conv_b_b"), "hn_b_g": row("hn_b_g"),
        "w_branch": prm["w_branch"][i].astype(BF16), "w_out": prm["w_out"][i].astype(BF16),
        "ln_g": row("ln_g"), "ln_b": row("ln_b"),
        "w_ple": prm["w_ple"][i].astype(BF16), "w_ple_gate": prm["w_ple_gate"][i].astype(BF16),
    }


def _run_group(x, p, init, ln_in_g, ln_in_b, layers):
    bsz, t, d = x.shape
    n = bsz * t
    shift0, wkv0, conv0, c0, n0, m0, k_past, v_past = init
    xf = _layer_norm_call(x.reshape(n, d), ln_in_g, ln_in_b)
    new = [[] for _ in range(8)]
    for i, lw in enumerate(layers):
        a_cols, b_qk, b_v, b_o, c_q, c_k, c_v, b_if = _inproj_call(xf, lw["w_mix"], lw["b_mix"])
        seq = lambda a: a.reshape(bsz, t, a.shape[-1])
        r, lwd, kh, v, kk, kka, last = _rwkv_pre_call(seq(a_cols), shift0[i], lw)
        ya, s_pairs = _rwkv_rec_call(r, lwd, kh, v, kk, kka, _wkv_to_pairs(wkv0[i]), lw)
        buf0 = jnp.pad(conv0[i], ((0, 0), (SUBLANES - (B_CONV - 1), 0), (0, 0)))
        m0b = jnp.broadcast_to(m0[i][:, :, None, None], (bsz, B_HEADS, 1, LANES))
        yb, buf, c1, n1, m1 = _mlstm_call(seq(b_qk), seq(b_v), seq(b_if), seq(b_o), buf0, c0[i],
                                          n0[i][:, :, None, :], m0b, lw)
        if k_past is None:
            yc = _sb_call(seq(c_q), seq(c_k), seq(c_v))
        else:
            past = lambda a: a.reshape(bsz, a.shape[1], C_WIDTH)
            yc = _sb_call(seq(c_q), seq(c_k), seq(c_v), past(k_past[i]), past(v_past[i]))
        xf = _merge_call(xf, ya.reshape(n, A_WIDTH), yb.reshape(n, B_WIDTH), yc.reshape(n, C_WIDTH),
                         p[i].reshape(n, D_PLE), lw)
        states = (last[:, 0], _wkv_from_pairs(s_pairs), buf[:, SUBLANES - (B_CONV - 1):], c1, n1[:, :, 0],
                  m1[:, :, 0, 0], c_k.reshape(bsz, t, C_HEADS, C_HEAD_DIM), c_v.reshape(bsz, t, C_HEADS, C_HEAD_DIM))
        for lst, s in zip(new, states):
            lst.append(s)
    return xf.reshape(bsz, t, d), [jnp.stack(lst) for lst in new]


def kernel(x_prompt, x_sample, state_shift_a, state_wkv, state_conv_b, state_mlstm_c, state_mlstm_n, state_mlstm_m, cache_sb_k, cache_sb_v, p_prompt, p_sample, ln_in_g, ln_in_b, w_in, b_in, mu_a, w0_a, w_decay_up, a0_a, w_iclr_up, k_k, k_a, r_k, gn_a_g, gn_a_b, conv_b_w, conv_b_b, hn_b_g, w_branch, w_out, ln_g, ln_b, w_ple, w_ple_gate):
    prm = dict(mu_a=mu_a, w0_a=w0_a, w_decay_up=w_decay_up, a0_a=a0_a, w_iclr_up=w_iclr_up, k_k=k_k, k_a=k_a,
               r_k=r_k, gn_a_g=gn_a_g, gn_a_b=gn_a_b, conv_b_w=conv_b_w, conv_b_b=conv_b_b, hn_b_g=hn_b_g,
               w_branch=w_branch, w_out=w_out, ln_g=ln_g, ln_b=ln_b, w_ple=w_ple, w_ple_gate=w_ple_gate)
    layers = [_layer_params(i, w_in, b_in, prm) for i in range(DEPTH)]
    bp = x_prompt.shape[0]
    zeros = lambda *s: jnp.zeros((DEPTH, bp) + s, F32)
    init_prompt = (zeros(A_SHIFT), zeros(A_HEADS, A_HEAD_DIM, A_HEAD_DIM), zeros(B_CONV - 1, 2 * B_WIDTH),
                   zeros(B_HEADS, B_HEAD_DIM, B_HEAD_DIM), zeros(B_HEADS, B_HEAD_DIM), zeros(B_HEADS), None, None)
    init_sample = (state_shift_a, state_wkv, state_conv_b, state_mlstm_c, state_mlstm_n, state_mlstm_m,
                   cache_sb_k, cache_sb_v)
    y_prompt, sp = _run_group(x_prompt, p_prompt, init_prompt, ln_in_g, ln_in_b, layers)
    y_sample, ss = _run_group(x_sample, p_sample, init_sample, ln_in_g, ln_in_b, layers)
    return (y_prompt, y_sample, sp[0], ss[0], sp[1], ss[1], sp[2], ss[2], sp[3], ss[3],
            sp[4], ss[4], sp[5], ss[5], sp[6], ss[6], sp[7], ss[7])
```

```python
import functools

import jax
import jax.numpy as jnp
from jax import lax
from jax.experimental import pallas as pl
from jax.experimental.pallas import tpu as pltpu

F32 = jnp.float32
BF16 = jnp.bfloat16

D_MODEL = 1024
DEPTH = 4
CHUNK = 64
D_PLE = 256
A_HEADS = 8
A_HEAD_DIM = 64
A_WIDTH = A_HEADS * A_HEAD_DIM
A_LORA = 64
A_SHIFT = 3 * A_WIDTH + 2 * A_LORA
B_HEADS = 4
B_HEAD_DIM = 128
B_WIDTH = B_HEADS * B_HEAD_DIM
B_CONV = 4
C_HEADS = 8
C_HEAD_DIM = 64
C_WIDTH = C_HEADS * C_HEAD_DIM
N_BRANCH = 3
DEEPNORM_ALPHA = (2 * DEPTH) ** 0.25
LN_EPS = 1e-5
GN_EPS_A = 64e-5
HN_EPS = 1e-6

LANES = 128
SUBLANES = 8
VMEM_LIMIT_BYTES = 56 * 1024 * 1024

_OFF_A_COLS = 0
_OFF_A_Z = _OFF_A_COLS + A_SHIFT
_OFF_B_QK = _OFF_A_Z + A_WIDTH
_OFF_B_V = _OFF_B_QK + 2 * B_WIDTH
_OFF_B_I = _OFF_B_V + B_WIDTH
_OFF_B_F = _OFF_B_I + B_HEADS
_OFF_B_O = _OFF_B_F + B_HEADS
_OFF_B_Z = _OFF_B_O + B_WIDTH
_OFF_C_Q = _OFF_B_Z + B_WIDTH
_OFF_C_K = _OFF_C_Q + C_WIDTH
_OFF_C_V = _OFF_C_K + C_WIDTH
_OFF_C_Z = _OFF_C_V + C_WIDTH
_OFF_GATES = _OFF_C_Z + C_WIDTH
_N_IN = _OFF_GATES + N_BRANCH * D_MODEL

_MIX_WIDTHS = (A_SHIFT, 2 * B_WIDTH, B_WIDTH, B_WIDTH, C_WIDTH, LANES)

NN = ((1,), (0,))
NT = ((1,), (1,))
TN = ((0,), (0,))


def _dg(a, b, dims=NN):
    return lax.dot_general(a, b, (dims, ((), ())), preferred_element_type=F32)


def _mm(a, b, dims=NN):
    return _dg(a.astype(BF16), b.astype(BF16), dims)


def _split3(x):
    h1 = x.astype(BF16)
    r1 = x - h1.astype(F32)
    h2 = r1.astype(BF16)
    h3 = (r1 - h2.astype(F32)).astype(BF16)
    return h1, h2, h3


def _mm_exact_rhs(lhs_bf16, x):
    h1, h2, h3 = _split3(x)
    return _dg(lhs_bf16, h1) + (_dg(lhs_bf16, h2) + _dg(lhs_bf16, h3))


def _mm_exact_lhs(x, rhs_bf16):
    h1, h2, h3 = _split3(x)
    return _dg(h1, rhs_bf16) + (_dg(h2, rhs_bf16) + _dg(h3, rhs_bf16))


def _softplus(x):
    return jnp.maximum(x, 0.0) + jnp.log(1.0 + jnp.exp(-jnp.abs(x)))


def _sigmoid(x):
    return jax.nn.sigmoid(x)


def _silu(x):
    return x * jax.nn.sigmoid(x)


def _norm_rows(x, eps):
    mu = jnp.mean(x, axis=-1, keepdims=True)
    xc = x - mu
    var = jnp.mean(xc * xc, axis=-1, keepdims=True)
    return xc * lax.rsqrt(var + eps)


def _tile(n, pref):
    if n <= pref:
        return n
    t = pref - pref % SUBLANES
    while n % t:
        t -= SUBLANES
    return t


def _params(*sem):
    return pltpu.CompilerParams(dimension_semantics=sem, vmem_limit_bytes=VMEM_LIMIT_BYTES)


def _const_spec(shape):
    nd = len(shape)
    return pl.BlockSpec(shape, lambda *_: (0,) * nd, pipeline_mode=pl.Buffered(1))


def _ln_kernel(x_ref, g_ref, b_ref, o_ref):
    o_ref[...] = _norm_rows(x_ref[...], LN_EPS) * g_ref[...] + b_ref[...]


def _layer_norm_call(x, g, b):
    n, d = x.shape
    tm = _tile(n, 512)
    return pl.pallas_call(
        _ln_kernel,
        out_shape=jax.ShapeDtypeStruct((n, d), F32),
        grid=(n // tm,),
        in_specs=[pl.BlockSpec((tm, d), lambda i: (i, 0)), _const_spec((1, d)), _const_spec((1, d))],
        out_specs=pl.BlockSpec((tm, d), lambda i: (i, 0)),
        compiler_params=_params("parallel"),
        name="ln_in",
    )(x, g.reshape(1, d), b.reshape(1, d))


def _inproj_kernel(x_ref, w_ref, b_ref, wkv_ref, bkv_ref, *refs):
    n_out = len(_MIX_WIDTHS) + 2
    out_refs = refs[-n_out:]
    xb = x_ref[...].astype(BF16)
    off = 0
    for o_ref, width in zip(out_refs, _MIX_WIDTHS):
        o_ref[...] = _dg(xb, w_ref[:, off:off + width]) + b_ref[:, off:off + width]
        off += width
    kvt = _dg(wkv_ref[...], xb, NT) + bkv_ref[...]
    out_refs[-2][0, 0] = kvt[:C_WIDTH]
    out_refs[-1][0, 0] = kvt[C_WIDTH:]


def _inproj_call(x, lw, layer, bsz, t, kt_stack=None, vt_stack=None):
    n, d = x.shape
    tm = _tile(t, 512)
    per_b = t // tm
    wtot = sum(_MIX_WIDTHS)
    stack = jax.ShapeDtypeStruct((DEPTH, bsz, C_WIDTH, t), F32)
    stack_spec = pl.BlockSpec((1, 1, C_WIDTH, tm), lambda i: (layer, i // per_b, 0, i % per_b))
    in_specs = [pl.BlockSpec((tm, d), lambda i: (i, 0)), _const_spec((d, wtot)), _const_spec((1, wtot)),
                _const_spec((2 * C_WIDTH, d)), _const_spec((2 * C_WIDTH, 1))]
    args = [x, lw["w_mix"], lw["b_mix"], lw["w_kvt"], lw["b_kvt"]]
    aliases = {}
    if kt_stack is not None:
        n_mix = len(_MIX_WIDTHS)
        aliases = {len(args): n_mix, len(args) + 1: n_mix + 1}
        in_specs += [pl.BlockSpec(memory_space=pl.ANY)] * 2
        args += [kt_stack, vt_stack]
    return pl.pallas_call(
        _inproj_kernel,
        out_shape=tuple(jax.ShapeDtypeStruct((n, wd), F32) for wd in _MIX_WIDTHS) + (stack, stack),
        grid=(n // tm,),
        in_specs=in_specs,
        out_specs=tuple(pl.BlockSpec((tm, wd), lambda i: (i, 0)) for wd in _MIX_WIDTHS) + (stack_spec, stack_spec),
        input_output_aliases=aliases,
        compiler_params=_params("parallel"),
        name="inproj",
    )(*args)


def _rwkv_kernel(cols_ref, last_ref, s0_ref, mu_ref, w0_ref, wup_ref, a0_ref, aup_ref, kk_ref, ka_ref, hsum_ref,
                 rk_ref, gng_ref, gnb_ref, y_o, last_o, s_o,
                 prev_scr, s_scr, cin_scr, at_scr, rt_scr, bt_scr, kt_scr, btw_scr, ktw_scr, v_scr, bon_scr,
                 w_scr, u0_scr, x1_scr, mrb_scr, wl_scr):
    t = pl.program_id(1)
    L = CHUNK
    tt = cols_ref.shape[1]
    nc = tt // L
    n_pairs = A_HEADS // 2

    @pl.when(t == 0)
    def _():
        prev_scr[...] = last_ref[0]
        s_scr[...] = s0_ref[0]

    ri = lax.broadcasted_iota(jnp.int32, (L, L), 0)
    ci = lax.broadcasted_iota(jnp.int32, (L, L), 1)
    strict = ri > ci
    incl = ri >= ci
    tri = jnp.where(incl, 1.0, 0.0).astype(BF16)
    eye = jnp.where(ri == ci, 1.0, 0.0)
    lane = lax.broadcasted_iota(jnp.int32, (1, LANES), 1)
    head_masks = (jnp.where(lane < A_HEAD_DIM, 1.0, 0.0), jnp.where(lane >= A_HEAD_DIM, 1.0, 0.0))
    inv_d = 1.0 / A_HEAD_DIM
    pair = lambda p: slice(p * LANES, (p + 1) * LANES)
    chunk_rows = lambda c: pl.ds(pl.multiple_of(c * L, L), L)

    u = cols_ref[0]
    row = lax.broadcasted_iota(jnp.int32, u.shape, 0)
    prev = jnp.where(row == 0, prev_scr[...], pltpu.roll(u, 1, 0))
    prev_scr[...] = u[tt - 1:tt, :]
    last_o[0] = u[tt - 1:tt, :]
    xs = u + (prev - u) * mu_ref[...]
    r = xs[:, 0:A_WIDTH]
    k = xs[:, A_WIDTH:2 * A_WIDTH]
    v = xs[:, 2 * A_WIDTH:3 * A_WIDTH]
    lora = xs[:, 3 * A_WIDTH:]
    w_log = -_softplus(-(w0_ref[...] + _mm(jnp.tanh(lora), wup_ref[...]))) - 0.5
    a = _sigmoid(a0_ref[...] + _mm(lora, aup_ref[...]))
    kk = k * kk_ref[...]
    ss = _mm_exact_lhs(kk * kk, hsum_ref[...])
    kk = kk / jnp.maximum(jnp.sqrt(ss), 1e-12)
    kh = k * (1.0 + (a - 1.0) * ka_ref[...])
    lw = -jnp.exp(w_log)
    for c in range(nc):
        cin_scr[c * L:(c + 1) * L, :] = _mm_exact_rhs(tri, lw[c * L:(c + 1) * L, :])
    cin = cin_scr[...]
    ein = jnp.exp(cin)
    einv = jnp.exp(-cin)
    bt = kk * a * einv
    kt = kh * einv
    at_scr[...] = -kk * jnp.exp(cin - lw)
    rt_scr[...] = r * ein
    bt_scr[...] = bt
    kt_scr[...] = kt
    v_scr[...] = v
    bon_scr[...] = r * kh * rk_ref[...]
    for c in range(nc):
        wl = ein[(c + 1) * L - 1:(c + 1) * L, :]
        wl_scr[c] = wl
        btw_scr[c * L:(c + 1) * L, :] = bt[c * L:(c + 1) * L, :] * wl
        ktw_scr[c * L:(c + 1) * L, :] = kt[c * L:(c + 1) * L, :] * wl

    group = 2 if nc % 2 == 0 else 1

    def phase1(i, carry):
        items = [(i * group + j, p, hm) for j in range(group) for p in range(n_pairs) for hm in head_masks]
        load = lambda ref: [ref[chunk_rows(c), pair(p)] * hm for c, p, hm in items]
        at_h, rt_h, v_h = load(at_scr), load(rt_scr), load(v_scr)
        lhs = [jnp.concatenate([x, y], axis=0).astype(BF16) for x, y in zip(at_h, rt_h)]
        sc_b = [_dg(x, bt_scr[chunk_rows(c), pair(p)].astype(BF16), NT) for x, (c, p, _) in zip(lhs, items)]
        sc_k = [_dg(x, kt_scr[chunk_rows(c), pair(p)].astype(BF16), NT) for x, (c, p, _) in zip(lhs, items)]
        a_ab = [jnp.where(strict, s[:L], 0.0) for s in sc_b]
        a_ak = [jnp.where(strict, s[:L], 0.0) for s in sc_k]
        m_rb = [jnp.where(incl, s[L:], 0.0) for s in sc_b]
        m_rk = [jnp.where(incl, s[L:], 0.0) for s in sc_k]
        tm = [eye + x for x in a_ab]
        ap = a_ab
        for _ in range(5):
            ap = [_mm(x, x) for x in ap]
            tm = [x + _mm(x, y) for x, y in zip(tm, ap)]
        x1 = [_mm(jnp.concatenate([x, y], axis=0), vh) for x, y, vh in zip(a_ak, m_rk, v_h)]
        wu = [_mm(x, jnp.concatenate([y, z[:L]], axis=1)) for x, y, z in zip(tm, at_h, x1)]
        for n in range(0, len(items), 2):
            c, p, _ = items[n]
            w_scr[chunk_rows(c), pair(p)] = wu[n][:, :LANES] + wu[n + 1][:, :LANES]
            u0_scr[chunk_rows(c), pair(p)] = wu[n][:, LANES:] + wu[n + 1][:, LANES:]
            x1_scr[chunk_rows(c), pair(p)] = x1[n][L:] + x1[n + 1][L:]
            mrb_scr[c, 2 * p] = m_rb[n]
            mrb_scr[c, 2 * p + 1] = m_rb[n + 1]
        return carry

    lax.fori_loop(0, nc // group, phase1, 0)

    def phase2(c, carry):
        rows = chunk_rows(c)
        heads = [(p, hm) for p in range(n_pairs) for hm in head_masks]
        s_pairs = [s_scr[p] for p in range(n_pairs)]
        s_pairs_b = [s.astype(BF16) for s in s_pairs]
        w_h = [w_scr[rows, pair(p)] * hm for p, hm in heads]
        rt_h = [rt_scr[rows, pair(p)] * hm for p, hm in heads]
        st = [_dg(jnp.concatenate([x, y], axis=0).astype(BF16), s_pairs_b[p], NT)
              for x, y, (p, _) in zip(w_h, rt_h, heads)]
        u_h = [u0_scr[rows, pair(p)] * hm + s[:L] for s, (p, hm) in zip(st, heads)]
        y_h = [s[L:] + _mm(mrb_scr[c, n], x) for n, (s, x) in enumerate(zip(st, u_h))]
        wl = wl_scr[c]
        for p in range(n_pairs):
            i0, i1 = 2 * p, 2 * p + 1
            hm0, hm1 = head_masks
            vp, bp, kp = v_scr[rows, pair(p)], btw_scr[rows, pair(p)], ktw_scr[rows, pair(p)]
            upd = _mm(jnp.concatenate([u_h[i0], vp * hm0, u_h[i1], vp * hm1], axis=0),
                      jnp.concatenate([bp * hm0, kp * hm0, bp * hm1, kp * hm1], axis=0), TN)
            s_scr[p] = s_pairs[p] * wl[:, pair(p)] + upd
            y_o[0, rows, pair(p)] = y_h[i0] + y_h[i1] + x1_scr[rows, pair(p)]
        return carry

    lax.fori_loop(0, nc, phase2, 0)

    for p in range(n_pairs):
        y_pair = y_o[0, :, pair(p)]
        normed = jnp.zeros(y_pair.shape, F32)
        bonus = jnp.zeros(y_pair.shape, F32)
        for hm in head_masks:
            mu = jnp.sum(y_pair * hm, axis=-1, keepdims=True) * inv_d
            yc = (y_pair - mu) * hm
            var = jnp.sum(yc * yc, axis=-1, keepdims=True) * inv_d
            normed = normed + yc * lax.rsqrt(var + GN_EPS_A)
            bonus = bonus + jnp.sum(bon_scr[:, pair(p)] * hm, axis=-1, keepdims=True) * (v_scr[:, pair(p)] * hm)
        y_o[0, :, pair(p)] = (normed * gng_ref[:, pair(p)] + gnb_ref[:, pair(p)]) + bonus
    s_o[0] = s_scr[...]


def _rwkv_call(cols, last, s0_pairs, lw):
    bsz, t, _ = cols.shape
    tt = _tile(t, 512)
    nc = tt // CHUNK
    blk = lambda w: pl.BlockSpec((1, tt, w), lambda b, i: (b, i, 0))
    row = pl.BlockSpec((1, 1, A_SHIFT), lambda b, i: (b, 0, 0))
    st = pl.BlockSpec((1, A_HEADS // 2, LANES, LANES), lambda b, i: (b, 0, 0, 0))
    tile = pltpu.VMEM((tt, A_WIDTH), F32)
    return pl.pallas_call(
        _rwkv_kernel,
        out_shape=(jax.ShapeDtypeStruct((bsz, t, A_WIDTH), F32),
                   jax.ShapeDtypeStruct((bsz, 1, A_SHIFT), F32),
                   jax.ShapeDtypeStruct((bsz, A_HEADS // 2, LANES, LANES), F32)),
        grid=(bsz, t // tt),
        in_specs=[blk(A_SHIFT), row, st, _const_spec((1, A_SHIFT)), _const_spec((1, A_WIDTH)),
                  _const_spec((2 * A_LORA, A_WIDTH)), _const_spec((1, A_WIDTH)), _const_spec((2 * A_LORA, A_WIDTH)),
                  _const_spec((1, A_WIDTH)), _const_spec((1, A_WIDTH)), _const_spec((A_WIDTH, A_WIDTH)),
                  _const_spec((1, A_WIDTH)), _const_spec((1, A_WIDTH)), _const_spec((1, A_WIDTH))],
        out_specs=(blk(A_WIDTH), row, st),
        scratch_shapes=[pltpu.VMEM((1, A_SHIFT), F32), pltpu.VMEM((A_HEADS // 2, LANES, LANES), F32)]
        + [tile] * 12
        + [pltpu.VMEM((nc, A_HEADS, CHUNK, CHUNK), F32), pltpu.VMEM((nc, 1, A_WIDTH), F32)],
        compiler_params=_params("parallel", "arbitrary"),
        name="rwkv",
    )(cols, last.reshape(bsz, 1, A_SHIFT), s0_pairs, lw["mu_a"], lw["w0_a"], lw["wup_pad"], lw["a0_a"],
      lw["aup_pad"], lw["k_k"], lw["k_a"], lw["head_sum"], lw["r_k"], lw["gn_a_g"], lw["gn_a_b"])


def _wkv_to_pairs(s):
    bsz = s.shape[0]
    s = s.reshape(bsz, A_HEADS // 2, 2, A_HEAD_DIM, A_HEAD_DIM)
    z = jnp.zeros_like(s[:, :, 0])
    top = jnp.concatenate([s[:, :, 0], z], axis=-1)
    bot = jnp.concatenate([z, s[:, :, 1]], axis=-1)
    return jnp.concatenate([top, bot], axis=-2)


def _wkv_from_pairs(sp):
    bsz = sp.shape[0]
    d = A_HEAD_DIM
    s = jnp.stack([sp[:, :, :d, :d], sp[:, :, d:, d:]], axis=2)
    return s.reshape(bsz, A_HEADS, d, d)


def _mlstm_kernel(qk_ref, v_ref, if_ref, o_ref, buf0_ref, c0_ref, n0_ref, m0_ref, cw_ref, cb_ref, hng_ref,
                  y_o, buf_o, c_o, n_o, m_o, ext_scr, c_scr, n_scr, m_scr):
    t = pl.program_id(1)
    L = CHUNK
    tt = qk_ref.shape[1]
    pad = SUBLANES

    @pl.when(t == 0)
    def _():
        ext_scr[0:pad, :] = buf0_ref[0]
        c_scr[...] = c0_ref[0]
        n_scr[...] = n0_ref[0]
        m_scr[...] = m0_ref[0]

    ext_scr[pad:pad + tt, :] = qk_ref[0]
    ri = lax.broadcasted_iota(jnp.int32, (L, L), 0)
    ci = lax.broadcasted_iota(jnp.int32, (L, L), 1)
    incl = ri >= ci
    eye = ri == ci
    tri = jnp.where(incl, 1.0, 0.0).astype(BF16)
    k_scale = B_HEAD_DIM ** -0.5

    n_chunks = tt // L
    group = 1
    hs = range(B_HEADS)
    sls = [slice(h * B_HEAD_DIM, (h + 1) * B_HEAD_DIM) for h in hs]

    def chunk_group(i, carry):
        rows, q_h, k_h, v_h, icol, bcol = [], [], [], [], [], []
        for j in range(group):
            base = pl.multiple_of((i * group + j) * L, L)
            x = ext_scr[pl.ds(base, L + pad), :]
            conv = cb_ref[...] + x[pad:, :] * cw_ref[B_CONV - 1:B_CONV, :]
            for tap in range(B_CONV - 1):
                conv = conv + pltpu.roll(x, B_CONV - 1 - tap, 0)[pad:, :] * cw_ref[tap:tap + 1, :]
            qk = _silu(conv)
            ifg = if_ref[0, pl.ds(base, L), :]
            bcum = _mm_exact_rhs(tri, -_softplus(-ifg))
            for h in hs:
                rows.append(pl.ds(base, L))
                q_h.append(qk[:, sls[h]])
                k_h.append(qk[:, B_WIDTH + h * B_HEAD_DIM:B_WIDTH + (h + 1) * B_HEAD_DIM] * k_scale)
                v_h.append(v_ref[0, pl.ds(base, L), sls[h]])
                icol.append(ifg[:, h:h + 1])
                bcol.append(bcum[:, B_HEADS + h:B_HEADS + h + 1])
        qk_t = [_mm(q, k, NT) for q, k in zip(q_h, k_h)]
        dcol = [i_ - b for i_, b in zip(icol, bcol)]
        drow = [jnp.sum(jnp.where(eye, d, 0.0), axis=0, keepdims=True) for d in dcol]
        dmat = [jnp.where(incl, b + d, -jnp.inf) for b, d in zip(bcol, drow)]
        dmax = [jnp.max(d, axis=-1, keepdims=True) for d in dmat]
        c_h = [c_scr[h] for h in hs]
        n_h = [n_scr[h] for h in hs]
        m_h = [m_scr[h][:, 0:1] for h in hs]
        for j in range(group):
            it = range(j * B_HEADS, (j + 1) * B_HEADS)
            qc = [_mm(q_h[n], c, NT) for n, c in zip(it, c_h)]
            inter = [bcol[n] + m for n, m in zip(it, m_h)]
            m_t = [jnp.maximum(x, dmax[n]) for n, x in zip(it, inter)]
            wmat = [jnp.exp(dmat[n] - m) * qk_t[n] for n, m in zip(it, m_t)]
            scale = [jnp.exp(x - m) for x, m in zip(inter, m_t)]
            wv = [_mm(w, v_h[n]) for n, w in zip(it, wmat)]
            m_h = [m[L - 1:L, :] for m in m_t]
            gcol = [jnp.exp(bcol[n][L - 1:L, :] + dcol[n] - m) for n, m in zip(it, m_h)]
            c_sc = [jnp.exp(x[L - 1:L, :] - m) for x, m in zip(inter, m_h)]
            gvk = [_mm(g * v_h[n], k_h[n], TN) for n, g in zip(it, gcol)]
            for h, n in zip(hs, it):
                num = wv[h] + scale[h] * qc[h]
                den = (jnp.sum(wmat[h], axis=-1, keepdims=True)
                       + scale[h] * jnp.sum(q_h[n] * n_h[h], axis=-1, keepdims=True))
                hid = num / jnp.maximum(jnp.abs(den), jnp.exp(-m_t[h]))
                y_o[0, rows[n], sls[h]] = (_sigmoid(o_ref[0, rows[n], sls[h]])
                                           * (_norm_rows(hid, HN_EPS) * hng_ref[:, sls[h]]))
            n_h = [s * n_ + jnp.sum(g * k_h[n], axis=0, keepdims=True) for n, s, n_, g in zip(it, c_sc, n_h, gcol)]
            c_h = [s * c + u for s, c, u in zip(c_sc, c_h, gvk)]
        for h in hs:
            c_scr[h] = c_h[h]
            n_scr[h] = n_h[h]
            m_scr[h] = jnp.broadcast_to(m_h[h], (1, LANES))
        return carry

    lax.fori_loop(0, n_chunks // group, chunk_group, 0)
    last = ext_scr[tt:tt + pad, :]
    ext_scr[0:pad, :] = last
    buf_o[0] = last
    c_o[0] = c_scr[...]
    n_o[0] = n_scr[...]
    m_o[0] = m_scr[...]


def _mlstm_call(qk, v, ifg, o, buf0, c0, n0, m0, lw):
    bsz, t, _ = v.shape
    tt = _tile(t, 512)
    blk = lambda w: pl.BlockSpec((1, tt, w), lambda b, i: (b, i, 0))
    fix = lambda *s: pl.BlockSpec((1,) + s, lambda b, i: (b,) + (0,) * len(s))
    return pl.pallas_call(
        _mlstm_kernel,
        out_shape=(jax.ShapeDtypeStruct((bsz, t, B_WIDTH), F32),
                   jax.ShapeDtypeStruct((bsz, SUBLANES, 2 * B_WIDTH), F32),
                   jax.ShapeDtypeStruct((bsz, B_HEADS, B_HEAD_DIM, B_HEAD_DIM), F32),
                   jax.ShapeDtypeStruct((bsz, B_HEADS, 1, B_HEAD_DIM), F32),
                   jax.ShapeDtypeStruct((bsz, B_HEADS, 1, LANES), F32)),
        grid=(bsz, t // tt),
        in_specs=[blk(2 * B_WIDTH), blk(B_WIDTH), blk(LANES), blk(B_WIDTH),
                  fix(SUBLANES, 2 * B_WIDTH), fix(B_HEADS, B_HEAD_DIM, B_HEAD_DIM), fix(B_HEADS, 1, B_HEAD_DIM),
                  fix(B_HEADS, 1, LANES),
                  _const_spec((B_CONV, 2 * B_WIDTH)), _const_spec((1, 2 * B_WIDTH)), _const_spec((1, B_WIDTH))],
        out_specs=(blk(B_WIDTH), fix(SUBLANES, 2 * B_WIDTH), fix(B_HEADS, B_HEAD_DIM, B_HEAD_DIM),
                   fix(B_HEADS, 1, B_HEAD_DIM), fix(B_HEADS, 1, LANES)),
        scratch_shapes=[pltpu.VMEM((tt + SUBLANES, 2 * B_WIDTH), F32),
                        pltpu.VMEM((B_HEADS, B_HEAD_DIM, B_HEAD_DIM), F32),
                        pltpu.VMEM((B_HEADS, 1, B_HEAD_DIM), F32),
                        pltpu.VMEM((B_HEADS, 1, LANES), F32)],
        compiler_params=_params("parallel", "arbitrary"),
        name="mlstm",
    )(qk, v, ifg, o, buf0, c0, n0, m0, lw["conv_b_w"], lw["conv_b_b"], lw["hn_b_g"])


SB_SKIP_LOG = -105.0


def _sb_kernel(*refs, n_past_blocks):
    if n_past_blocks:
        q_ref, k_ref, v_ref, kp_ref, vp_ref, o_ref, q2_scr, c_scr, acc_scr, pad_scr = refs
    else:
        q_ref, k_ref, v_ref, o_ref, q2_scr, c_scr, acc_scr, pad_scr = refs
    qi = pl.program_id(1)
    qb = q_ref.shape[1]
    kb = LANES
    n_pairs = C_HEADS // 2
    n_rows = C_HEADS * qb

    rj = lax.broadcasted_iota(jnp.int32, (2 * kb, 2 * kb), 0) & (kb - 1)
    cj = lax.broadcasted_iota(jnp.int32, (2 * kb, 2 * kb), 1)
    cum2 = jnp.where((cj >= kb) | (rj > cj), 1.0, 0.0).astype(BF16)
    qpos = lax.broadcasted_iota(jnp.int32, (n_rows, kb), 0) & (qb - 1)
    kpos = lax.broadcasted_iota(jnp.int32, (n_rows, kb), 1)
    diag_mask = kpos < qpos
    lane = lax.broadcasted_iota(jnp.int32, (1, LANES), 1)
    lo_lanes = lane < C_HEAD_DIM
    q = q_ref[0] * (C_HEAD_DIM ** -0.5)
    for p in range(n_pairs):
        qs = q[:, p * LANES:(p + 1) * LANES]
        q2_scr[p] = jnp.concatenate([jnp.where(lo_lanes, qs, 0.0), jnp.where(lo_lanes, 0.0, qs)], axis=0).astype(BF16)

    def load_t(ref, slot, p, keys):
        x = ref[0, 0, p * LANES:(p + 1) * LANES, keys]
        if x.shape[1] == kb:
            return x.astype(BF16)
        pad_scr[slot] = jnp.zeros((LANES, kb), F32)
        pad_scr[slot, :, 0:x.shape[1]] = x
        return pad_scr[slot].astype(BF16)

    def block(kr, vr, keys, first):
        z = jnp.concatenate([_dg(q2_scr[p], load_t(kr, 2 * p, p, keys)) for p in range(n_pairs)], axis=0)
        sp = _softplus(z)
        lk = jnp.where(diag_mask, -sp, 0.0) if first else -sp
        h1 = lk.astype(BF16)
        h2 = (lk - h1.astype(F32)).astype(BF16)
        cs = _dg(jnp.concatenate([h1, h2], axis=1), cum2)
        if first:
            a = jnp.where(diag_mask, jnp.exp((z - sp) + cs[:, :kb]), 0.0)
            c_new = cs[:, kb:]
        else:
            c_old = c_scr[...]
            a = jnp.exp((z - sp) + cs[:, :kb] + c_old)
            c_new = c_old + cs[:, kb:]
        c_scr[...] = c_new
        ab = a.astype(BF16)
        for p in range(n_pairs):
            rs = slice(p * 2 * qb, (p + 1) * 2 * qb)
            upd = _dg(ab[rs], load_t(vr, 2 * p + 1, p, keys), NT)
            acc_scr[rs] = upd if first else acc_scr[rs] + upd
        return jnp.max(c_new) > SB_SKIP_LOG

    own_keys = pl.ds(pl.multiple_of(qi * qb, qb), qb) if qb == kb else pl.ds(0, qb)
    go = block(k_ref, v_ref, own_keys, True)

    def sweep(kr, vr, n_blocks, go):
        def cond(carry):
            j, more = carry
            return jnp.logical_and(j < n_blocks, more)

        def body(carry):
            j, _ = carry
            keys = pl.ds(pl.multiple_of((n_blocks - 1 - j) * kb, kb), kb)
            return j + 1, block(kr, vr, keys, False)

        return lax.while_loop(cond, body, (jnp.int32(0), go))[1]

    if qb == kb:
        go = sweep(k_ref, v_ref, qi, go)
    if n_past_blocks:
        go = sweep(kp_ref, vp_ref, n_past_blocks, go)

    for p in range(n_pairs):
        r0 = p * 2 * qb
        o_ref[0, :, p * LANES:(p + 1) * LANES] = jnp.where(lo_lanes, acc_scr[r0:r0 + qb], acc_scr[r0 + qb:r0 + 2 * qb])


def _sb_call(q, kt, vt, layer, kt_past=None, vt_past=None):
    bsz, t, _ = q.shape
    qb = min(LANES, t)
    assert t % qb == 0 and (qb == LANES or t == qb)
    n_past = 0 if kt_past is None else kt_past.shape[-1]
    assert n_past % LANES == 0
    qblk = pl.BlockSpec((1, qb, C_WIDTH), lambda b, i: (b, i, 0))
    full = lambda n: pl.BlockSpec((1, 1, C_WIDTH, n), lambda b, i: (layer, b, 0, 0))
    in_specs = [qblk, full(t), full(t)]
    args = [q, kt, vt]
    if n_past:
        in_specs += [full(n_past), full(n_past)]
        args += [kt_past, vt_past]
    return pl.pallas_call(
        functools.partial(_sb_kernel, n_past_blocks=n_past // LANES),
        out_shape=jax.ShapeDtypeStruct((bsz, t, C_WIDTH), F32),
        grid=(bsz, t // qb),
        in_specs=in_specs,
        out_specs=qblk,
        scratch_shapes=[pltpu.VMEM((C_HEADS // 2, 2 * qb, LANES), BF16),
                        pltpu.VMEM((C_HEADS * qb, LANES), F32), pltpu.VMEM((C_HEADS * qb, LANES), F32),
                        pltpu.VMEM((C_HEADS, LANES, LANES), F32)],
        compiler_params=_params("parallel", "arbitrary"),
        name="stick_breaking",
    )(*args)


def _merge_kernel(x_ref, ya_ref, yb_ref, yc_ref, p_ref, wz_ref, bz_ref, wg_ref, bg_ref, wbr_ref, wout_ref,
                  lng_ref, lnb_ref, wple_ref, wpg_ref, o_ref):
    x = x_ref[...]
    xb = x.astype(BF16)
    width = A_WIDTH
    mix_pre = jnp.zeros(x.shape, F32)
    for n, y_ref in enumerate((ya_ref, yb_ref, yc_ref)):
        z = _dg(xb, wz_ref[:, n * width:(n + 1) * width]) + bz_ref[:, n * width:(n + 1) * width]
        ys = y_ref[...] * _silu(z)
        br = _mm(ys, wbr_ref[n])
        gate = _dg(xb, wg_ref[:, n * D_MODEL:(n + 1) * D_MODEL]) + bg_ref[:, n * D_MODEL:(n + 1) * D_MODEL]
        mix_pre = mix_pre + _sigmoid(gate) * br
    mix = _mm(mix_pre, wout_ref[...])
    x1 = _norm_rows(DEEPNORM_ALPHA * x + mix, LN_EPS) * lng_ref[...] + lnb_ref[...]
    o_ref[...] = x1 + _mm(p_ref[...], wple_ref[...]) * _sigmoid(_mm(x1, wpg_ref[...]))


def _merge_call(x, ya, yb, yc, p, lw):
    n, d = x.shape
    tm = _tile(n, 512)
    rows = lambda w: pl.BlockSpec((tm, w), lambda i: (i, 0))
    return pl.pallas_call(
        _merge_kernel,
        out_shape=jax.ShapeDtypeStruct((n, d), F32),
        grid=(n // tm,),
        in_specs=[rows(d), rows(A_WIDTH), rows(B_WIDTH), rows(C_WIDTH), rows(D_PLE),
                  _const_spec((d, N_BRANCH * A_WIDTH)), _const_spec((1, N_BRANCH * A_WIDTH)),
                  _const_spec((d, N_BRANCH * d)), _const_spec((1, N_BRANCH * d)),
                  _const_spec((N_BRANCH, A_WIDTH, d)), _const_spec((d, d)),
                  _const_spec((1, d)), _const_spec((1, d)), _const_spec((D_PLE, d)), _const_spec((d, d))],
        out_specs=rows(d),
        compiler_params=_params("parallel"),
        name="merge",
    )(x, ya, yb, yc, p, lw["w_z"], lw["b_z"], lw["w_gates"], lw["b_gates"], lw["w_branch"], lw["w_out"],
      lw["ln_g"], lw["ln_b"], lw["w_ple"], lw["w_ple_gate"])


def _cols(w, *ranges):
    return jnp.concatenate([w[..., a:b] for a, b in ranges], axis=-1)


def _layer_params(i, w_in, b_in, prm):
    w = w_in[i]
    b = b_in[i]
    mix_ranges = ((_OFF_A_COLS, _OFF_A_Z), (_OFF_B_QK, _OFF_B_V), (_OFF_B_V, _OFF_B_I), (_OFF_B_O, _OFF_B_Z),
                  (_OFF_C_Q, _OFF_C_K), (_OFF_B_I, _OFF_B_O))
    pad = LANES - 2 * B_HEADS
    w_mix = jnp.pad(_cols(w, *mix_ranges), ((0, 0), (0, pad)))
    b_mix = jnp.pad(_cols(b, *mix_ranges), ((0, pad),))
    z_ranges = ((_OFF_A_Z, _OFF_B_QK), (_OFF_B_Z, _OFF_C_Q), (_OFF_C_Z, _OFF_GATES))
    zeros_lora = jnp.zeros((A_LORA, A_WIDTH), F32)
    hid = jnp.arange(A_WIDTH) // A_HEAD_DIM
    row = lambda name: prm[name][i].reshape(1, -1)
    return {
        "w_mix": w_mix.astype(BF16), "b_mix": b_mix.reshape(1, -1),
        "w_kvt": w[:, _OFF_C_K:_OFF_C_Z].T.astype(BF16), "b_kvt": b[_OFF_C_K:_OFF_C_Z].reshape(-1, 1),
        "w_z": _cols(w, *z_ranges).astype(BF16), "b_z": _cols(b, *z_ranges).reshape(1, -1),
        "w_gates": w[:, _OFF_GATES:].astype(BF16), "b_gates": b[_OFF_GATES:].reshape(1, -1),
        "mu_a": row("mu_a"), "w0_a": row("w0_a"), "a0_a": row("a0_a"), "k_k": row("k_k"), "k_a": row("k_a"),
        "wup_pad": jnp.concatenate([prm["w_decay_up"][i], zeros_lora], axis=0).astype(BF16),
        "aup_pad": jnp.concatenate([zeros_lora, prm["w_iclr_up"][i]], axis=0).astype(BF16),
        "head_sum": (hid[:, None] == hid[None, :]).astype(BF16),
        "r_k": row("r_k"), "gn_a_g": row("gn_a_g"), "gn_a_b": row("gn_a_b"),
        "conv_b_w": prm["conv_b_w"][i], "conv_b_b": row("conv_b_b"), "hn_b_g": row("hn_b_g"),
        "w_branch": prm["w_branch"][i].astype(BF16), "w_out": prm["w_out"][i].astype(BF16),
        "ln_g": row("ln_g"), "ln_b": row("ln_b"),
        "w_ple": prm["w_ple"][i].astype(BF16), "w_ple_gate": prm["w_ple_gate"][i].astype(BF16),
    }


def _run_group(x, p, init, ln_in_g, ln_in_b, layers):
    bsz, t, d = x.shape
    n = bsz * t
    shift0, wkv0, conv0, c0, n0, m0, k_past, v_past = init
    xf = _layer_norm_call(x.reshape(n, d), ln_in_g, ln_in_b)
    new = [[] for _ in range(6)]
    kt = vt = None
    channel_major = lambda a: jnp.transpose(a, (0, 1, 3, 4, 2)).reshape(a.shape[0], a.shape[1], C_WIDTH, a.shape[2])
    if k_past is not None:
        k_past, v_past = channel_major(k_past), channel_major(v_past)
    for i, lw in enumerate(layers):
        a_cols, b_qk, b_v, b_o, c_q, b_if, kt, vt = _inproj_call(xf, lw, i, bsz, t, kt, vt)
        seq = lambda a: a.reshape(bsz, t, a.shape[-1])
        ya, last, s_pairs = _rwkv_call(seq(a_cols), shift0[i], _wkv_to_pairs(wkv0[i]), lw)
        buf0 = jnp.pad(conv0[i], ((0, 0), (SUBLANES - (B_CONV - 1), 0), (0, 0)))
        m0b = jnp.broadcast_to(m0[i][:, :, None, None], (bsz, B_HEADS, 1, LANES))
        yb, buf, c1, n1, m1 = _mlstm_call(seq(b_qk), seq(b_v), seq(b_if), seq(b_o), buf0, c0[i],
                                          n0[i][:, :, None, :], m0b, lw)
        yc = _sb_call(seq(c_q), kt, vt, i, k_past, v_past)
        xf = _merge_call(xf, ya.reshape(n, A_WIDTH), yb.reshape(n, B_WIDTH), yc.reshape(n, C_WIDTH),
                         p[i].reshape(n, D_PLE), lw)
        states = (last[:, 0], _wkv_from_pairs(s_pairs), buf[:, SUBLANES - (B_CONV - 1):], c1, n1[:, :, 0],
                  m1[:, :, 0, 0])
        for lst, s in zip(new, states):
            lst.append(s)
    token_major = lambda a: jnp.transpose(a.reshape(DEPTH, bsz, C_HEADS, C_HEAD_DIM, t), (0, 1, 4, 2, 3))
    return xf.reshape(bsz, t, d), [jnp.stack(lst) for lst in new] + [token_major(kt), token_major(vt)]


def kernel(x_prompt, x_sample, state_shift_a, state_wkv, state_conv_b, state_mlstm_c, state_mlstm_n, state_mlstm_m, cache_sb_k, cache_sb_v, p_prompt, p_sample, ln_in_g, ln_in_b, w_in, b_in, mu_a, w0_a, w_decay_up, a0_a, w_iclr_up, k_k, k_a, r_k, gn_a_g, gn_a_b, conv_b_w, conv_b_b, hn_b_g, w_branch, w_out, ln_g, ln_b, w_ple, w_ple_gate):
    prm = dict(mu_a=mu_a, w0_a=w0_a, w_decay_up=w_decay_up, a0_a=a0_a, w_iclr_up=w_iclr_up, k_k=k_k, k_a=k_a,
               r_k=r_k, gn_a_g=gn_a_g, gn_a_b=gn_a_b, conv_b_w=conv_b_w, conv_b_b=conv_b_b, hn_b_g=hn_b_g,
               w_branch=w_branch, w_out=w_out, ln_g=ln_g, ln_b=ln_b, w_ple=w_ple, w_ple_gate=w_ple_gate)
    layers = [_layer_params(i, w_in, b_in, prm) for i in range(DEPTH)]
    bp = x_prompt.shape[0]
    zeros = lambda *s: jnp.zeros((DEPTH, bp) + s, F32)
    init_prompt = (zeros(A_SHIFT), zeros(A_HEADS, A_HEAD_DIM, A_HEAD_DIM), zeros(B_CONV - 1, 2 * B_WIDTH),
                   zeros(B_HEADS, B_HEAD_DIM, B_HEAD_DIM), zeros(B_HEADS, B_HEAD_DIM), zeros(B_HEADS), None, None)
    init_sample = (state_shift_a, state_wkv, state_conv_b, state_mlstm_c, state_mlstm_n, state_mlstm_m,
                   cache_sb_k, cache_sb_v)
    y_prompt, sp = _run_group(x_prompt, p_prompt, init_prompt, ln_in_g, ln_in_b, layers)
    y_sample, ss = _run_group(x_sample, p_sample, init_sample, ln_in_g, ln_in_b, layers)
    return (y_prompt, y_sample, sp[0], ss[0], sp[1], ss[1], sp[2], ss[2], sp[3], ss[3],
            sp[4], ss[4], sp[5], ss[5], sp[6], ss[6], sp[7], ss[7])
```

```python
import functools

import jax
import jax.numpy as jnp
from jax import lax
from jax.experimental import pallas as pl
from jax.experimental.pallas import tpu as pltpu

F32 = jnp.float32
BF16 = jnp.bfloat16

D_MODEL = 1024
DEPTH = 4
CHUNK = 64
D_PLE = 256
A_HEADS = 8
A_HEAD_DIM = 64
A_WIDTH = A_HEADS * A_HEAD_DIM
A_LORA = 64
A_SHIFT = 3 * A_WIDTH + 2 * A_LORA
B_HEADS = 4
B_HEAD_DIM = 128
B_WIDTH = B_HEADS * B_HEAD_DIM
B_CONV = 4
C_HEADS = 8
C_HEAD_DIM = 64
C_WIDTH = C_HEADS * C_HEAD_DIM
N_BRANCH = 3
DEEPNORM_ALPHA = (2 * DEPTH) ** 0.25
LN_EPS = 1e-5
GN_EPS_A = 64e-5
HN_EPS = 1e-6

LANES = 128
SUBLANES = 8
VMEM_LIMIT_BYTES = 56 * 1024 * 1024

_OFF_A_COLS = 0
_OFF_A_Z = _OFF_A_COLS + A_SHIFT
_OFF_B_QK = _OFF_A_Z + A_WIDTH
_OFF_B_V = _OFF_B_QK + 2 * B_WIDTH
_OFF_B_I = _OFF_B_V + B_WIDTH
_OFF_B_F = _OFF_B_I + B_HEADS
_OFF_B_O = _OFF_B_F + B_HEADS
_OFF_B_Z = _OFF_B_O + B_WIDTH
_OFF_C_Q = _OFF_B_Z + B_WIDTH
_OFF_C_K = _OFF_C_Q + C_WIDTH
_OFF_C_V = _OFF_C_K + C_WIDTH
_OFF_C_Z = _OFF_C_V + C_WIDTH
_OFF_GATES = _OFF_C_Z + C_WIDTH
_N_IN = _OFF_GATES + N_BRANCH * D_MODEL

_MIX_WIDTHS = (A_SHIFT, 2 * B_WIDTH, B_WIDTH, B_WIDTH, C_WIDTH, LANES)

NN = ((1,), (0,))
NT = ((1,), (1,))
TN = ((0,), (0,))


def _dg(a, b, dims=NN):
    return lax.dot_general(a, b, (dims, ((), ())), preferred_element_type=F32)


def _mm(a, b, dims=NN):
    return _dg(a.astype(BF16), b.astype(BF16), dims)


def _split3(x):
    h1 = x.astype(BF16)
    r1 = x - h1.astype(F32)
    h2 = r1.astype(BF16)
    h3 = (r1 - h2.astype(F32)).astype(BF16)
    return h1, h2, h3


def _mm_exact_rhs(lhs_bf16, x):
    h1, h2, h3 = _split3(x)
    return _dg(lhs_bf16, h1) + (_dg(lhs_bf16, h2) + _dg(lhs_bf16, h3))


def _mm_exact_lhs(x, rhs_bf16):
    h1, h2, h3 = _split3(x)
    return _dg(h1, rhs_bf16) + (_dg(h2, rhs_bf16) + _dg(h3, rhs_bf16))


def _softplus(x):
    return jnp.maximum(x, 0.0) + jnp.log(1.0 + jnp.exp(-jnp.abs(x)))


def _sigmoid(x):
    return jax.nn.sigmoid(x)


def _silu(x):
    return x * jax.nn.sigmoid(x)


def _norm_rows(x, eps):
    mu = jnp.mean(x, axis=-1, keepdims=True)
    xc = x - mu
    var = jnp.mean(xc * xc, axis=-1, keepdims=True)
    return xc * lax.rsqrt(var + eps)


def _tile(n, pref):
    if n <= pref:
        return n
    t = pref - pref % SUBLANES
    while n % t:
        t -= SUBLANES
    return t


def _params(*sem):
    return pltpu.CompilerParams(dimension_semantics=sem, vmem_limit_bytes=VMEM_LIMIT_BYTES)


def _const_spec(shape):
    nd = len(shape)
    return pl.BlockSpec(shape, lambda *_: (0,) * nd, pipeline_mode=pl.Buffered(1))


def _ln_kernel(x_ref, g_ref, b_ref, o_ref):
    o_ref[...] = _norm_rows(x_ref[...], LN_EPS) * g_ref[...] + b_ref[...]


def _layer_norm_call(x, g, b):
    n, d = x.shape
    tm = _tile(n, 512)
    return pl.pallas_call(
        _ln_kernel,
        out_shape=jax.ShapeDtypeStruct((n, d), F32),
        grid=(n // tm,),
        in_specs=[pl.BlockSpec((tm, d), lambda i: (i, 0)), _const_spec((1, d)), _const_spec((1, d))],
        out_specs=pl.BlockSpec((tm, d), lambda i: (i, 0)),
        compiler_params=_params("parallel"),
        name="ln_in",
    )(x, g.reshape(1, d), b.reshape(1, d))


def _inproj_kernel(x_ref, w_ref, b_ref, wkv_ref, bkv_ref, *refs):
    n_out = len(_MIX_WIDTHS) + 2
    out_refs = refs[-n_out:]
    xb = x_ref[...].astype(BF16)
    off = 0
    for o_ref, width in zip(out_refs, _MIX_WIDTHS):
        o_ref[...] = _dg(xb, w_ref[:, off:off + width]) + b_ref[:, off:off + width]
        off += width
    kvt = _dg(wkv_ref[...], xb, NT) + bkv_ref[...]
    out_refs[-2][0, 0] = kvt[:C_WIDTH]
    out_refs[-1][0, 0] = kvt[C_WIDTH:]


def _inproj_call(x, lw, layer, bsz, t, kt_stack=None, vt_stack=None):
    n, d = x.shape
    tm = _tile(t, 512)
    per_b = t // tm
    wtot = sum(_MIX_WIDTHS)
    stack = jax.ShapeDtypeStruct((DEPTH, bsz, C_WIDTH, t), F32)
    stack_spec = pl.BlockSpec((1, 1, C_WIDTH, tm), lambda i: (layer, i // per_b, 0, i % per_b))
    in_specs = [pl.BlockSpec((tm, d), lambda i: (i, 0)), _const_spec((d, wtot)), _const_spec((1, wtot)),
                _const_spec((2 * C_WIDTH, d)), _const_spec((2 * C_WIDTH, 1))]
    args = [x, lw["w_mix"], lw["b_mix"], lw["w_kvt"], lw["b_kvt"]]
    aliases = {}
    if kt_stack is not None:
        n_mix = len(_MIX_WIDTHS)
        aliases = {len(args): n_mix, len(args) + 1: n_mix + 1}
        in_specs += [pl.BlockSpec(memory_space=pl.ANY)] * 2
        args += [kt_stack, vt_stack]
    return pl.pallas_call(
        _inproj_kernel,
        out_shape=tuple(jax.ShapeDtypeStruct((n, wd), F32) for wd in _MIX_WIDTHS) + (stack, stack),
        grid=(n // tm,),
        in_specs=in_specs,
        out_specs=tuple(pl.BlockSpec((tm, wd), lambda i: (i, 0)) for wd in _MIX_WIDTHS) + (stack_spec, stack_spec),
        input_output_aliases=aliases,
        compiler_params=_params("parallel"),
        name="inproj",
    )(*args)


RWKV_RING = 3


def _round_robin(*steps):
    steps = list(steps)
    while steps:
        for s in list(steps):
            try:
                next(s)
            except StopIteration:
                steps.remove(s)


def _rwkv_pipe_kernel(cols_ref, last_ref, s0_ref, mu_ref, w0_ref, wup_ref, a0_ref, aup_ref, kk_ref, ka_ref, hsum_ref,
                      rk_ref, gng_ref, gnb_ref, y_o, last_o, s_o,
                      prev_scr, s_scr, at_r, rt_r, bt_r, kt_r, btw_r, ktw_r, v_r, bon_r, w_r, u0_r, x1_r, mrb_r, wl_r):
    t = pl.program_id(1)
    L = CHUNK
    tt = cols_ref.shape[1]
    gr = min(tt, 2 * L)
    cpg = gr // L
    ng = tt // gr
    n_pairs = A_HEADS // 2
    rings = (at_r, rt_r, bt_r, kt_r, btw_r, ktw_r, v_r, bon_r, w_r, u0_r, x1_r, mrb_r, wl_r)

    @pl.when(t == 0)
    def _():
        prev_scr[...] = last_ref[0]
        s_scr[...] = s0_ref[0]
        for ring in rings:
            ring[...] = jnp.zeros(ring.shape, F32)

    ri = lax.broadcasted_iota(jnp.int32, (L, L), 0)
    ci = lax.broadcasted_iota(jnp.int32, (L, L), 1)
    strict = ri > ci
    incl = ri >= ci
    tri = jnp.where(incl, 1.0, 0.0).astype(BF16)
    eye = jnp.where(ri == ci, 1.0, 0.0)
    lane = lax.broadcasted_iota(jnp.int32, (1, LANES), 1)
    head_masks = (jnp.where(lane < A_HEAD_DIM, 1.0, 0.0), jnp.where(lane >= A_HEAD_DIM, 1.0, 0.0))
    inv_d = 1.0 / A_HEAD_DIM
    pair = lambda p: slice(p * LANES, (p + 1) * LANES)
    chunk = lambda c: slice(c * L, (c + 1) * L)
    heads = [(p, hm) for p in range(n_pairs) for hm in head_masks]

    def prologue(g):
        gp = jnp.minimum(g, ng - 1)
        slot = lax.rem(gp, RWKV_RING)
        r0 = pl.multiple_of(gp * gr, gr)
        u = cols_ref[0, pl.ds(r0, gr), :]
        above = cols_ref[0, pl.ds(pl.multiple_of(jnp.maximum(r0 - SUBLANES, 0), SUBLANES), SUBLANES), :]
        before = jnp.where(gp == 0, prev_scr[...], above[SUBLANES - 1:SUBLANES, :])
        row = lax.broadcasted_iota(jnp.int32, u.shape, 0)
        prev = jnp.where(row == 0, before, pltpu.roll(u, 1, 0))
        xs = u + (prev - u) * mu_ref[...]
        yield
        r = xs[:, 0:A_WIDTH]
        k = xs[:, A_WIDTH:2 * A_WIDTH]
        v = xs[:, 2 * A_WIDTH:3 * A_WIDTH]
        lora = xs[:, 3 * A_WIDTH:]
        w_log = -_softplus(-(w0_ref[...] + _mm(jnp.tanh(lora), wup_ref[...]))) - 0.5
        a = _sigmoid(a0_ref[...] + _mm(lora, aup_ref[...]))
        yield
        kk = k * kk_ref[...]
        ss = _mm_exact_lhs(kk * kk, hsum_ref[...])
        kk = kk / jnp.maximum(jnp.sqrt(ss), 1e-12)
        yield
        kh = k * (1.0 + (a - 1.0) * ka_ref[...])
        lw = -jnp.exp(w_log)
        cin = jnp.concatenate([_mm_exact_rhs(tri, lw[chunk(c), :]) for c in range(cpg)], axis=0)
        yield
        ein = jnp.exp(cin)
        einv = jnp.exp(-cin)
        bt = kk * a * einv
        kt = kh * einv
        at_r[slot] = -kk * jnp.exp(cin - lw)
        rt_r[slot] = r * ein
        yield
        bt_r[slot] = bt
        kt_r[slot] = kt
        v_r[slot] = v
        bon_r[slot] = r * kh * rk_ref[...]
        yield
        for c in range(cpg):
            wl = ein[(c + 1) * L - 1:(c + 1) * L, :]
            wl_r[slot, c] = wl
            btw_r[slot, chunk(c), :] = bt[chunk(c), :] * wl
            ktw_r[slot, chunk(c), :] = kt[chunk(c), :] * wl
        yield

    def state_free(g):
        slot = lax.rem(jnp.clip(g - 1, 0, ng - 1), RWKV_RING)
        items = [(c, p, hm) for c in range(cpg) for p, hm in heads]
        load = lambda ring: [ring[slot, chunk(c), pair(p)] * hm for c, p, hm in items]
        at_h, rt_h, v_h = load(at_r), load(rt_r), load(v_r)
        lhs = [jnp.concatenate([x, y], axis=0).astype(BF16) for x, y in zip(at_h, rt_h)]
        yield
        sc_b = [_dg(x, bt_r[slot, chunk(c), pair(p)].astype(BF16), NT) for x, (c, p, _) in zip(lhs, items)]
        sc_k = [_dg(x, kt_r[slot, chunk(c), pair(p)].astype(BF16), NT) for x, (c, p, _) in zip(lhs, items)]
        yield
        a_ab = [jnp.where(strict, s[:L], 0.0) for s in sc_b]
        a_ak = [jnp.where(strict, s[:L], 0.0) for s in sc_k]
        m_rb = [jnp.where(incl, s[L:], 0.0) for s in sc_b]
        m_rk = [jnp.where(incl, s[L:], 0.0) for s in sc_k]
        tm = [eye + x for x in a_ab]
        ap = a_ab
        for _ in range(5):
            ap = [_mm(x, x) for x in ap]
            yield
            tm = [x + _mm(x, y) for x, y in zip(tm, ap)]
            yield
        x1 = [_mm(jnp.concatenate([x, y], axis=0), vh) for x, y, vh in zip(a_ak, m_rk, v_h)]
        yield
        wu = [_mm(x, jnp.concatenate([y, z[:L]], axis=1)) for x, y, z in zip(tm, at_h, x1)]
        yield
        for n in range(0, len(items), 2):
            c, p, _ = items[n]
            w_r[slot, chunk(c), pair(p)] = wu[n][:, :LANES] + wu[n + 1][:, :LANES]
            u0_r[slot, chunk(c), pair(p)] = wu[n][:, LANES:] + wu[n + 1][:, LANES:]
            x1_r[slot, chunk(c), pair(p)] = x1[n][L:] + x1[n + 1][L:]
            mrb_r[slot, c, 2 * p] = m_rb[n]
            mrb_r[slot, c, 2 * p + 1] = m_rb[n + 1]
        yield

    def state_step(g):
        valid = g >= 2
        g2 = jnp.clip(g - 2, 0, ng - 1)
        slot = lax.rem(g2, RWKV_RING)
        hm0, hm1 = head_masks
        y_pairs = [[] for _ in range(n_pairs)]
        for c in range(cpg):
            s_pairs = [s_scr[p] for p in range(n_pairs)]
            s_pairs_b = [s.astype(BF16) for s in s_pairs]
            w_h = [w_r[slot, chunk(c), pair(p)] * hm for p, hm in heads]
            rt_h = [rt_r[slot, chunk(c), pair(p)] * hm for p, hm in heads]
            st = [_dg(jnp.concatenate([x, y], axis=0).astype(BF16), s_pairs_b[p], NT)
                  for x, y, (p, _) in zip(w_h, rt_h, heads)]
            yield
            u_h = [u0_r[slot, chunk(c), pair(p)] * hm + s[:L] for s, (p, hm) in zip(st, heads)]
            y_h = [s[L:] + _mm(mrb_r[slot, c, n], x) for n, (s, x) in enumerate(zip(st, u_h))]
            upd = []
            for p in range(n_pairs):
                vp, bp, kp = v_r[slot, chunk(c), pair(p)], btw_r[slot, chunk(c), pair(p)], ktw_r[slot, chunk(c), pair(p)]
                upd.append(_mm(jnp.concatenate([u_h[2 * p], vp * hm0, u_h[2 * p + 1], vp * hm1], axis=0),
                               jnp.concatenate([bp * hm0, kp * hm0, bp * hm1, kp * hm1], axis=0), TN))
            yield
            wl = wl_r[slot, c]
            for p in range(n_pairs):
                s_scr[p] = jnp.where(valid, s_pairs[p] * wl[:, pair(p)] + upd[p], s_pairs[p])
                y_pairs[p].append(y_h[2 * p] + y_h[2 * p + 1] + x1_r[slot, chunk(c), pair(p)])
            yield
        rows = pl.ds(pl.multiple_of(g2 * gr, gr), gr)
        for p in range(n_pairs):
            y_pair = jnp.concatenate(y_pairs[p], axis=0)
            normed = jnp.zeros(y_pair.shape, F32)
            bonus = jnp.zeros(y_pair.shape, F32)
            for hm in head_masks:
                mu = jnp.sum(y_pair * hm, axis=-1, keepdims=True) * inv_d
                yc = (y_pair - mu) * hm
                var = jnp.sum(yc * yc, axis=-1, keepdims=True) * inv_d
                normed = normed + yc * lax.rsqrt(var + GN_EPS_A)
                bonus = bonus + (jnp.sum(bon_r[slot, :, pair(p)] * hm, axis=-1, keepdims=True)
                                 * (v_r[slot, :, pair(p)] * hm))
            y_o[0, rows, pair(p)] = (normed * gng_ref[:, pair(p)] + gnb_ref[:, pair(p)]) + bonus
            yield

    def trip(g, carry):
        _round_robin(prologue(g), state_free(g), state_step(g))
        return carry

    lax.fori_loop(0, ng + 2, trip, 0)
    prev_scr[...] = cols_ref[0, tt - 1:tt, :]
    last_o[0] = cols_ref[0, tt - 1:tt, :]
    s_o[0] = s_scr[...]


def _rwkv_call(cols, last, s0_pairs, lw):
    bsz, t, _ = cols.shape
    tt = _tile(t, 2048)
    gr = min(tt, 2 * CHUNK)
    blk = lambda w: pl.BlockSpec((1, tt, w), lambda b, i: (b, i, 0))
    row = pl.BlockSpec((1, 1, A_SHIFT), lambda b, i: (b, 0, 0))
    st = pl.BlockSpec((1, A_HEADS // 2, LANES, LANES), lambda b, i: (b, 0, 0, 0))
    ring = pltpu.VMEM((RWKV_RING, gr, A_WIDTH), F32)
    return pl.pallas_call(
        _rwkv_pipe_kernel,
        out_shape=(jax.ShapeDtypeStruct((bsz, t, A_WIDTH), F32),
                   jax.ShapeDtypeStruct((bsz, 1, A_SHIFT), F32),
                   jax.ShapeDtypeStruct((bsz, A_HEADS // 2, LANES, LANES), F32)),
        grid=(bsz, t // tt),
        in_specs=[blk(A_SHIFT), row, st, _const_spec((1, A_SHIFT)), _const_spec((1, A_WIDTH)),
                  _const_spec((2 * A_LORA, A_WIDTH)), _const_spec((1, A_WIDTH)), _const_spec((2 * A_LORA, A_WIDTH)),
                  _const_spec((1, A_WIDTH)), _const_spec((1, A_WIDTH)), _const_spec((A_WIDTH, A_WIDTH)),
                  _const_spec((1, A_WIDTH)), _const_spec((1, A_WIDTH)), _const_spec((1, A_WIDTH))],
        out_specs=(blk(A_WIDTH), row, st),
        scratch_shapes=[pltpu.VMEM((1, A_SHIFT), F32), pltpu.VMEM((A_HEADS // 2, LANES, LANES), F32)]
        + [ring] * 11
        + [pltpu.VMEM((RWKV_RING, gr // CHUNK, A_HEADS, CHUNK, CHUNK), F32),
           pltpu.VMEM((RWKV_RING, gr // CHUNK, 1, A_WIDTH), F32)],
        compiler_params=_params("parallel", "arbitrary"),
        name="rwkv",
    )(cols, last.reshape(bsz, 1, A_SHIFT), s0_pairs, lw["mu_a"], lw["w0_a"], lw["wup_pad"], lw["a0_a"],
      lw["aup_pad"], lw["k_k"], lw["k_a"], lw["head_sum"], lw["r_k"], lw["gn_a_g"], lw["gn_a_b"])


def _wkv_to_pairs(s):
    bsz = s.shape[0]
    s = s.reshape(bsz, A_HEADS // 2, 2, A_HEAD_DIM, A_HEAD_DIM)
    z = jnp.zeros_like(s[:, :, 0])
    top = jnp.concatenate([s[:, :, 0], z], axis=-1)
    bot = jnp.concatenate([z, s[:, :, 1]], axis=-1)
    return jnp.concatenate([top, bot], axis=-2)


def _wkv_from_pairs(sp):
    bsz = sp.shape[0]
    d = A_HEAD_DIM
    s = jnp.stack([sp[:, :, :d, :d], sp[:, :, d:, d:]], axis=2)
    return s.reshape(bsz, A_HEADS, d, d)


def _mlstm_kernel(qk_ref, v_ref, if_ref, o_ref, buf0_ref, c0_ref, n0_ref, m0_ref, cw_ref, cb_ref, hng_ref,
                  y_o, buf_o, c_o, n_o, m_o, ext_scr, c_scr, n_scr, m_scr):
    t = pl.program_id(1)
    tt = qk_ref.shape[1]
    L = min(tt, LANES)
    pad = SUBLANES

    @pl.when(t == 0)
    def _():
        ext_scr[0:pad, :] = buf0_ref[0]
        c_scr[...] = c0_ref[0]
        n_scr[...] = n0_ref[0]
        m_scr[...] = m0_ref[0]

    ext_scr[pad:pad + tt, :] = qk_ref[0]
    ri = lax.broadcasted_iota(jnp.int32, (L, L), 0)
    ci = lax.broadcasted_iota(jnp.int32, (L, L), 1)
    incl = ri >= ci
    eye = ri == ci
    tri = jnp.where(incl, 1.0, 0.0).astype(BF16)
    k_scale = B_HEAD_DIM ** -0.5

    n_chunks = tt // L
    group = 1
    hs = range(B_HEADS)
    sls = [slice(h * B_HEAD_DIM, (h + 1) * B_HEAD_DIM) for h in hs]

    def chunk_group(i, carry):
        rows, q_h, k_h, v_h, icol, bcol = [], [], [], [], [], []
        for j in range(group):
            base = pl.multiple_of((i * group + j) * L, L)
            x = ext_scr[pl.ds(base, L + pad), :]
            conv = cb_ref[...] + x[pad:, :] * cw_ref[B_CONV - 1:B_CONV, :]
            for tap in range(B_CONV - 1):
                conv = conv + pltpu.roll(x, B_CONV - 1 - tap, 0)[pad:, :] * cw_ref[tap:tap + 1, :]
            qk = _silu(conv)
            ifg = if_ref[0, pl.ds(base, L), :]
            bcum = _mm_exact_rhs(tri, -_softplus(-ifg))
            for h in hs:
                rows.append(pl.ds(base, L))
                q_h.append(qk[:, sls[h]])
                k_h.append(qk[:, B_WIDTH + h * B_HEAD_DIM:B_WIDTH + (h + 1) * B_HEAD_DIM] * k_scale)
                v_h.append(v_ref[0, pl.ds(base, L), sls[h]])
                icol.append(ifg[:, h:h + 1])
                bcol.append(bcum[:, B_HEADS + h:B_HEADS + h + 1])
        qk_t = [_mm(q, k, NT) for q, k in zip(q_h, k_h)]
        dcol = [i_ - b for i_, b in zip(icol, bcol)]
        drow = [jnp.sum(jnp.where(eye, d, 0.0), axis=0, keepdims=True) for d in dcol]
        dmat = [jnp.where(incl, b + d, -jnp.inf) for b, d in zip(bcol, drow)]
        dmax = [jnp.max(d, axis=-1, keepdims=True) for d in dmat]
        c_h = [c_scr[h] for h in hs]
        n_h = [n_scr[h] for h in hs]
        m_h = [m_scr[h][:, 0:1] for h in hs]
        for j in range(group):
            it = range(j * B_HEADS, (j + 1) * B_HEADS)
            qc = [_mm(q_h[n], c, NT) for n, c in zip(it, c_h)]
            inter = [bcol[n] + m for n, m in zip(it, m_h)]
            m_t = [jnp.maximum(x, dmax[n]) for n, x in zip(it, inter)]
            wmat = [jnp.exp(dmat[n] - m) * qk_t[n] for n, m in zip(it, m_t)]
            scale = [jnp.exp(x - m) for x, m in zip(inter, m_t)]
            wv = [_mm(w, v_h[n]) for n, w in zip(it, wmat)]
            m_h = [m[L - 1:L, :] for m in m_t]
            gcol = [jnp.exp(bcol[n][L - 1:L, :] + dcol[n] - m) for n, m in zip(it, m_h)]
            c_sc = [jnp.exp(x[L - 1:L, :] - m) for x, m in zip(inter, m_h)]
            gvk = [_mm(g * v_h[n], k_h[n], TN) for n, g in zip(it, gcol)]
            for h, n in zip(hs, it):
                num = wv[h] + scale[h] * qc[h]
                den = (jnp.sum(wmat[h], axis=-1, keepdims=True)
                       + scale[h] * jnp.sum(q_h[n] * n_h[h], axis=-1, keepdims=True))
                hid = num / jnp.maximum(jnp.abs(den), jnp.exp(-m_t[h]))
                y_o[0, rows[n], sls[h]] = (_sigmoid(o_ref[0, rows[n], sls[h]])
                                           * (_norm_rows(hid, HN_EPS) * hng_ref[:, sls[h]]))
            n_h = [s * n_ + jnp.sum(g * k_h[n], axis=0, keepdims=True) for n, s, n_, g in zip(it, c_sc, n_h, gcol)]
            c_h = [s * c + u for s, c, u in zip(c_sc, c_h, gvk)]
        for h in hs:
            c_scr[h] = c_h[h]
            n_scr[h] = n_h[h]
            m_scr[h] = jnp.broadcast_to(m_h[h], (1, LANES))
        return carry

    lax.fori_loop(0, n_chunks // group, chunk_group, 0)
    last = ext_scr[tt:tt + pad, :]
    ext_scr[0:pad, :] = last
    buf_o[0] = last
    c_o[0] = c_scr[...]
    n_o[0] = n_scr[...]
    m_o[0] = m_scr[...]


def _mlstm_call(qk, v, ifg, o, buf0, c0, n0, m0, lw):
    bsz, t, _ = v.shape
    tt = _tile(t, 512)
    blk = lambda w: pl.BlockSpec((1, tt, w), lambda b, i: (b, i, 0))
    fix = lambda *s: pl.BlockSpec((1,) + s, lambda b, i: (b,) + (0,) * len(s))
    return pl.pallas_call(
        _mlstm_kernel,
        out_shape=(jax.ShapeDtypeStruct((bsz, t, B_WIDTH), F32),
                   jax.ShapeDtypeStruct((bsz, SUBLANES, 2 * B_WIDTH), F32),
                   jax.ShapeDtypeStruct((bsz, B_HEADS, B_HEAD_DIM, B_HEAD_DIM), F32),
                   jax.ShapeDtypeStruct((bsz, B_HEADS, 1, B_HEAD_DIM), F32),
                   jax.ShapeDtypeStruct((bsz, B_HEADS, 1, LANES), F32)),
        grid=(bsz, t // tt),
        in_specs=[blk(2 * B_WIDTH), blk(B_WIDTH), blk(LANES), blk(B_WIDTH),
                  fix(SUBLANES, 2 * B_WIDTH), fix(B_HEADS, B_HEAD_DIM, B_HEAD_DIM), fix(B_HEADS, 1, B_HEAD_DIM),
                  fix(B_HEADS, 1, LANES),
                  _const_spec((B_CONV, 2 * B_WIDTH)), _const_spec((1, 2 * B_WIDTH)), _const_spec((1, B_WIDTH))],
        out_specs=(blk(B_WIDTH), fix(SUBLANES, 2 * B_WIDTH), fix(B_HEADS, B_HEAD_DIM, B_HEAD_DIM),
                   fix(B_HEADS, 1, B_HEAD_DIM), fix(B_HEADS, 1, LANES)),
        scratch_shapes=[pltpu.VMEM((tt + SUBLANES, 2 * B_WIDTH), F32),
                        pltpu.VMEM((B_HEADS, B_HEAD_DIM, B_HEAD_DIM), F32),
                        pltpu.VMEM((B_HEADS, 1, B_HEAD_DIM), F32),
                        pltpu.VMEM((B_HEADS, 1, LANES), F32)],
        compiler_params=_params("parallel", "arbitrary"),
        name="mlstm",
    )(qk, v, ifg, o, buf0, c0, n0, m0, lw["conv_b_w"], lw["conv_b_b"], lw["hn_b_g"])


SB_SKIP_LOG = -105.0


def _sb_kernel(*refs, n_past_blocks):
    if n_past_blocks:
        q_ref, k_ref, v_ref, kp_ref, vp_ref, o_ref, q2_scr, c_scr, acc_scr, pad_scr = refs
    else:
        q_ref, k_ref, v_ref, o_ref, q2_scr, c_scr, acc_scr, pad_scr = refs
    qi = pl.program_id(1)
    qb = q_ref.shape[1]
    kb = LANES
    n_pairs = C_HEADS // 2
    n_rows = C_HEADS * qb

    rj = lax.broadcasted_iota(jnp.int32, (2 * kb, 2 * kb), 0) & (kb - 1)
    cj = lax.broadcasted_iota(jnp.int32, (2 * kb, 2 * kb), 1)
    cum2 = jnp.where((cj >= kb) | (rj > cj), 1.0, 0.0).astype(BF16)
    qpos = lax.broadcasted_iota(jnp.int32, (n_rows, kb), 0) & (qb - 1)
    kpos = lax.broadcasted_iota(jnp.int32, (n_rows, kb), 1)
    diag_mask = kpos < qpos
    lane = lax.broadcasted_iota(jnp.int32, (1, LANES), 1)
    lo_lanes = lane < C_HEAD_DIM
    q = q_ref[0] * (C_HEAD_DIM ** -0.5)
    for p in range(n_pairs):
        qs = q[:, p * LANES:(p + 1) * LANES]
        q2_scr[p] = jnp.concatenate([jnp.where(lo_lanes, qs, 0.0), jnp.where(lo_lanes, 0.0, qs)], axis=0).astype(BF16)

    def load_t(ref, slot, p, keys):
        x = ref[0, 0, p * LANES:(p + 1) * LANES, keys]
        if x.shape[1] == kb:
            return x.astype(BF16)
        pad_scr[slot] = jnp.zeros((LANES, kb), F32)
        pad_scr[slot, :, 0:x.shape[1]] = x
        return pad_scr[slot].astype(BF16)

    def block(kr, vr, keys, first):
        z = jnp.concatenate([_dg(q2_scr[p], load_t(kr, 2 * p, p, keys)) for p in range(n_pairs)], axis=0)
        sp = _softplus(z)
        lk = jnp.where(diag_mask, -sp, 0.0) if first else -sp
        h1 = lk.astype(BF16)
        h2 = (lk - h1.astype(F32)).astype(BF16)
        cs = _dg(jnp.concatenate([h1, h2], axis=1), cum2)
        if first:
            a = jnp.where(diag_mask, jnp.exp((z - sp) + cs[:, :kb]), 0.0)
            c_new = cs[:, kb:]
        else:
            c_old = c_scr[...]
            a = jnp.exp((z - sp) + cs[:, :kb] + c_old)
            c_new = c_old + cs[:, kb:]
        c_scr[...] = c_new
        ab = a.astype(BF16)
        for p in range(n_pairs):
            rs = slice(p * 2 * qb, (p + 1) * 2 * qb)
            upd = _dg(ab[rs], load_t(vr, 2 * p + 1, p, keys), NT)
            acc_scr[rs] = upd if first else acc_scr[rs] + upd
        return jnp.max(c_new) > SB_SKIP_LOG

    own_keys = pl.ds(pl.multiple_of(qi * qb, qb), qb) if qb == kb else pl.ds(0, qb)
    go = block(k_ref, v_ref, own_keys, True)

    def sweep(kr, vr, n_blocks, go):
        def cond(carry):
            j, more = carry
            return jnp.logical_and(j < n_blocks, more)

        def body(carry):
            j, _ = carry
            keys = pl.ds(pl.multiple_of((n_blocks - 1 - j) * kb, kb), kb)
            return j + 1, block(kr, vr, keys, False)

        return lax.while_loop(cond, body, (jnp.int32(0), go))[1]

    if qb == kb:
        go = sweep(k_ref, v_ref, qi, go)
    if n_past_blocks:
        go = sweep(kp_ref, vp_ref, n_past_blocks, go)

    for p in range(n_pairs):
        r0 = p * 2 * qb
        o_ref[0, :, p * LANES:(p + 1) * LANES] = jnp.where(lo_lanes, acc_scr[r0:r0 + qb], acc_scr[r0 + qb:r0 + 2 * qb])


def _sb_call(q, kt, vt, layer, kt_past=None, vt_past=None):
    bsz, t, _ = q.shape
    qb = min(LANES, t)
    assert t % qb == 0 and (qb == LANES or t == qb)
    n_past = 0 if kt_past is None else kt_past.shape[-1]
    assert n_past % LANES == 0
    qblk = pl.BlockSpec((1, qb, C_WIDTH), lambda b, i: (b, i, 0))
    full = lambda n: pl.BlockSpec((1, 1, C_WIDTH, n), lambda b, i: (layer, b, 0, 0))
    in_specs = [qblk, full(t), full(t)]
    args = [q, kt, vt]
    if n_past:
        in_specs += [full(n_past), full(n_past)]
        args += [kt_past, vt_past]
    return pl.pallas_call(
        functools.partial(_sb_kernel, n_past_blocks=n_past // LANES),
        out_shape=jax.ShapeDtypeStruct((bsz, t, C_WIDTH), F32),
        grid=(bsz, t // qb),
        in_specs=in_specs,
        out_specs=qblk,
        scratch_shapes=[pltpu.VMEM((C_HEADS // 2, 2 * qb, LANES), BF16),
                        pltpu.VMEM((C_HEADS * qb, LANES), F32), pltpu.VMEM((C_HEADS * qb, LANES), F32),
                        pltpu.VMEM((C_HEADS, LANES, LANES), F32)],
        compiler_params=_params("parallel", "arbitrary"),
        name="stick_breaking",
    )(*args)


def _merge_kernel(x_ref, ya_ref, yb_ref, yc_ref, p_ref, wz_ref, bz_ref, wg_ref, bg_ref, wbr_ref, wout_ref,
                  lng_ref, lnb_ref, wple_ref, wpg_ref, o_ref):
    x = x_ref[...]
    xb = x.astype(BF16)
    width = A_WIDTH
    mix_pre = jnp.zeros(x.shape, F32)
    for n, y_ref in enumerate((ya_ref, yb_ref, yc_ref)):
        z = _dg(xb, wz_ref[:, n * width:(n + 1) * width]) + bz_ref[:, n * width:(n + 1) * width]
        ys = y_ref[...] * _silu(z)
        br = _mm(ys, wbr_ref[n])
        gate = _dg(xb, wg_ref[:, n * D_MODEL:(n + 1) * D_MODEL]) + bg_ref[:, n * D_MODEL:(n + 1) * D_MODEL]
        mix_pre = mix_pre + _sigmoid(gate) * br
    mix = _mm(mix_pre, wout_ref[...])
    x1 = _norm_rows(DEEPNORM_ALPHA * x + mix, LN_EPS) * lng_ref[...] + lnb_ref[...]
    o_ref[...] = x1 + _mm(p_ref[0], wple_ref[...]) * _sigmoid(_mm(x1, wpg_ref[...]))


def _merge_call(x, ya, yb, yc, p_all, layer, lw):
    n, d = x.shape
    tm = _tile(n, 512)
    rows = lambda w: pl.BlockSpec((tm, w), lambda i: (i, 0))
    return pl.pallas_call(
        _merge_kernel,
        out_shape=jax.ShapeDtypeStruct((n, d), F32),
        grid=(n // tm,),
        in_specs=[rows(d), rows(A_WIDTH), rows(B_WIDTH), rows(C_WIDTH),
                  pl.BlockSpec((1, tm, D_PLE), lambda i: (layer, i, 0)),
                  _const_spec((d, N_BRANCH * A_WIDTH)), _const_spec((1, N_BRANCH * A_WIDTH)),
                  _const_spec((d, N_BRANCH * d)), _const_spec((1, N_BRANCH * d)),
                  _const_spec((N_BRANCH, A_WIDTH, d)), _const_spec((d, d)),
                  _const_spec((1, d)), _const_spec((1, d)), _const_spec((D_PLE, d)), _const_spec((d, d))],
        out_specs=rows(d),
        compiler_params=_params("parallel"),
        name="merge",
    )(x, ya, yb, yc, p_all, lw["w_z"], lw["b_z"], lw["w_gates"], lw["b_gates"], lw["w_branch"], lw["w_out"],
      lw["ln_g"], lw["ln_b"], lw["w_ple"], lw["w_ple_gate"])


def _cols(w, *ranges):
    return jnp.concatenate([w[..., a:b] for a, b in ranges], axis=-1)


def _layer_params(i, w_in, b_in, prm):
    w = w_in[i]
    b = b_in[i]
    mix_ranges = ((_OFF_A_COLS, _OFF_A_Z), (_OFF_B_QK, _OFF_B_V), (_OFF_B_V, _OFF_B_I), (_OFF_B_O, _OFF_B_Z),
                  (_OFF_C_Q, _OFF_C_K), (_OFF_B_I, _OFF_B_O))
    pad = LANES - 2 * B_HEADS
    w_mix = jnp.pad(_cols(w, *mix_ranges), ((0, 0), (0, pad)))
    b_mix = jnp.pad(_cols(b, *mix_ranges), ((0, pad),))
    z_ranges = ((_OFF_A_Z, _OFF_B_QK), (_OFF_B_Z, _OFF_C_Q), (_OFF_C_Z, _OFF_GATES))
    zeros_lora = jnp.zeros((A_LORA, A_WIDTH), F32)
    hid = jnp.arange(A_WIDTH) // A_HEAD_DIM
    row = lambda name: prm[name][i].reshape(1, -1)
    return {
        "w_mix": w_mix.astype(BF16), "b_mix": b_mix.reshape(1, -1),
        "w_kvt": w[:, _OFF_C_K:_OFF_C_Z].T.astype(BF16), "b_kvt": b[_OFF_C_K:_OFF_C_Z].reshape(-1, 1),
        "w_z": _cols(w, *z_ranges).astype(BF16), "b_z": _cols(b, *z_ranges).reshape(1, -1),
        "w_gates": w[:, _OFF_GATES:].astype(BF16), "b_gates": b[_OFF_GATES:].reshape(1, -1),
        "mu_a": row("mu_a"), "w0_a": row("w0_a"), "a0_a": row("a0_a"), "k_k": row("k_k"), "k_a": row("k_a"),
        "wup_pad": jnp.concatenate([prm["w_decay_up"][i], zeros_lora], axis=0).astype(BF16),
        "aup_pad": jnp.concatenate([zeros_lora, prm["w_iclr_up"][i]], axis=0).astype(BF16),
        "head_sum": (hid[:, None] == hid[None, :]).astype(BF16),
        "r_k": row("r_k"), "gn_a_g": row("gn_a_g"), "gn_a_b": row("gn_a_b"),
        "conv_b_w": prm["conv_b_w"][i], "conv_b_b": row("conv_b_b"), "hn_b_g": row("hn_b_g"),
        "w_branch": prm["w_branch"][i].astype(BF16), "w_out": prm["w_out"][i].astype(BF16),
        "ln_g": row("ln_g"), "ln_b": row("ln_b"),
        "w_ple": prm["w_ple"][i].astype(BF16), "w_ple_gate": prm["w_ple_gate"][i].astype(BF16),
    }


def _run_group(x, p, init, ln_in_g, ln_in_b, layers):
    bsz, t, d = x.shape
    n = bsz * t
    shift0, wkv0, conv0, c0, n0, m0, k_past, v_past = init
    xf = _layer_norm_call(x.reshape(n, d), ln_in_g, ln_in_b)
    new = [[] for _ in range(6)]
    kt = vt = None
    channel_major = lambda a: jnp.transpose(a, (0, 1, 3, 4, 2)).reshape(a.shape[0], a.shape[1], C_WIDTH, a.shape[2])
    if k_past is not None:
        k_past, v_past = channel_major(k_past), channel_major(v_past)
    for i, lw in enumerate(layers):
        a_cols, b_qk, b_v, b_o, c_q, b_if, kt, vt = _inproj_call(xf, lw, i, bsz, t, kt, vt)
        seq = lambda a: a.reshape(bsz, t, a.shape[-1])
        ya, last, s_pairs = _rwkv_call(seq(a_cols), shift0[i], _wkv_to_pairs(wkv0[i]), lw)
        buf0 = jnp.pad(conv0[i], ((0, 0), (SUBLANES - (B_CONV - 1), 0), (0, 0)))
        m0b = jnp.broadcast_to(m0[i][:, :, None, None], (bsz, B_HEADS, 1, LANES))
        yb, buf, c1, n1, m1 = _mlstm_call(seq(b_qk), seq(b_v), seq(b_if), seq(b_o), buf0, c0[i],
                                          n0[i][:, :, None, :], m0b, lw)
        yc = _sb_call(seq(c_q), kt, vt, i, k_past, v_past)
        xf = _merge_call(xf, ya.reshape(n, A_WIDTH), yb.reshape(n, B_WIDTH), yc.reshape(n, C_WIDTH),
                         p.reshape(DEPTH, n, D_PLE), i, lw)
        states = (last[:, 0], _wkv_from_pairs(s_pairs), buf[:, SUBLANES - (B_CONV - 1):], c1, n1[:, :, 0],
                  m1[:, :, 0, 0])
        for lst, s in zip(new, states):
            lst.append(s)
    token_major = lambda a: jnp.transpose(a.reshape(DEPTH, bsz, C_HEADS, C_HEAD_DIM, t), (0, 1, 4, 2, 3))
    return xf.reshape(bsz, t, d), [jnp.stack(lst) for lst in new] + [token_major(kt), token_major(vt)]


def kernel(x_prompt, x_sample, state_shift_a, state_wkv, state_conv_b, state_mlstm_c, state_mlstm_n, state_mlstm_m, cache_sb_k, cache_sb_v, p_prompt, p_sample, ln_in_g, ln_in_b, w_in, b_in, mu_a, w0_a, w_decay_up, a0_a, w_iclr_up, k_k, k_a, r_k, gn_a_g, gn_a_b, conv_b_w, conv_b_b, hn_b_g, w_branch, w_out, ln_g, ln_b, w_ple, w_ple_gate):
    prm = dict(mu_a=mu_a, w0_a=w0_a, w_decay_up=w_decay_up, a0_a=a0_a, w_iclr_up=w_iclr_up, k_k=k_k, k_a=k_a,
               r_k=r_k, gn_a_g=gn_a_g, gn_a_b=gn_a_b, conv_b_w=conv_b_w, conv_b_b=conv_b_b, hn_b_g=hn_b_g,
               w_branch=w_branch, w_out=w_out, ln_g=ln_g, ln_b=ln_b, w_ple=w_ple, w_ple_gate=w_ple_gate)
    layers = [_layer_params(i, w_in, b_in, prm) for i in range(DEPTH)]
    bp = x_prompt.shape[0]
    zeros = lambda *s: jnp.zeros((DEPTH, bp) + s, F32)
    init_prompt = (zeros(A_SHIFT), zeros(A_HEADS, A_HEAD_DIM, A_HEAD_DIM), zeros(B_CONV - 1, 2 * B_WIDTH),
                   zeros(B_HEADS, B_HEAD_DIM, B_HEAD_DIM), zeros(B_HEADS, B_HEAD_DIM), zeros(B_HEADS), None, None)
    init_sample = (state_shift_a, state_wkv, state_conv_b, state_mlstm_c, state_mlstm_n, state_mlstm_m,
                   cache_sb_k, cache_sb_v)
    y_prompt, sp = _run_group(x_prompt, p_prompt, init_prompt, ln_in_g, ln_in_b, layers)
    y_sample, ss = _run_group(x_sample, p_sample, init_sample, ln_in_g, ln_in_b, layers)
    return (y_prompt, y_sample, sp[0], ss[0], sp[1], ss[1], sp[2], ss[2], sp[3], ss[3],
            sp[4], ss[4], sp[5], ss[5], sp[6], ss[6], sp[7], ss[7])
```

```python
import functools

import jax
import jax.numpy as jnp
from jax import lax
from jax.experimental import pallas as pl
from jax.experimental.pallas import tpu as pltpu

F32 = jnp.float32
BF16 = jnp.bfloat16

D_MODEL = 1024
DEPTH = 4
CHUNK = 64
D_PLE = 256
A_HEADS = 8
A_HEAD_DIM = 64
A_WIDTH = A_HEADS * A_HEAD_DIM
A_LORA = 64
A_SHIFT = 3 * A_WIDTH + 2 * A_LORA
B_HEADS = 4
B_HEAD_DIM = 128
B_WIDTH = B_HEADS * B_HEAD_DIM
B_CONV = 4
C_HEADS = 8
C_HEAD_DIM = 64
C_WIDTH = C_HEADS * C_HEAD_DIM
N_BRANCH = 3
DEEPNORM_ALPHA = (2 * DEPTH) ** 0.25
LN_EPS = 1e-5
GN_EPS_A = 64e-5
HN_EPS = 1e-6

LANES = 128
SUBLANES = 8
VMEM_LIMIT_BYTES = 56 * 1024 * 1024

_OFF_A_COLS = 0
_OFF_A_Z = _OFF_A_COLS + A_SHIFT
_OFF_B_QK = _OFF_A_Z + A_WIDTH
_OFF_B_V = _OFF_B_QK + 2 * B_WIDTH
_OFF_B_I = _OFF_B_V + B_WIDTH
_OFF_B_F = _OFF_B_I + B_HEADS
_OFF_B_O = _OFF_B_F + B_HEADS
_OFF_B_Z = _OFF_B_O + B_WIDTH
_OFF_C_Q = _OFF_B_Z + B_WIDTH
_OFF_C_K = _OFF_C_Q + C_WIDTH
_OFF_C_V = _OFF_C_K + C_WIDTH
_OFF_C_Z = _OFF_C_V + C_WIDTH
_OFF_GATES = _OFF_C_Z + C_WIDTH
_N_IN = _OFF_GATES + N_BRANCH * D_MODEL

_MIX_WIDTHS = (A_SHIFT, 2 * B_WIDTH, B_WIDTH, B_WIDTH, C_WIDTH, LANES)

NN = ((1,), (0,))
NT = ((1,), (1,))
TN = ((0,), (0,))


def _dg(a, b, dims=NN):
    return lax.dot_general(a, b, (dims, ((), ())), preferred_element_type=F32)


def _mm(a, b, dims=NN):
    return _dg(a.astype(BF16), b.astype(BF16), dims)


def _split3(x):
    h1 = x.astype(BF16)
    r1 = x - h1.astype(F32)
    h2 = r1.astype(BF16)
    h3 = (r1 - h2.astype(F32)).astype(BF16)
    return h1, h2, h3


def _mm_exact_rhs(lhs_bf16, x):
    h1, h2, h3 = _split3(x)
    return _dg(lhs_bf16, h1) + (_dg(lhs_bf16, h2) + _dg(lhs_bf16, h3))


def _mm_exact_lhs(x, rhs_bf16):
    h1, h2, h3 = _split3(x)
    return _dg(h1, rhs_bf16) + (_dg(h2, rhs_bf16) + _dg(h3, rhs_bf16))


def _softplus(x):
    return jnp.maximum(x, 0.0) + jnp.log(1.0 + jnp.exp(-jnp.abs(x)))


def _sigmoid(x):
    return jax.nn.sigmoid(x)


def _silu(x):
    return x * jax.nn.sigmoid(x)


def _norm_rows(x, eps):
    mu = jnp.mean(x, axis=-1, keepdims=True)
    xc = x - mu
    var = jnp.mean(xc * xc, axis=-1, keepdims=True)
    return xc * lax.rsqrt(var + eps)


def _tile(n, pref):
    if n <= pref:
        return n
    t = pref - pref % SUBLANES
    while n % t:
        t -= SUBLANES
    return t


def _params(*sem):
    return pltpu.CompilerParams(dimension_semantics=sem, vmem_limit_bytes=VMEM_LIMIT_BYTES)


def _const_spec(shape):
    nd = len(shape)
    return pl.BlockSpec(shape, lambda *_: (0,) * nd, pipeline_mode=pl.Buffered(1))


def _ln_kernel(x_ref, g_ref, b_ref, o_ref):
    o_ref[...] = _norm_rows(x_ref[...], LN_EPS) * g_ref[...] + b_ref[...]


def _layer_norm_call(x, g, b):
    n, d = x.shape
    tm = _tile(n, 512)
    return pl.pallas_call(
        _ln_kernel,
        out_shape=jax.ShapeDtypeStruct((n, d), F32),
        grid=(n // tm,),
        in_specs=[pl.BlockSpec((tm, d), lambda i: (i, 0)), _const_spec((1, d)), _const_spec((1, d))],
        out_specs=pl.BlockSpec((tm, d), lambda i: (i, 0)),
        compiler_params=_params("parallel"),
        name="ln_in",
    )(x, g.reshape(1, d), b.reshape(1, d))


def _inproj_kernel(x_ref, w_ref, b_ref, wkv_ref, bkv_ref, *refs):
    n_out = len(_MIX_WIDTHS) + 2
    out_refs = refs[-n_out:]
    xb = x_ref[...].astype(BF16)
    off = 0
    for o_ref, width in zip(out_refs, _MIX_WIDTHS):
        o_ref[...] = _dg(xb, w_ref[:, off:off + width]) + b_ref[:, off:off + width]
        off += width
    kvt = _dg(wkv_ref[...], xb, NT) + bkv_ref[...]
    out_refs[-2][0, 0] = kvt[:C_WIDTH]
    out_refs[-1][0, 0] = kvt[C_WIDTH:]


def _inproj_call(x, lw, layer, bsz, t, kt_stack=None, vt_stack=None):
    n, d = x.shape
    tm = _tile(t, 512)
    per_b = t // tm
    wtot = sum(_MIX_WIDTHS)
    stack = jax.ShapeDtypeStruct((DEPTH, bsz, C_WIDTH, t), F32)
    stack_spec = pl.BlockSpec((1, 1, C_WIDTH, tm), lambda i: (layer, i // per_b, 0, i % per_b))
    in_specs = [pl.BlockSpec((tm, d), lambda i: (i, 0)), _const_spec((d, wtot)), _const_spec((1, wtot)),
                _const_spec((2 * C_WIDTH, d)), _const_spec((2 * C_WIDTH, 1))]
    args = [x, lw["w_mix"], lw["b_mix"], lw["w_kvt"], lw["b_kvt"]]
    aliases = {}
    if kt_stack is not None:
        n_mix = len(_MIX_WIDTHS)
        aliases = {len(args): n_mix, len(args) + 1: n_mix + 1}
        in_specs += [pl.BlockSpec(memory_space=pl.ANY)] * 2
        args += [kt_stack, vt_stack]
    return pl.pallas_call(
        _inproj_kernel,
        out_shape=tuple(jax.ShapeDtypeStruct((n, wd), F32) for wd in _MIX_WIDTHS) + (stack, stack),
        grid=(n // tm,),
        in_specs=in_specs,
        out_specs=tuple(pl.BlockSpec((tm, wd), lambda i: (i, 0)) for wd in _MIX_WIDTHS) + (stack_spec, stack_spec),
        input_output_aliases=aliases,
        compiler_params=_params("parallel"),
        name="inproj",
    )(*args)


RWKV_RING = 3


def _round_robin(*steps):
    steps = list(steps)
    while steps:
        for s in list(steps):
            try:
                next(s)
            except StopIteration:
                steps.remove(s)


def _rwkv_pipe_kernel(cols_ref, last_ref, s0_ref, mu_ref, w0_ref, wup_ref, a0_ref, aup_ref, kk_ref, ka_ref, hsum_ref,
                      rk_ref, gng_ref, gnb_ref, y_o, last_o, s_o,
                      prev_scr, s_scr, at_r, rt_r, bt_r, kt_r, btw_r, ktw_r, v_r, bon_r, w_r, u0_r, x1_r, mrb_r, wl_r):
    t = pl.program_id(1)
    L = CHUNK
    tt = cols_ref.shape[1]
    gr = min(tt, 2 * L)
    cpg = gr // L
    ng = tt // gr
    n_pairs = A_HEADS // 2
    rings = (at_r, rt_r, bt_r, kt_r, btw_r, ktw_r, v_r, bon_r, w_r, u0_r, x1_r, mrb_r, wl_r)

    @pl.when(t == 0)
    def _():
        prev_scr[...] = last_ref[0]
        s_scr[...] = s0_ref[0]
        if ng > 1:
            for ring in rings:
                ring[...] = jnp.zeros(ring.shape, F32)

    ri = lax.broadcasted_iota(jnp.int32, (L, L), 0)
    ci = lax.broadcasted_iota(jnp.int32, (L, L), 1)
    strict = ri > ci
    incl = ri >= ci
    tri = jnp.where(incl, 1.0, 0.0).astype(BF16)
    eye = jnp.where(ri == ci, 1.0, 0.0)
    lane = lax.broadcasted_iota(jnp.int32, (1, LANES), 1)
    head_masks = (jnp.where(lane < A_HEAD_DIM, 1.0, 0.0), jnp.where(lane >= A_HEAD_DIM, 1.0, 0.0))
    inv_d = 1.0 / A_HEAD_DIM
    pair = lambda p: slice(p * LANES, (p + 1) * LANES)
    chunk = lambda c: slice(c * L, (c + 1) * L)
    heads = [(p, hm) for p in range(n_pairs) for hm in head_masks]

    def prologue(g):
        gp = jnp.minimum(g, ng - 1)
        slot = lax.rem(gp, RWKV_RING)
        r0 = pl.multiple_of(gp * gr, gr)
        u = cols_ref[0, pl.ds(r0, gr), :]
        above = cols_ref[0, pl.ds(pl.multiple_of(jnp.maximum(r0 - SUBLANES, 0), SUBLANES), SUBLANES), :]
        before = jnp.where(gp == 0, prev_scr[...], above[SUBLANES - 1:SUBLANES, :])
        row = lax.broadcasted_iota(jnp.int32, u.shape, 0)
        prev = jnp.where(row == 0, before, pltpu.roll(u, 1, 0))
        xs = u + (prev - u) * mu_ref[...]
        yield
        r = xs[:, 0:A_WIDTH]
        k = xs[:, A_WIDTH:2 * A_WIDTH]
        v = xs[:, 2 * A_WIDTH:3 * A_WIDTH]
        lora = xs[:, 3 * A_WIDTH:]
        w_log = -_softplus(-(w0_ref[...] + _mm(jnp.tanh(lora), wup_ref[...]))) - 0.5
        a = _sigmoid(a0_ref[...] + _mm(lora, aup_ref[...]))
        yield
        kk = k * kk_ref[...]
        ss = _mm_exact_lhs(kk * kk, hsum_ref[...])
        kk = kk / jnp.maximum(jnp.sqrt(ss), 1e-12)
        yield
        kh = k * (1.0 + (a - 1.0) * ka_ref[...])
        lw = -jnp.exp(w_log)
        cin = jnp.concatenate([_mm_exact_rhs(tri, lw[chunk(c), :]) for c in range(cpg)], axis=0)
        yield
        ein = jnp.exp(cin)
        einv = jnp.exp(-cin)
        bt = kk * a * einv
        kt = kh * einv
        at_r[slot] = -kk * jnp.exp(cin - lw)
        rt_r[slot] = r * ein
        yield
        bt_r[slot] = bt
        kt_r[slot] = kt
        v_r[slot] = v
        bon_r[slot] = r * kh * rk_ref[...]
        yield
        for c in range(cpg):
            wl = ein[(c + 1) * L - 1:(c + 1) * L, :]
            wl_r[slot, c] = wl
            btw_r[slot, chunk(c), :] = bt[chunk(c), :] * wl
            ktw_r[slot, chunk(c), :] = kt[chunk(c), :] * wl
        yield

    def state_free(g):
        slot = lax.rem(jnp.clip(g - 1, 0, ng - 1), RWKV_RING)
        items = [(c, p, hm) for c in range(cpg) for p, hm in heads]
        load = lambda ring: [ring[slot, chunk(c), pair(p)] * hm for c, p, hm in items]
        at_h, rt_h, v_h = load(at_r), load(rt_r), load(v_r)
        lhs = [jnp.concatenate([x, y], axis=0).astype(BF16) for x, y in zip(at_h, rt_h)]
        yield
        sc_b = [_dg(x, bt_r[slot, chunk(c), pair(p)].astype(BF16), NT) for x, (c, p, _) in zip(lhs, items)]
        sc_k = [_dg(x, kt_r[slot, chunk(c), pair(p)].astype(BF16), NT) for x, (c, p, _) in zip(lhs, items)]
        yield
        a_ab = [jnp.where(strict, s[:L], 0.0) for s in sc_b]
        a_ak = [jnp.where(strict, s[:L], 0.0) for s in sc_k]
        m_rb = [jnp.where(incl, s[L:], 0.0) for s in sc_b]
        m_rk = [jnp.where(incl, s[L:], 0.0) for s in sc_k]
        tm = [eye + x for x in a_ab]
        ap = a_ab
        for _ in range(5):
            ap = [_mm(x, x) for x in ap]
            yield
            tm = [x + _mm(x, y) for x, y in zip(tm, ap)]
            yield
        x1 = [_mm(jnp.concatenate([x, y], axis=0), vh) for x, y, vh in zip(a_ak, m_rk, v_h)]
        yield
        wu = [_mm(x, jnp.concatenate([y, z[:L]], axis=1)) for x, y, z in zip(tm, at_h, x1)]
        yield
        for n in range(0, len(items), 2):
            c, p, _ = items[n]
            w_r[slot, chunk(c), pair(p)] = wu[n][:, :LANES] + wu[n + 1][:, :LANES]
            u0_r[slot, chunk(c), pair(p)] = wu[n][:, LANES:] + wu[n + 1][:, LANES:]
            x1_r[slot, chunk(c), pair(p)] = x1[n][L:] + x1[n + 1][L:]
            mrb_r[slot, c, 2 * p] = m_rb[n]
            mrb_r[slot, c, 2 * p + 1] = m_rb[n + 1]
        yield

    def state_step(g):
        valid = g >= 2
        g2 = jnp.clip(g - 2, 0, ng - 1)
        slot = lax.rem(g2, RWKV_RING)
        hm0, hm1 = head_masks
        y_pairs = [[] for _ in range(n_pairs)]
        for c in range(cpg):
            s_pairs = [s_scr[p] for p in range(n_pairs)]
            s_pairs_b = [s.astype(BF16) for s in s_pairs]
            w_h = [w_r[slot, chunk(c), pair(p)] * hm for p, hm in heads]
            rt_h = [rt_r[slot, chunk(c), pair(p)] * hm for p, hm in heads]
            st = [_dg(jnp.concatenate([x, y], axis=0).astype(BF16), s_pairs_b[p], NT)
                  for x, y, (p, _) in zip(w_h, rt_h, heads)]
            yield
            u_h = [u0_r[slot, chunk(c), pair(p)] * hm + s[:L] for s, (p, hm) in zip(st, heads)]
            y_h = [s[L:] + _mm(mrb_r[slot, c, n], x) for n, (s, x) in enumerate(zip(st, u_h))]
            upd = []
            for p in range(n_pairs):
                vp, bp, kp = v_r[slot, chunk(c), pair(p)], btw_r[slot, chunk(c), pair(p)], ktw_r[slot, chunk(c), pair(p)]
                upd.append(_mm(jnp.concatenate([u_h[2 * p], vp * hm0, u_h[2 * p + 1], vp * hm1], axis=0),
                               jnp.concatenate([bp * hm0, kp * hm0, bp * hm1, kp * hm1], axis=0), TN))
            yield
            wl = wl_r[slot, c]
            for p in range(n_pairs):
                s_scr[p] = jnp.where(valid, s_pairs[p] * wl[:, pair(p)] + upd[p], s_pairs[p])
                y_pairs[p].append(y_h[2 * p] + y_h[2 * p + 1] + x1_r[slot, chunk(c), pair(p)])
            yield
        rows = pl.ds(pl.multiple_of(g2 * gr, gr), gr)
        for p in range(n_pairs):
            y_pair = jnp.concatenate(y_pairs[p], axis=0)
            normed = jnp.zeros(y_pair.shape, F32)
            bonus = jnp.zeros(y_pair.shape, F32)
            for hm in head_masks:
                mu = jnp.sum(y_pair * hm, axis=-1, keepdims=True) * inv_d
                yc = (y_pair - mu) * hm
                var = jnp.sum(yc * yc, axis=-1, keepdims=True) * inv_d
                normed = normed + yc * lax.rsqrt(var + GN_EPS_A)
                bonus = bonus + (jnp.sum(bon_r[slot, :, pair(p)] * hm, axis=-1, keepdims=True)
                                 * (v_r[slot, :, pair(p)] * hm))
            y_o[0, rows, pair(p)] = (normed * gng_ref[:, pair(p)] + gnb_ref[:, pair(p)]) + bonus
            yield

    def trip(g, carry):
        _round_robin(prologue(g), state_free(g), state_step(g))
        return carry

    if ng == 1:
        for g, step in enumerate((prologue, state_free, state_step)):
            _round_robin(step(g))
    else:
        lax.fori_loop(0, ng + 2, trip, 0)
    prev_scr[...] = cols_ref[0, tt - 1:tt, :]
    last_o[0] = cols_ref[0, tt - 1:tt, :]
    s_o[0] = s_scr[...]


def _rwkv_call(cols, last, s0_pairs, lw):
    bsz, t, _ = cols.shape
    tt = _tile(t, 2048)
    gr = min(tt, 2 * CHUNK)
    blk = lambda w: pl.BlockSpec((1, tt, w), lambda b, i: (b, i, 0))
    row = pl.BlockSpec((1, 1, A_SHIFT), lambda b, i: (b, 0, 0))
    st = pl.BlockSpec((1, A_HEADS // 2, LANES, LANES), lambda b, i: (b, 0, 0, 0))
    ring = pltpu.VMEM((RWKV_RING, gr, A_WIDTH), F32)
    return pl.pallas_call(
        _rwkv_pipe_kernel,
        out_shape=(jax.ShapeDtypeStruct((bsz, t, A_WIDTH), F32),
                   jax.ShapeDtypeStruct((bsz, 1, A_SHIFT), F32),
                   jax.ShapeDtypeStruct((bsz, A_HEADS // 2, LANES, LANES), F32)),
        grid=(bsz, t // tt),
        in_specs=[blk(A_SHIFT), row, st, _const_spec((1, A_SHIFT)), _const_spec((1, A_WIDTH)),
                  _const_spec((2 * A_LORA, A_WIDTH)), _const_spec((1, A_WIDTH)), _const_spec((2 * A_LORA, A_WIDTH)),
                  _const_spec((1, A_WIDTH)), _const_spec((1, A_WIDTH)), _const_spec((A_WIDTH, A_WIDTH)),
                  _const_spec((1, A_WIDTH)), _const_spec((1, A_WIDTH)), _const_spec((1, A_WIDTH))],
        out_specs=(blk(A_WIDTH), row, st),
        scratch_shapes=[pltpu.VMEM((1, A_SHIFT), F32), pltpu.VMEM((A_HEADS // 2, LANES, LANES), F32)]
        + [ring] * 11
        + [pltpu.VMEM((RWKV_RING, gr // CHUNK, A_HEADS, CHUNK, CHUNK), F32),
           pltpu.VMEM((RWKV_RING, gr // CHUNK, 1, A_WIDTH), F32)],
        compiler_params=_params("parallel", "arbitrary"),
        name="rwkv",
    )(cols, last.reshape(bsz, 1, A_SHIFT), s0_pairs, lw["mu_a"], lw["w0_a"], lw["wup_pad"], lw["a0_a"],
      lw["aup_pad"], lw["k_k"], lw["k_a"], lw["head_sum"], lw["r_k"], lw["gn_a_g"], lw["gn_a_b"])


def _wkv_to_pairs(s):
    bsz = s.shape[0]
    s = s.reshape(bsz, A_HEADS // 2, 2, A_HEAD_DIM, A_HEAD_DIM)
    z = jnp.zeros_like(s[:, :, 0])
    top = jnp.concatenate([s[:, :, 0], z], axis=-1)
    bot = jnp.concatenate([z, s[:, :, 1]], axis=-1)
    return jnp.concatenate([top, bot], axis=-2)


def _wkv_from_pairs(sp):
    bsz = sp.shape[0]
    d = A_HEAD_DIM
    s = jnp.stack([sp[:, :, :d, :d], sp[:, :, d:, d:]], axis=2)
    return s.reshape(bsz, A_HEADS, d, d)


def _mlstm_kernel(qk_ref, v_ref, if_ref, o_ref, buf0_ref, c0_ref, n0_ref, m0_ref, cw_ref, cb_ref, hng_ref,
                  y_o, buf_o, c_o, n_o, m_o, ext_scr, c_scr, n_scr, m_scr):
    t = pl.program_id(1)
    tt = qk_ref.shape[1]
    L = min(tt, LANES)
    pad = SUBLANES

    @pl.when(t == 0)
    def _():
        ext_scr[0:pad, :] = buf0_ref[0]
        c_scr[...] = c0_ref[0]
        n_scr[...] = n0_ref[0]
        m_scr[...] = m0_ref[0]

    ext_scr[pad:pad + tt, :] = qk_ref[0]
    ri = lax.broadcasted_iota(jnp.int32, (L, L), 0)
    ci = lax.broadcasted_iota(jnp.int32, (L, L), 1)
    incl = ri >= ci
    eye = ri == ci
    tri = jnp.where(incl, 1.0, 0.0).astype(BF16)
    k_scale = B_HEAD_DIM ** -0.5

    n_chunks = tt // L
    group = 1
    hs = range(B_HEADS)
    sls = [slice(h * B_HEAD_DIM, (h + 1) * B_HEAD_DIM) for h in hs]

    def chunk_group(i, carry):
        rows, q_h, k_h, v_h, icol, bcol = [], [], [], [], [], []
        for j in range(group):
            base = pl.multiple_of((i * group + j) * L, L)
            x = ext_scr[pl.ds(base, L + pad), :]
            conv = cb_ref[...] + x[pad:, :] * cw_ref[B_CONV - 1:B_CONV, :]
            for tap in range(B_CONV - 1):
                conv = conv + pltpu.roll(x, B_CONV - 1 - tap, 0)[pad:, :] * cw_ref[tap:tap + 1, :]
            qk = _silu(conv)
            ifg = if_ref[0, pl.ds(base, L), :]
            bcum = _mm_exact_rhs(tri, -_softplus(-ifg))
            for h in hs:
                rows.append(pl.ds(base, L))
                q_h.append(qk[:, sls[h]])
                k_h.append(qk[:, B_WIDTH + h * B_HEAD_DIM:B_WIDTH + (h + 1) * B_HEAD_DIM] * k_scale)
                v_h.append(v_ref[0, pl.ds(base, L), sls[h]])
                icol.append(ifg[:, h:h + 1])
                bcol.append(bcum[:, B_HEADS + h:B_HEADS + h + 1])
        qk_t = [_mm(q, k, NT) for q, k in zip(q_h, k_h)]
        dcol = [i_ - b for i_, b in zip(icol, bcol)]
        drow = [jnp.sum(jnp.where(eye, d, 0.0), axis=0, keepdims=True) for d in dcol]
        dmat = [jnp.where(incl, b + d, -jnp.inf) for b, d in zip(bcol, drow)]
        dmax = [jnp.max(d, axis=-1, keepdims=True) for d in dmat]
        c_h = [c_scr[h] for h in hs]
        n_h = [n_scr[h] for h in hs]
        m_h = [m_scr[h][:, 0:1] for h in hs]
        for j in range(group):
            it = range(j * B_HEADS, (j + 1) * B_HEADS)
            qc = [_mm(q_h[n], c, NT) for n, c in zip(it, c_h)]
            inter = [bcol[n] + m for n, m in zip(it, m_h)]
            m_t = [jnp.maximum(x, dmax[n]) for n, x in zip(it, inter)]
            wmat = [jnp.exp(dmat[n] - m) * qk_t[n] for n, m in zip(it, m_t)]
            scale = [jnp.exp(x - m) for x, m in zip(inter, m_t)]
            wv = [_mm(w, v_h[n]) for n, w in zip(it, wmat)]
            m_h = [m[L - 1:L, :] for m in m_t]
            gcol = [jnp.exp(bcol[n][L - 1:L, :] + dcol[n] - m) for n, m in zip(it, m_h)]
            c_sc = [jnp.exp(x[L - 1:L, :] - m) for x, m in zip(inter, m_h)]
            gvk = [_mm(g * v_h[n], k_h[n], TN) for n, g in zip(it, gcol)]
            for h, n in zip(hs, it):
                num = wv[h] + scale[h] * qc[h]
                den = (jnp.sum(wmat[h], axis=-1, keepdims=True)
                       + scale[h] * jnp.sum(q_h[n] * n_h[h], axis=-1, keepdims=True))
                hid = num / jnp.maximum(jnp.abs(den), jnp.exp(-m_t[h]))
                y_o[0, rows[n], sls[h]] = (_sigmoid(o_ref[0, rows[n], sls[h]])
                                           * (_norm_rows(hid, HN_EPS) * hng_ref[:, sls[h]]))
            n_h = [s * n_ + jnp.sum(g * k_h[n], axis=0, keepdims=True) for n, s, n_, g in zip(it, c_sc, n_h, gcol)]
            c_h = [s * c + u for s, c, u in zip(c_sc, c_h, gvk)]
        for h in hs:
            c_scr[h] = c_h[h]
            n_scr[h] = n_h[h]
            m_scr[h] = jnp.broadcast_to(m_h[h], (1, LANES))
        return carry

    lax.fori_loop(0, n_chunks // group, chunk_group, 0)
    last = ext_scr[tt:tt + pad, :]
    ext_scr[0:pad, :] = last
    buf_o[0] = last
    c_o[0] = c_scr[...]
    n_o[0] = n_scr[...]
    m_o[0] = m_scr[...]


def _mlstm_call(qk, v, ifg, o, buf0, c0, n0, m0, lw):
    bsz, t, _ = v.shape
    tt = _tile(t, 512)
    blk = lambda w: pl.BlockSpec((1, tt, w), lambda b, i: (b, i, 0))
    fix = lambda *s: pl.BlockSpec((1,) + s, lambda b, i: (b,) + (0,) * len(s))
    return pl.pallas_call(
        _mlstm_kernel,
        out_shape=(jax.ShapeDtypeStruct((bsz, t, B_WIDTH), F32),
                   jax.ShapeDtypeStruct((bsz, SUBLANES, 2 * B_WIDTH), F32),
                   jax.ShapeDtypeStruct((bsz, B_HEADS, B_HEAD_DIM, B_HEAD_DIM), F32),
                   jax.ShapeDtypeStruct((bsz, B_HEADS, 1, B_HEAD_DIM), F32),
                   jax.ShapeDtypeStruct((bsz, B_HEADS, 1, LANES), F32)),
        grid=(bsz, t // tt),
        in_specs=[blk(2 * B_WIDTH), blk(B_WIDTH), blk(LANES), blk(B_WIDTH),
                  fix(SUBLANES, 2 * B_WIDTH), fix(B_HEADS, B_HEAD_DIM, B_HEAD_DIM), fix(B_HEADS, 1, B_HEAD_DIM),
                  fix(B_HEADS, 1, LANES),
                  _const_spec((B_CONV, 2 * B_WIDTH)), _const_spec((1, 2 * B_WIDTH)), _const_spec((1, B_WIDTH))],
        out_specs=(blk(B_WIDTH), fix(SUBLANES, 2 * B_WIDTH), fix(B_HEADS, B_HEAD_DIM, B_HEAD_DIM),
                   fix(B_HEADS, 1, B_HEAD_DIM), fix(B_HEADS, 1, LANES)),
        scratch_shapes=[pltpu.VMEM((tt + SUBLANES, 2 * B_WIDTH), F32),
                        pltpu.VMEM((B_HEADS, B_HEAD_DIM, B_HEAD_DIM), F32),
                        pltpu.VMEM((B_HEADS, 1, B_HEAD_DIM), F32),
                        pltpu.VMEM((B_HEADS, 1, LANES), F32)],
        compiler_params=_params("parallel", "arbitrary"),
        name="mlstm",
    )(qk, v, ifg, o, buf0, c0, n0, m0, lw["conv_b_w"], lw["conv_b_b"], lw["hn_b_g"])


SB_SKIP_LOG = -105.0


def _sb_kernel(*refs, n_past_blocks):
    if n_past_blocks:
        q_ref, k_ref, v_ref, kp_ref, vp_ref, o_ref, q2_scr, c_scr, acc_scr, pad_scr = refs
    else:
        q_ref, k_ref, v_ref, o_ref, q2_scr, c_scr, acc_scr, pad_scr = refs
    qi = pl.program_id(1)
    qb = q_ref.shape[1]
    kb = LANES
    n_pairs = C_HEADS // 2
    n_rows = C_HEADS * qb

    rj = lax.broadcasted_iota(jnp.int32, (2 * kb, 2 * kb), 0) & (kb - 1)
    cj = lax.broadcasted_iota(jnp.int32, (2 * kb, 2 * kb), 1)
    cum2 = jnp.where((cj >= kb) | (rj > cj), 1.0, 0.0).astype(BF16)
    qpos = lax.broadcasted_iota(jnp.int32, (n_rows, kb), 0) & (qb - 1)
    kpos = lax.broadcasted_iota(jnp.int32, (n_rows, kb), 1)
    diag_mask = kpos < qpos
    lane = lax.broadcasted_iota(jnp.int32, (1, LANES), 1)
    lo_lanes = lane < C_HEAD_DIM
    q = q_ref[0] * (C_HEAD_DIM ** -0.5)
    for p in range(n_pairs):
        qs = q[:, p * LANES:(p + 1) * LANES]
        q2_scr[p] = jnp.concatenate([jnp.where(lo_lanes, qs, 0.0), jnp.where(lo_lanes, 0.0, qs)], axis=0).astype(BF16)

    def load_t(ref, slot, p, keys):
        x = ref[0, 0, p * LANES:(p + 1) * LANES, keys]
        if x.shape[1] == kb:
            return x.astype(BF16)
        pad_scr[slot] = jnp.zeros((LANES, kb), F32)
        pad_scr[slot, :, 0:x.shape[1]] = x
        return pad_scr[slot].astype(BF16)

    def half_block(pairs, kr, vr, keys, first, c_max):
        rows = slice(pairs[0] * 2 * qb, (pairs[-1] + 1) * 2 * qb)
        mask = diag_mask[:rows.stop - rows.start]
        z = jnp.concatenate([_dg(q2_scr[p], load_t(kr, 2 * p, p, keys)) for p in pairs], axis=0)
        yield
        sp = _softplus(z)
        lk = jnp.where(mask, -sp, 0.0) if first else -sp
        h1 = lk.astype(BF16)
        h2 = (lk - h1.astype(F32)).astype(BF16)
        yield
        cs = _dg(jnp.concatenate([h1, h2], axis=1), cum2)
        yield
        if first:
            a = jnp.where(mask, jnp.exp((z - sp) + cs[:, :kb]), 0.0)
            c_new = cs[:, kb:]
        else:
            c_old = c_scr[rows]
            a = jnp.exp((z - sp) + cs[:, :kb] + c_old)
            c_new = c_old + cs[:, kb:]
        c_scr[rows] = c_new
        c_max.append(jnp.max(c_new))
        ab = a.astype(BF16)
        yield
        for n, p in enumerate(pairs):
            rs = slice(p * 2 * qb, (p + 1) * 2 * qb)
            upd = _dg(ab[n * 2 * qb:(n + 1) * 2 * qb], load_t(vr, 2 * p + 1, p, keys), NT)
            acc_scr[rs] = upd if first else acc_scr[rs] + upd
        yield

    def block(kr, vr, keys, first):
        c_max = []
        lead = half_block(tuple(range(n_pairs // 2)), kr, vr, keys, first, c_max)
        lag = half_block(tuple(range(n_pairs // 2, n_pairs)), kr, vr, keys, first, c_max)
        next(lead)
        _round_robin(lead, lag)
        return jnp.maximum(*c_max) > SB_SKIP_LOG

    own_keys = pl.ds(pl.multiple_of(qi * qb, qb), qb) if qb == kb else pl.ds(0, qb)
    go = block(k_ref, v_ref, own_keys, True)

    def sweep(kr, vr, n_blocks, go):
        def cond(carry):
            j, more = carry
            return jnp.logical_and(j < n_blocks, more)

        def body(carry):
            j, _ = carry
            keys = pl.ds(pl.multiple_of((n_blocks - 1 - j) * kb, kb), kb)
            return j + 1, block(kr, vr, keys, False)

        return lax.while_loop(cond, body, (jnp.int32(0), go))[1]

    if qb == kb:
        go = sweep(k_ref, v_ref, qi, go)
    if n_past_blocks:
        go = sweep(kp_ref, vp_ref, n_past_blocks, go)

    for p in range(n_pairs):
        r0 = p * 2 * qb
        o_ref[0, :, p * LANES:(p + 1) * LANES] = jnp.where(lo_lanes, acc_scr[r0:r0 + qb], acc_scr[r0 + qb:r0 + 2 * qb])


def _sb_call(q, kt, vt, layer, kt_past=None, vt_past=None):
    bsz, t, _ = q.shape
    qb = min(LANES, t)
    assert t % qb == 0 and (qb == LANES or t == qb)
    n_past = 0 if kt_past is None else kt_past.shape[-1]
    assert n_past % LANES == 0
    qblk = pl.BlockSpec((1, qb, C_WIDTH), lambda b, i: (b, i, 0))
    full = lambda n: pl.BlockSpec((1, 1, C_WIDTH, n), lambda b, i: (layer, b, 0, 0))
    in_specs = [qblk, full(t), full(t)]
    args = [q, kt, vt]
    if n_past:
        in_specs += [full(n_past), full(n_past)]
        args += [kt_past, vt_past]
    return pl.pallas_call(
        functools.partial(_sb_kernel, n_past_blocks=n_past // LANES),
        out_shape=jax.ShapeDtypeStruct((bsz, t, C_WIDTH), F32),
        grid=(bsz, t // qb),
        in_specs=in_specs,
        out_specs=qblk,
        scratch_shapes=[pltpu.VMEM((C_HEADS // 2, 2 * qb, LANES), BF16),
                        pltpu.VMEM((C_HEADS * qb, LANES), F32), pltpu.VMEM((C_HEADS * qb, LANES), F32),
                        pltpu.VMEM((C_HEADS, LANES, LANES), F32)],
        compiler_params=_params("parallel", "arbitrary"),
        name="stick_breaking",
    )(*args)


def _merge_kernel(x_ref, ya_ref, yb_ref, yc_ref, p_ref, wz_ref, bz_ref, wg_ref, bg_ref, wbr_ref, wout_ref,
                  lng_ref, lnb_ref, wple_ref, wpg_ref, o_ref):
    x = x_ref[...]
    xb = x.astype(BF16)
    width = A_WIDTH
    mix_pre = jnp.zeros(x.shape, F32)
    for n, y_ref in enumerate((ya_ref, yb_ref, yc_ref)):
        z = _dg(xb, wz_ref[:, n * width:(n + 1) * width]) + bz_ref[:, n * width:(n + 1) * width]
        ys = y_ref[...] * _silu(z)
        br = _mm(ys, wbr_ref[n])
        gate = _dg(xb, wg_ref[:, n * D_MODEL:(n + 1) * D_MODEL]) + bg_ref[:, n * D_MODEL:(n + 1) * D_MODEL]
        mix_pre = mix_pre + _sigmoid(gate) * br
    mix = _mm(mix_pre, wout_ref[...])
    x1 = _norm_rows(DEEPNORM_ALPHA * x + mix, LN_EPS) * lng_ref[...] + lnb_ref[...]
    o_ref[...] = x1 + _mm(p_ref[0], wple_ref[...]) * _sigmoid(_mm(x1, wpg_ref[...]))


def _merge_call(x, ya, yb, yc, p_all, layer, lw):
    n, d = x.shape
    tm = _tile(n, 512)
    rows = lambda w: pl.BlockSpec((tm, w), lambda i: (i, 0))
    return pl.pallas_call(
        _merge_kernel,
        out_shape=jax.ShapeDtypeStruct((n, d), F32),
        grid=(n // tm,),
        in_specs=[rows(d), rows(A_WIDTH), rows(B_WIDTH), rows(C_WIDTH),
                  pl.BlockSpec((1, tm, D_PLE), lambda i: (layer, i, 0)),
                  _const_spec((d, N_BRANCH * A_WIDTH)), _const_spec((1, N_BRANCH * A_WIDTH)),
                  _const_spec((d, N_BRANCH * d)), _const_spec((1, N_BRANCH * d)),
                  _const_spec((N_BRANCH, A_WIDTH, d)), _const_spec((d, d)),
                  _const_spec((1, d)), _const_spec((1, d)), _const_spec((D_PLE, d)), _const_spec((d, d))],
        out_specs=rows(d),
        compiler_params=_params("parallel"),
        name="merge",
    )(x, ya, yb, yc, p_all, lw["w_z"], lw["b_z"], lw["w_gates"], lw["b_gates"], lw["w_branch"], lw["w_out"],
      lw["ln_g"], lw["ln_b"], lw["w_ple"], lw["w_ple_gate"])


def _cols(w, *ranges):
    return jnp.concatenate([w[..., a:b] for a, b in ranges], axis=-1)


def _layer_params(i, w_in, b_in, prm):
    w = w_in[i]
    b = b_in[i]
    mix_ranges = ((_OFF_A_COLS, _OFF_A_Z), (_OFF_B_QK, _OFF_B_V), (_OFF_B_V, _OFF_B_I), (_OFF_B_O, _OFF_B_Z),
                  (_OFF_C_Q, _OFF_C_K), (_OFF_B_I, _OFF_B_O))
    pad = LANES - 2 * B_HEADS
    w_mix = jnp.pad(_cols(w, *mix_ranges), ((0, 0), (0, pad)))
    b_mix = jnp.pad(_cols(b, *mix_ranges), ((0, pad),))
    z_ranges = ((_OFF_A_Z, _OFF_B_QK), (_OFF_B_Z, _OFF_C_Q), (_OFF_C_Z, _OFF_GATES))
    zeros_lora = jnp.zeros((A_LORA, A_WIDTH), F32)
    hid = jnp.arange(A_WIDTH) // A_HEAD_DIM
    row = lambda name: prm[name][i].reshape(1, -1)
    return {
        "w_mix": w_mix.astype(BF16), "b_mix": b_mix.reshape(1, -1),
        "w_kvt": w[:, _OFF_C_K:_OFF_C_Z].T.astype(BF16), "b_kvt": b[_OFF_C_K:_OFF_C_Z].reshape(-1, 1),
        "w_z": _cols(w, *z_ranges).astype(BF16), "b_z": _cols(b, *z_ranges).reshape(1, -1),
        "w_gates": w[:, _OFF_GATES:].astype(BF16), "b_gates": b[_OFF_GATES:].reshape(1, -1),
        "mu_a": row("mu_a"), "w0_a": row("w0_a"), "a0_a": row("a0_a"), "k_k": row("k_k"), "k_a": row("k_a"),
        "wup_pad": jnp.concatenate([prm["w_decay_up"][i], zeros_lora], axis=0).astype(BF16),
        "aup_pad": jnp.concatenate([zeros_lora, prm["w_iclr_up"][i]], axis=0).astype(BF16),
        "head_sum": (hid[:, None] == hid[None, :]).astype(BF16),
        "r_k": row("r_k"), "gn_a_g": row("gn_a_g"), "gn_a_b": row("gn_a_b"),
        "conv_b_w": prm["conv_b_w"][i], "conv_b_b": row("conv_b_b"), "hn_b_g": row("hn_b_g"),
        "w_branch": prm["w_branch"][i].astype(BF16), "w_out": prm["w_out"][i].astype(BF16),
        "ln_g": row("ln_g"), "ln_b": row("ln_b"),
        "w_ple": prm["w_ple"][i].astype(BF16), "w_ple_gate": prm["w_ple_gate"][i].astype(BF16),
    }


def _run_group(x, p, init, ln_in_g, ln_in_b, layers):
    bsz, t, d = x.shape
    n = bsz * t
    shift0, wkv0, conv0, c0, n0, m0, k_past, v_past = init
    xf = _layer_norm_call(x.reshape(n, d), ln_in_g, ln_in_b)
    new = [[] for _ in range(6)]
    kt = vt = None
    channel_major = lambda a: jnp.transpose(a, (0, 1, 3, 4, 2)).reshape(a.shape[0], a.shape[1], C_WIDTH, a.shape[2])
    if k_past is not None:
        k_past, v_past = channel_major(k_past), channel_major(v_past)
    for i, lw in enumerate(layers):
        a_cols, b_qk, b_v, b_o, c_q, b_if, kt, vt = _inproj_call(xf, lw, i, bsz, t, kt, vt)
        seq = lambda a: a.reshape(bsz, t, a.shape[-1])
        ya, last, s_pairs = _rwkv_call(seq(a_cols), shift0[i], _wkv_to_pairs(wkv0[i]), lw)
        buf0 = jnp.pad(conv0[i], ((0, 0), (SUBLANES - (B_CONV - 1), 0), (0, 0)))
        m0b = jnp.broadcast_to(m0[i][:, :, None, None], (bsz, B_HEADS, 1, LANES))
        yb, buf, c1, n1, m1 = _mlstm_call(seq(b_qk), seq(b_v), seq(b_if), seq(b_o), buf0, c0[i],
                                          n0[i][:, :, None, :], m0b, lw)
        yc = _sb_call(seq(c_q), kt, vt, i, k_past, v_past)
        xf = _merge_call(xf, ya.reshape(n, A_WIDTH), yb.reshape(n, B_WIDTH), yc.reshape(n, C_WIDTH),
                         p.reshape(DEPTH, n, D_PLE), i, lw)
        states = (last[:, 0], _wkv_from_pairs(s_pairs), buf[:, SUBLANES - (B_CONV - 1):], c1, n1[:, :, 0],
                  m1[:, :, 0, 0])
        for lst, s in zip(new, states):
            lst.append(s)
    token_major = lambda a: jnp.transpose(a.reshape(DEPTH, bsz, C_HEADS, C_HEAD_DIM, t), (0, 1, 4, 2, 3))
    return xf.reshape(bsz, t, d), [jnp.stack(lst) for lst in new] + [token_major(kt), token_major(vt)]


def kernel(x_prompt, x_sample, state_shift_a, state_wkv, state_conv_b, state_mlstm_c, state_mlstm_n, state_mlstm_m, cache_sb_k, cache_sb_v, p_prompt, p_sample, ln_in_g, ln_in_b, w_in, b_in, mu_a, w0_a, w_decay_up, a0_a, w_iclr_up, k_k, k_a, r_k, gn_a_g, gn_a_b, conv_b_w, conv_b_b, hn_b_g, w_branch, w_out, ln_g, ln_b, w_ple, w_ple_gate):
    prm = dict(mu_a=mu_a, w0_a=w0_a, w_decay_up=w_decay_up, a0_a=a0_a, w_iclr_up=w_iclr_up, k_k=k_k, k_a=k_a,
               r_k=r_k, gn_a_g=gn_a_g, gn_a_b=gn_a_b, conv_b_w=conv_b_w, conv_b_b=conv_b_b, hn_b_g=hn_b_g,
               w_branch=w_branch, w_out=w_out, ln_g=ln_g, ln_b=ln_b, w_ple=w_ple, w_ple_gate=w_ple_gate)
    layers = [_layer_params(i, w_in, b_in, prm) for i in range(DEPTH)]
    bp = x_prompt.shape[0]
    zeros = lambda *s: jnp.zeros((DEPTH, bp) + s, F32)
    init_prompt = (zeros(A_SHIFT), zeros(A_HEADS, A_HEAD_DIM, A_HEAD_DIM), zeros(B_CONV - 1, 2 * B_WIDTH),
                   zeros(B_HEADS, B_HEAD_DIM, B_HEAD_DIM), zeros(B_HEADS, B_HEAD_DIM), zeros(B_HEADS), None, None)
    init_sample = (state_shift_a, state_wkv, state_conv_b, state_mlstm_c, state_mlstm_n, state_mlstm_m,
                   cache_sb_k, cache_sb_v)
    y_prompt, sp = _run_group(x_prompt, p_prompt, init_prompt, ln_in_g, ln_in_b, layers)
    y_sample, ss = _run_group(x_sample, p_sample, init_sample, ln_in_g, ln_in_b, layers)
    return (y_prompt, y_sample, sp[0], ss[0], sp[1], ss[1], sp[2], ss[2], sp[3], ss[3],
            sp[4], ss[4], sp[5], ss[5], sp[6], ss[6], sp[7], ss[7])
```

```python
import functools

import jax
import jax.numpy as jnp
from jax import lax
from jax.experimental import pallas as pl
from jax.experimental.pallas import tpu as pltpu

F32 = jnp.float32
BF16 = jnp.bfloat16

D_MODEL = 1024
DEPTH = 4
CHUNK = 64
D_PLE = 256
A_HEADS = 8
A_HEAD_DIM = 64
A_WIDTH = A_HEADS * A_HEAD_DIM
A_LORA = 64
A_SHIFT = 3 * A_WIDTH + 2 * A_LORA
B_HEADS = 4
B_HEAD_DIM = 128
B_WIDTH = B_HEADS * B_HEAD_DIM
B_CONV = 4
C_HEADS = 8
C_HEAD_DIM = 64
C_WIDTH = C_HEADS * C_HEAD_DIM
N_BRANCH = 3
DEEPNORM_ALPHA = (2 * DEPTH) ** 0.25
LN_EPS = 1e-5
GN_EPS_A = 64e-5
HN_EPS = 1e-6

LANES = 128
SUBLANES = 8
VMEM_LIMIT_BYTES = 56 * 1024 * 1024

_OFF_A_COLS = 0
_OFF_A_Z = _OFF_A_COLS + A_SHIFT
_OFF_B_QK = _OFF_A_Z + A_WIDTH
_OFF_B_V = _OFF_B_QK + 2 * B_WIDTH
_OFF_B_I = _OFF_B_V + B_WIDTH
_OFF_B_F = _OFF_B_I + B_HEADS
_OFF_B_O = _OFF_B_F + B_HEADS
_OFF_B_Z = _OFF_B_O + B_WIDTH
_OFF_C_Q = _OFF_B_Z + B_WIDTH
_OFF_C_K = _OFF_C_Q + C_WIDTH
_OFF_C_V = _OFF_C_K + C_WIDTH
_OFF_C_Z = _OFF_C_V + C_WIDTH
_OFF_GATES = _OFF_C_Z + C_WIDTH
_N_IN = _OFF_GATES + N_BRANCH * D_MODEL

_MIX_WIDTHS = (A_SHIFT, 2 * B_WIDTH, B_WIDTH, B_WIDTH, C_WIDTH, LANES)

NN = ((1,), (0,))
NT = ((1,), (1,))
TN = ((0,), (0,))


def _dg(a, b, dims=NN):
    return lax.dot_general(a, b, (dims, ((), ())), preferred_element_type=F32)


def _mm(a, b, dims=NN):
    return _dg(a.astype(BF16), b.astype(BF16), dims)


def _split3(x):
    h1 = x.astype(BF16)
    r1 = x - h1.astype(F32)
    h2 = r1.astype(BF16)
    h3 = (r1 - h2.astype(F32)).astype(BF16)
    return h1, h2, h3


def _mm_exact_rhs(lhs_bf16, x):
    h1, h2, h3 = _split3(x)
    return _dg(lhs_bf16, h1) + (_dg(lhs_bf16, h2) + _dg(lhs_bf16, h3))


def _mm_exact_lhs(x, rhs_bf16):
    h1, h2, h3 = _split3(x)
    return _dg(h1, rhs_bf16) + (_dg(h2, rhs_bf16) + _dg(h3, rhs_bf16))


def _softplus(x):
    return jnp.maximum(x, 0.0) + jnp.log(1.0 + jnp.exp(-jnp.abs(x)))


def _sigmoid(x):
    return jax.nn.sigmoid(x)


def _silu(x):
    return x * jax.nn.sigmoid(x)


def _norm_rows(x, eps):
    mu = jnp.mean(x, axis=-1, keepdims=True)
    xc = x - mu
    var = jnp.mean(xc * xc, axis=-1, keepdims=True)
    return xc * lax.rsqrt(var + eps)


def _tile(n, pref):
    if n <= pref:
        return n
    t = pref - pref % SUBLANES
    while n % t:
        t -= SUBLANES
    return t


def _params(*sem):
    return pltpu.CompilerParams(dimension_semantics=sem, vmem_limit_bytes=VMEM_LIMIT_BYTES)


def _const_spec(shape):
    nd = len(shape)
    return pl.BlockSpec(shape, lambda *_: (0,) * nd, pipeline_mode=pl.Buffered(1))


def _ln_kernel(x_ref, g_ref, b_ref, o_ref):
    o_ref[...] = _norm_rows(x_ref[...], LN_EPS) * g_ref[...] + b_ref[...]


def _layer_norm_call(x, g, b):
    n, d = x.shape
    tm = _tile(n, 512)
    return pl.pallas_call(
        _ln_kernel,
        out_shape=jax.ShapeDtypeStruct((n, d), F32),
        grid=(n // tm,),
        in_specs=[pl.BlockSpec((tm, d), lambda i: (i, 0)), _const_spec((1, d)), _const_spec((1, d))],
        out_specs=pl.BlockSpec((tm, d), lambda i: (i, 0)),
        compiler_params=_params("parallel"),
        name="ln_in",
    )(x, g.reshape(1, d), b.reshape(1, d))


def _inproj_kernel(x_ref, w_ref, b_ref, wkv_ref, bkv_ref, *refs):
    n_out = len(_MIX_WIDTHS) + 2
    out_refs = refs[-n_out:]
    xb = x_ref[...].astype(BF16)
    off = 0
    for o_ref, width in zip(out_refs, _MIX_WIDTHS):
        o_ref[...] = _dg(xb, w_ref[:, off:off + width]) + b_ref[:, off:off + width]
        off += width
    kvt = _dg(wkv_ref[...], xb, NT) + bkv_ref[...]
    out_refs[-2][0, 0] = kvt[:C_WIDTH]
    out_refs[-1][0, 0] = kvt[C_WIDTH:]


def _inproj_call(x, lw, layer, bsz, t, kt_stack=None, vt_stack=None):
    n, d = x.shape
    tm = _tile(t, 512)
    per_b = t // tm
    wtot = sum(_MIX_WIDTHS)
    stack = jax.ShapeDtypeStruct((DEPTH, bsz, C_WIDTH, t), F32)
    stack_spec = pl.BlockSpec((1, 1, C_WIDTH, tm), lambda i: (layer, i // per_b, 0, i % per_b))
    in_specs = [pl.BlockSpec((tm, d), lambda i: (i, 0)), _const_spec((d, wtot)), _const_spec((1, wtot)),
                _const_spec((2 * C_WIDTH, d)), _const_spec((2 * C_WIDTH, 1))]
    args = [x, lw["w_mix"], lw["b_mix"], lw["w_kvt"], lw["b_kvt"]]
    aliases = {}
    if kt_stack is not None:
        n_mix = len(_MIX_WIDTHS)
        aliases = {len(args): n_mix, len(args) + 1: n_mix + 1}
        in_specs += [pl.BlockSpec(memory_space=pl.ANY)] * 2
        args += [kt_stack, vt_stack]
    return pl.pallas_call(
        _inproj_kernel,
        out_shape=tuple(jax.ShapeDtypeStruct((n, wd), F32) for wd in _MIX_WIDTHS) + (stack, stack),
        grid=(n // tm,),
        in_specs=in_specs,
        out_specs=tuple(pl.BlockSpec((tm, wd), lambda i: (i, 0)) for wd in _MIX_WIDTHS) + (stack_spec, stack_spec),
        input_output_aliases=aliases,
        compiler_params=_params("parallel"),
        name="inproj",
    )(*args)


RWKV_RING = 3


def _round_robin(*steps):
    steps = list(steps)
    while steps:
        for s in list(steps):
            try:
                next(s)
            except StopIteration:
                steps.remove(s)


def _rwkv_pipe_kernel(cols_ref, last_ref, s0_ref, mu_ref, w0_ref, wup_ref, a0_ref, aup_ref, kk_ref, ka_ref, hsum_ref,
                      rk_ref, gng_ref, gnb_ref, y_o, last_o, s_o,
                      prev_scr, s_scr, at_r, rt_r, bt_r, kt_r, btw_r, ktw_r, v_r, bon_r, w_r, u0_r, x1_r, mrb_r, wl_r):
    t = pl.program_id(1)
    L = CHUNK
    tt = cols_ref.shape[1]
    gr = min(tt, 2 * L)
    cpg = gr // L
    ng = tt // gr
    n_pairs = A_HEADS // 2
    rings = (at_r, rt_r, bt_r, kt_r, btw_r, ktw_r, v_r, bon_r, w_r, u0_r, x1_r, mrb_r, wl_r)

    @pl.when(t == 0)
    def _():
        prev_scr[...] = last_ref[0]
        s_scr[...] = s0_ref[0]
        if ng > 1:
            for ring in rings:
                ring[...] = jnp.zeros(ring.shape, F32)

    ri = lax.broadcasted_iota(jnp.int32, (L, L), 0)
    ci = lax.broadcasted_iota(jnp.int32, (L, L), 1)
    strict = ri > ci
    incl = ri >= ci
    tri = jnp.where(incl, 1.0, 0.0).astype(BF16)
    eye = jnp.where(ri == ci, 1.0, 0.0)
    lane = lax.broadcasted_iota(jnp.int32, (1, LANES), 1)
    head_masks = (jnp.where(lane < A_HEAD_DIM, 1.0, 0.0), jnp.where(lane >= A_HEAD_DIM, 1.0, 0.0))
    inv_d = 1.0 / A_HEAD_DIM
    pair = lambda p: slice(p * LANES, (p + 1) * LANES)
    chunk = lambda c: slice(c * L, (c + 1) * L)
    heads = [(p, hm) for p in range(n_pairs) for hm in head_masks]

    def prologue(g):
        gp = jnp.minimum(g, ng - 1)
        slot = lax.rem(gp, RWKV_RING)
        r0 = pl.multiple_of(gp * gr, gr)
        u = cols_ref[0, pl.ds(r0, gr), :]
        above = cols_ref[0, pl.ds(pl.multiple_of(jnp.maximum(r0 - SUBLANES, 0), SUBLANES), SUBLANES), :]
        before = jnp.where(gp == 0, prev_scr[...], above[SUBLANES - 1:SUBLANES, :])
        row = lax.broadcasted_iota(jnp.int32, u.shape, 0)
        prev = jnp.where(row == 0, before, pltpu.roll(u, 1, 0))
        xs = u + (prev - u) * mu_ref[...]
        yield
        r = xs[:, 0:A_WIDTH]
        k = xs[:, A_WIDTH:2 * A_WIDTH]
        v = xs[:, 2 * A_WIDTH:3 * A_WIDTH]
        lora = xs[:, 3 * A_WIDTH:]
        w_log = -_softplus(-(w0_ref[...] + _mm(jnp.tanh(lora), wup_ref[...]))) - 0.5
        a = _sigmoid(a0_ref[...] + _mm(lora, aup_ref[...]))
        yield
        kk = k * kk_ref[...]
        ss = _mm_exact_lhs(kk * kk, hsum_ref[...])
        kk = kk / jnp.maximum(jnp.sqrt(ss), 1e-12)
        yield
        kh = k * (1.0 + (a - 1.0) * ka_ref[...])
        lw = -jnp.exp(w_log)
        cin = jnp.concatenate([_mm_exact_rhs(tri, lw[chunk(c), :]) for c in range(cpg)], axis=0)
        yield
        ein = jnp.exp(cin)
        einv = jnp.exp(-cin)
        bt = kk * a * einv
        kt = kh * einv
        at_r[slot] = -kk * jnp.exp(cin - lw)
        rt_r[slot] = r * ein
        yield
        bt_r[slot] = bt
        kt_r[slot] = kt
        v_r[slot] = v
        bon_r[slot] = r * kh * rk_ref[...]
        yield
        for c in range(cpg):
            wl = ein[(c + 1) * L - 1:(c + 1) * L, :]
            wl_r[slot, c] = wl
            btw_r[slot, chunk(c), :] = bt[chunk(c), :] * wl
            ktw_r[slot, chunk(c), :] = kt[chunk(c), :] * wl
        yield

    def state_free(g):
        slot = lax.rem(jnp.clip(g - 1, 0, ng - 1), RWKV_RING)
        items = [(c, p, hm) for c in range(cpg) for p, hm in heads]
        load = lambda ring: [ring[slot, chunk(c), pair(p)] * hm for c, p, hm in items]
        at_h, rt_h, v_h = load(at_r), load(rt_r), load(v_r)
        lhs = [jnp.concatenate([x, y], axis=0).astype(BF16) for x, y in zip(at_h, rt_h)]
        yield
        sc_b = [_dg(x, bt_r[slot, chunk(c), pair(p)].astype(BF16), NT) for x, (c, p, _) in zip(lhs, items)]
        sc_k = [_dg(x, kt_r[slot, chunk(c), pair(p)].astype(BF16), NT) for x, (c, p, _) in zip(lhs, items)]
        yield
        a_ab = [jnp.where(strict, s[:L], 0.0) for s in sc_b]
        a_ak = [jnp.where(strict, s[:L], 0.0) for s in sc_k]
        m_rb = [jnp.where(incl, s[L:], 0.0) for s in sc_b]
        m_rk = [jnp.where(incl, s[L:], 0.0) for s in sc_k]
        tm = [eye + x for x in a_ab]
        ap = a_ab
        for _ in range(5):
            ap = [_mm(x, x) for x in ap]
            yield
            tm = [x + _mm(x, y) for x, y in zip(tm, ap)]
            yield
        x1 = [_mm(jnp.concatenate([x, y], axis=0), vh) for x, y, vh in zip(a_ak, m_rk, v_h)]
        yield
        wu = [_mm(x, jnp.concatenate([y, z[:L]], axis=1)) for x, y, z in zip(tm, at_h, x1)]
        yield
        for n in range(0, len(items), 2):
            c, p, _ = items[n]
            w_r[slot, chunk(c), pair(p)] = wu[n][:, :LANES] + wu[n + 1][:, :LANES]
            u0_r[slot, chunk(c), pair(p)] = wu[n][:, LANES:] + wu[n + 1][:, LANES:]
            x1_r[slot, chunk(c), pair(p)] = x1[n][L:] + x1[n + 1][L:]
            mrb_r[slot, c, 2 * p] = m_rb[n]
            mrb_r[slot, c, 2 * p + 1] = m_rb[n + 1]
        yield

    def state_step(g):
        valid = g >= 2
        g2 = jnp.clip(g - 2, 0, ng - 1)
        slot = lax.rem(g2, RWKV_RING)
        hm0, hm1 = head_masks
        y_pairs = [[] for _ in range(n_pairs)]
        for c in range(cpg):
            s_pairs = [s_scr[p] for p in range(n_pairs)]
            s_pairs_b = [s.astype(BF16) for s in s_pairs]
            w_h = [w_r[slot, chunk(c), pair(p)] * hm for p, hm in heads]
            rt_h = [rt_r[slot, chunk(c), pair(p)] * hm for p, hm in heads]
            st = [_dg(jnp.concatenate([x, y], axis=0).astype(BF16), s_pairs_b[p], NT)
                  for x, y, (p, _) in zip(w_h, rt_h, heads)]
            yield
            u_h = [u0_r[slot, chunk(c), pair(p)] * hm + s[:L] for s, (p, hm) in zip(st, heads)]
            y_h = [s[L:] + _mm(mrb_r[slot, c, n], x) for n, (s, x) in enumerate(zip(st, u_h))]
            upd = []
            for p in range(n_pairs):
                vp, bp, kp = v_r[slot, chunk(c), pair(p)], btw_r[slot, chunk(c), pair(p)], ktw_r[slot, chunk(c), pair(p)]
                upd.append(_mm(jnp.concatenate([u_h[2 * p], vp * hm0, u_h[2 * p + 1], vp * hm1], axis=0),
                               jnp.concatenate([bp * hm0, kp * hm0, bp * hm1, kp * hm1], axis=0), TN))
            yield
            wl = wl_r[slot, c]
            for p in range(n_pairs):
                s_scr[p] = jnp.where(valid, s_pairs[p] * wl[:, pair(p)] + upd[p], s_pairs[p])
                y_pairs[p].append(y_h[2 * p] + y_h[2 * p + 1] + x1_r[slot, chunk(c), pair(p)])
            yield
        rows = pl.ds(pl.multiple_of(g2 * gr, gr), gr)
        for p in range(n_pairs):
            y_pair = jnp.concatenate(y_pairs[p], axis=0)
            normed = jnp.zeros(y_pair.shape, F32)
            bonus = jnp.zeros(y_pair.shape, F32)
            for hm in head_masks:
                mu = jnp.sum(y_pair * hm, axis=-1, keepdims=True) * inv_d
                yc = (y_pair - mu) * hm
                var = jnp.sum(yc * yc, axis=-1, keepdims=True) * inv_d
                normed = normed + yc * lax.rsqrt(var + GN_EPS_A)
                bonus = bonus + (jnp.sum(bon_r[slot, :, pair(p)] * hm, axis=-1, keepdims=True)
                                 * (v_r[slot, :, pair(p)] * hm))
            y_o[0, rows, pair(p)] = (normed * gng_ref[:, pair(p)] + gnb_ref[:, pair(p)]) + bonus
            yield

    def trip(g, carry):
        _round_robin(prologue(g), state_free(g), state_step(g))
        return carry

    if ng == 1:
        for g, step in enumerate((prologue, state_free, state_step)):
            _round_robin(step(g))
    else:
        lax.fori_loop(0, ng + 2, trip, 0)
    prev_scr[...] = cols_ref[0, tt - 1:tt, :]
    last_o[0] = cols_ref[0, tt - 1:tt, :]
    s_o[0] = s_scr[...]


def _rwkv_call(cols, last, s0_pairs, lw):
    bsz, t, _ = cols.shape
    tt = _tile(t, 2048)
    gr = min(tt, 2 * CHUNK)
    blk = lambda w: pl.BlockSpec((1, tt, w), lambda b, i: (b, i, 0))
    row = pl.BlockSpec((1, 1, A_SHIFT), lambda b, i: (b, 0, 0))
    st = pl.BlockSpec((1, A_HEADS // 2, LANES, LANES), lambda b, i: (b, 0, 0, 0))
    ring = pltpu.VMEM((RWKV_RING, gr, A_WIDTH), F32)
    return pl.pallas_call(
        _rwkv_pipe_kernel,
        out_shape=(jax.ShapeDtypeStruct((bsz, t, A_WIDTH), F32),
                   jax.ShapeDtypeStruct((bsz, 1, A_SHIFT), F32),
                   jax.ShapeDtypeStruct((bsz, A_HEADS // 2, LANES, LANES), F32)),
        grid=(bsz, t // tt),
        in_specs=[blk(A_SHIFT), row, st, _const_spec((1, A_SHIFT)), _const_spec((1, A_WIDTH)),
                  _const_spec((2 * A_LORA, A_WIDTH)), _const_spec((1, A_WIDTH)), _const_spec((2 * A_LORA, A_WIDTH)),
                  _const_spec((1, A_WIDTH)), _const_spec((1, A_WIDTH)), _const_spec((A_WIDTH, A_WIDTH)),
                  _const_spec((1, A_WIDTH)), _const_spec((1, A_WIDTH)), _const_spec((1, A_WIDTH))],
        out_specs=(blk(A_WIDTH), row, st),
        scratch_shapes=[pltpu.VMEM((1, A_SHIFT), F32), pltpu.VMEM((A_HEADS // 2, LANES, LANES), F32)]
        + [ring] * 11
        + [pltpu.VMEM((RWKV_RING, gr // CHUNK, A_HEADS, CHUNK, CHUNK), F32),
           pltpu.VMEM((RWKV_RING, gr // CHUNK, 1, A_WIDTH), F32)],
        compiler_params=_params("parallel", "arbitrary"),
        name="rwkv",
    )(cols, last.reshape(bsz, 1, A_SHIFT), s0_pairs, lw["mu_a"], lw["w0_a"], lw["wup_pad"], lw["a0_a"],
      lw["aup_pad"], lw["k_k"], lw["k_a"], lw["head_sum"], lw["r_k"], lw["gn_a_g"], lw["gn_a_b"])


def _wkv_to_pairs(s):
    bsz = s.shape[0]
    s = s.reshape(bsz, A_HEADS // 2, 2, A_HEAD_DIM, A_HEAD_DIM)
    z = jnp.zeros_like(s[:, :, 0])
    top = jnp.concatenate([s[:, :, 0], z], axis=-1)
    bot = jnp.concatenate([z, s[:, :, 1]], axis=-1)
    return jnp.concatenate([top, bot], axis=-2)


def _wkv_from_pairs(sp):
    bsz = sp.shape[0]
    d = A_HEAD_DIM
    s = jnp.stack([sp[:, :, :d, :d], sp[:, :, d:, d:]], axis=2)
    return s.reshape(bsz, A_HEADS, d, d)


def _mlstm_steps(t, qk_ref, v_ref, if_ref, o_ref, buf0_ref, c0_ref, n0_ref, m0_ref, cw_ref, cb_ref, hng_ref,
                 y_o, buf_o, c_o, n_o, m_o, ext_scr, c_scr, n_scr, m_scr):
    tt = qk_ref.shape[1]
    L = min(tt, LANES)
    assert tt == L
    pad = SUBLANES

    @pl.when(t == 0)
    def _():
        ext_scr[0:pad, :] = buf0_ref[0]
        c_scr[...] = c0_ref[0]
        n_scr[...] = n0_ref[0]
        m_scr[...] = m0_ref[0]

    ext_scr[pad:pad + tt, :] = qk_ref[0]
    ri = lax.broadcasted_iota(jnp.int32, (L, L), 0)
    ci = lax.broadcasted_iota(jnp.int32, (L, L), 1)
    incl = ri >= ci
    eye = ri == ci
    tri = jnp.where(incl, 1.0, 0.0).astype(BF16)
    k_scale = B_HEAD_DIM ** -0.5

    hs = range(B_HEADS)
    sls = [slice(h * B_HEAD_DIM, (h + 1) * B_HEAD_DIM) for h in hs]

    x = ext_scr[...]
    conv = cb_ref[...] + x[pad:, :] * cw_ref[B_CONV - 1:B_CONV, :]
    for tap in range(B_CONV - 1):
        conv = conv + pltpu.roll(x, B_CONV - 1 - tap, 0)[pad:, :] * cw_ref[tap:tap + 1, :]
    yield
    qk = _silu(conv)
    yield
    ifg = if_ref[0]
    bcum = _mm_exact_rhs(tri, -_softplus(-ifg))
    yield
    q_h = [qk[:, sl] for sl in sls]
    k_h = [qk[:, B_WIDTH + h * B_HEAD_DIM:B_WIDTH + (h + 1) * B_HEAD_DIM] * k_scale for h in hs]
    v_h = [v_ref[0, :, sl] for sl in sls]
    icol = [ifg[:, h:h + 1] for h in hs]
    bcol = [bcum[:, B_HEADS + h:B_HEADS + h + 1] for h in hs]
    qk_t = [_mm(q, k, NT) for q, k in zip(q_h, k_h)]
    yield
    dcol = [i_ - b for i_, b in zip(icol, bcol)]
    drow = [jnp.sum(jnp.where(eye, d, 0.0), axis=0, keepdims=True) for d in dcol]
    dmat = [jnp.where(incl, b + d, -jnp.inf) for b, d in zip(bcol, drow)]
    yield
    dmax = [jnp.max(d, axis=-1, keepdims=True) for d in dmat]
    yield
    c_h = [c_scr[h] for h in hs]
    n_h = [n_scr[h] for h in hs]
    m_h = [m_scr[h][:, 0:1] for h in hs]
    qc = [_mm(q, c, NT) for q, c in zip(q_h, c_h)]
    yield
    inter = [b + m for b, m in zip(bcol, m_h)]
    m_t = [jnp.maximum(x_, d) for x_, d in zip(inter, dmax)]
    wmat = [jnp.exp(d - m) * s for d, m, s in zip(dmat, m_t, qk_t)]
    yield
    scale = [jnp.exp(x_ - m) for x_, m in zip(inter, m_t)]
    wv = [_mm(w, v) for w, v in zip(wmat, v_h)]
    yield
    m_new = [m[L - 1:L, :] for m in m_t]
    gcol = [jnp.exp(b[L - 1:L, :] + d - m) for b, d, m in zip(bcol, dcol, m_new)]
    c_sc = [jnp.exp(x_[L - 1:L, :] - m) for x_, m in zip(inter, m_new)]
    gvk = [_mm(g * v, k, TN) for g, v, k in zip(gcol, v_h, k_h)]
    yield
    for h in hs:
        num = wv[h] + scale[h] * qc[h]
        den = (jnp.sum(wmat[h], axis=-1, keepdims=True)
               + scale[h] * jnp.sum(q_h[h] * n_h[h], axis=-1, keepdims=True))
        hid = num / jnp.maximum(jnp.abs(den), jnp.exp(-m_t[h]))
        y_o[0, :, sls[h]] = _sigmoid(o_ref[0, :, sls[h]]) * (_norm_rows(hid, HN_EPS) * hng_ref[:, sls[h]])
        yield
    for h in hs:
        c_scr[h] = c_sc[h] * c_h[h] + gvk[h]
        n_scr[h] = c_sc[h] * n_h[h] + jnp.sum(gcol[h] * k_h[h], axis=0, keepdims=True)
        m_scr[h] = jnp.broadcast_to(m_new[h], (1, LANES))
    last = ext_scr[tt:tt + pad, :]
    ext_scr[0:pad, :] = last
    buf_o[0] = last
    c_o[0] = c_scr[...]
    n_o[0] = n_scr[...]
    m_o[0] = m_scr[...]


SB_SKIP_LOG = -105.0


_N_MLSTM_IN, _N_MLSTM_OUT, _N_SB_SCRATCH = 11, 5, 4


def _sb_mlstm_kernel(*refs, n_past_blocks):
    n_att = 5 if n_past_blocks else 3
    att_in, refs = refs[:n_att], refs[n_att:]
    ml_in, o_ref, refs = refs[:_N_MLSTM_IN], refs[_N_MLSTM_IN], refs[_N_MLSTM_IN + 1:]
    ml_out, refs = refs[:_N_MLSTM_OUT], refs[_N_MLSTM_OUT:]
    (q2_scr, c_scr, acc_scr, pad_scr), ml_scr = refs[:_N_SB_SCRATCH], refs[_N_SB_SCRATCH:]
    if n_past_blocks:
        q_ref, k_ref, v_ref, kp_ref, vp_ref = att_in
    else:
        q_ref, k_ref, v_ref = att_in
    mlstm = _mlstm_steps(pl.program_id(1), *ml_in, *ml_out, *ml_scr)
    qi = pl.program_id(1)
    qb = q_ref.shape[1]
    kb = LANES
    n_pairs = C_HEADS // 2
    n_rows = C_HEADS * qb

    rj = lax.broadcasted_iota(jnp.int32, (2 * kb, 2 * kb), 0) & (kb - 1)
    cj = lax.broadcasted_iota(jnp.int32, (2 * kb, 2 * kb), 1)
    cum2 = jnp.where((cj >= kb) | (rj > cj), 1.0, 0.0).astype(BF16)
    qpos = lax.broadcasted_iota(jnp.int32, (n_rows, kb), 0) & (qb - 1)
    kpos = lax.broadcasted_iota(jnp.int32, (n_rows, kb), 1)
    diag_mask = kpos < qpos
    lane = lax.broadcasted_iota(jnp.int32, (1, LANES), 1)
    lo_lanes = lane < C_HEAD_DIM
    q = q_ref[0] * (C_HEAD_DIM ** -0.5)
    for p in range(n_pairs):
        qs = q[:, p * LANES:(p + 1) * LANES]
        q2_scr[p] = jnp.concatenate([jnp.where(lo_lanes, qs, 0.0), jnp.where(lo_lanes, 0.0, qs)], axis=0).astype(BF16)

    def load_t(ref, slot, p, keys):
        x = ref[0, 0, p * LANES:(p + 1) * LANES, keys]
        if x.shape[1] == kb:
            return x.astype(BF16)
        pad_scr[slot] = jnp.zeros((LANES, kb), F32)
        pad_scr[slot, :, 0:x.shape[1]] = x
        return pad_scr[slot].astype(BF16)

    def half_block(pairs, kr, vr, keys, first, c_max):
        rows = slice(pairs[0] * 2 * qb, (pairs[-1] + 1) * 2 * qb)
        mask = diag_mask[:rows.stop - rows.start]
        z = jnp.concatenate([_dg(q2_scr[p], load_t(kr, 2 * p, p, keys)) for p in pairs], axis=0)
        yield
        sp = _softplus(z)
        lk = jnp.where(mask, -sp, 0.0) if first else -sp
        h1 = lk.astype(BF16)
        h2 = (lk - h1.astype(F32)).astype(BF16)
        yield
        cs = _dg(jnp.concatenate([h1, h2], axis=1), cum2)
        yield
        if first:
            a = jnp.where(mask, jnp.exp((z - sp) + cs[:, :kb]), 0.0)
            c_new = cs[:, kb:]
        else:
            c_old = c_scr[rows]
            a = jnp.exp((z - sp) + cs[:, :kb] + c_old)
            c_new = c_old + cs[:, kb:]
        c_scr[rows] = c_new
        c_max.append(jnp.max(c_new))
        ab = a.astype(BF16)
        yield
        for n, p in enumerate(pairs):
            rs = slice(p * 2 * qb, (p + 1) * 2 * qb)
            upd = _dg(ab[n * 2 * qb:(n + 1) * 2 * qb], load_t(vr, 2 * p + 1, p, keys), NT)
            acc_scr[rs] = upd if first else acc_scr[rs] + upd
        yield

    def block(kr, vr, keys, first, others=()):
        c_max = []
        lead = half_block(tuple(range(n_pairs // 2)), kr, vr, keys, first, c_max)
        lag = half_block(tuple(range(n_pairs // 2, n_pairs)), kr, vr, keys, first, c_max)
        next(lead)
        _round_robin(lead, lag, *others)
        return jnp.maximum(*c_max) > SB_SKIP_LOG

    own_keys = pl.ds(pl.multiple_of(qi * qb, qb), qb) if qb == kb else pl.ds(0, qb)
    go = block(k_ref, v_ref, own_keys, True, (mlstm,))

    def sweep(kr, vr, n_blocks, go):
        def cond(carry):
            j, more = carry
            return jnp.logical_and(j < n_blocks, more)

        def body(carry):
            j, _ = carry
            keys = pl.ds(pl.multiple_of((n_blocks - 1 - j) * kb, kb), kb)
            return j + 1, block(kr, vr, keys, False)

        return lax.while_loop(cond, body, (jnp.int32(0), go))[1]

    if qb == kb:
        go = sweep(k_ref, v_ref, qi, go)
    if n_past_blocks:
        go = sweep(kp_ref, vp_ref, n_past_blocks, go)

    for p in range(n_pairs):
        r0 = p * 2 * qb
        o_ref[0, :, p * LANES:(p + 1) * LANES] = jnp.where(lo_lanes, acc_scr[r0:r0 + qb], acc_scr[r0 + qb:r0 + 2 * qb])


def _sb_mlstm_call(q, kt, vt, layer, kt_past, vt_past, qk, v, ifg, o, buf0, c0, n0, m0, lw):
    bsz, t, _ = q.shape
    qb = min(LANES, t)
    assert t % qb == 0 and (qb == LANES or t == qb)
    n_past = 0 if kt_past is None else kt_past.shape[-1]
    assert n_past % LANES == 0
    blk = lambda w: pl.BlockSpec((1, qb, w), lambda b, i: (b, i, 0))
    full = lambda n: pl.BlockSpec((1, 1, C_WIDTH, n), lambda b, i: (layer, b, 0, 0))
    fix = lambda *s: pl.BlockSpec((1,) + s, lambda b, i: (b,) + (0,) * len(s))
    in_specs = [blk(C_WIDTH), full(t), full(t)]
    args = [q, kt, vt]
    if n_past:
        in_specs += [full(n_past), full(n_past)]
        args += [kt_past, vt_past]
    in_specs += [blk(2 * B_WIDTH), blk(B_WIDTH), blk(LANES), blk(B_WIDTH),
                 fix(SUBLANES, 2 * B_WIDTH), fix(B_HEADS, B_HEAD_DIM, B_HEAD_DIM), fix(B_HEADS, 1, B_HEAD_DIM),
                 fix(B_HEADS, 1, LANES),
                 _const_spec((B_CONV, 2 * B_WIDTH)), _const_spec((1, 2 * B_WIDTH)), _const_spec((1, B_WIDTH))]
    args += [qk, v, ifg, o, buf0, c0, n0, m0, lw["conv_b_w"], lw["conv_b_b"], lw["hn_b_g"]]
    assert len(in_specs) - (5 if n_past else 3) == _N_MLSTM_IN
    return pl.pallas_call(
        functools.partial(_sb_mlstm_kernel, n_past_blocks=n_past // LANES),
        out_shape=(jax.ShapeDtypeStruct((bsz, t, C_WIDTH), F32),
                   jax.ShapeDtypeStruct((bsz, t, B_WIDTH), F32),
                   jax.ShapeDtypeStruct((bsz, SUBLANES, 2 * B_WIDTH), F32),
                   jax.ShapeDtypeStruct((bsz, B_HEADS, B_HEAD_DIM, B_HEAD_DIM), F32),
                   jax.ShapeDtypeStruct((bsz, B_HEADS, 1, B_HEAD_DIM), F32),
                   jax.ShapeDtypeStruct((bsz, B_HEADS, 1, LANES), F32)),
        grid=(bsz, t // qb),
        in_specs=in_specs,
        out_specs=(blk(C_WIDTH), blk(B_WIDTH), fix(SUBLANES, 2 * B_WIDTH), fix(B_HEADS, B_HEAD_DIM, B_HEAD_DIM),
                   fix(B_HEADS, 1, B_HEAD_DIM), fix(B_HEADS, 1, LANES)),
        scratch_shapes=[pltpu.VMEM((C_HEADS // 2, 2 * qb, LANES), BF16),
                        pltpu.VMEM((C_HEADS * qb, LANES), F32), pltpu.VMEM((C_HEADS * qb, LANES), F32),
                        pltpu.VMEM((C_HEADS, LANES, LANES), F32),
                        pltpu.VMEM((qb + SUBLANES, 2 * B_WIDTH), F32),
                        pltpu.VMEM((B_HEADS, B_HEAD_DIM, B_HEAD_DIM), F32),
                        pltpu.VMEM((B_HEADS, 1, B_HEAD_DIM), F32),
                        pltpu.VMEM((B_HEADS, 1, LANES), F32)],
        compiler_params=_params("parallel", "arbitrary"),
        name="sb_mlstm",
    )(*args)


def _merge_kernel(x_ref, ya_ref, yb_ref, yc_ref, p_ref, wz_ref, bz_ref, wg_ref, bg_ref, wbr_ref, wout_ref,
                  lng_ref, lnb_ref, wple_ref, wpg_ref, o_ref):
    x = x_ref[...]
    xb = x.astype(BF16)
    width = A_WIDTH
    mix_pre = jnp.zeros(x.shape, F32)
    for n, y_ref in enumerate((ya_ref, yb_ref, yc_ref)):
        z = _dg(xb, wz_ref[:, n * width:(n + 1) * width]) + bz_ref[:, n * width:(n + 1) * width]
        ys = y_ref[...] * _silu(z)
        br = _mm(ys, wbr_ref[n])
        gate = _dg(xb, wg_ref[:, n * D_MODEL:(n + 1) * D_MODEL]) + bg_ref[:, n * D_MODEL:(n + 1) * D_MODEL]
        mix_pre = mix_pre + _sigmoid(gate) * br
    mix = _mm(mix_pre, wout_ref[...])
    x1 = _norm_rows(DEEPNORM_ALPHA * x + mix, LN_EPS) * lng_ref[...] + lnb_ref[...]
    o_ref[...] = x1 + _mm(p_ref[0], wple_ref[...]) * _sigmoid(_mm(x1, wpg_ref[...]))


def _merge_call(x, ya, yb, yc, p_all, layer, lw):
    n, d = x.shape
    tm = _tile(n, 512)
    rows = lambda w: pl.BlockSpec((tm, w), lambda i: (i, 0))
    return pl.pallas_call(
        _merge_kernel,
        out_shape=jax.ShapeDtypeStruct((n, d), F32),
        grid=(n // tm,),
        in_specs=[rows(d), rows(A_WIDTH), rows(B_WIDTH), rows(C_WIDTH),
                  pl.BlockSpec((1, tm, D_PLE), lambda i: (layer, i, 0)),
                  _const_spec((d, N_BRANCH * A_WIDTH)), _const_spec((1, N_BRANCH * A_WIDTH)),
                  _const_spec((d, N_BRANCH * d)), _const_spec((1, N_BRANCH * d)),
                  _const_spec((N_BRANCH, A_WIDTH, d)), _const_spec((d, d)),
                  _const_spec((1, d)), _const_spec((1, d)), _const_spec((D_PLE, d)), _const_spec((d, d))],
        out_specs=rows(d),
        compiler_params=_params("parallel"),
        name="merge",
    )(x, ya, yb, yc, p_all, lw["w_z"], lw["b_z"], lw["w_gates"], lw["b_gates"], lw["w_branch"], lw["w_out"],
      lw["ln_g"], lw["ln_b"], lw["w_ple"], lw["w_ple_gate"])


def _cols(w, *ranges):
    return jnp.concatenate([w[..., a:b] for a, b in ranges], axis=-1)


def _layer_params(i, w_in, b_in, prm):
    w = w_in[i]
    b = b_in[i]
    mix_ranges = ((_OFF_A_COLS, _OFF_A_Z), (_OFF_B_QK, _OFF_B_V), (_OFF_B_V, _OFF_B_I), (_OFF_B_O, _OFF_B_Z),
                  (_OFF_C_Q, _OFF_C_K), (_OFF_B_I, _OFF_B_O))
    pad = LANES - 2 * B_HEADS
    w_mix = jnp.pad(_cols(w, *mix_ranges), ((0, 0), (0, pad)))
    b_mix = jnp.pad(_cols(b, *mix_ranges), ((0, pad),))
    z_ranges = ((_OFF_A_Z, _OFF_B_QK), (_OFF_B_Z, _OFF_C_Q), (_OFF_C_Z, _OFF_GATES))
    zeros_lora = jnp.zeros((A_LORA, A_WIDTH), F32)
    hid = jnp.arange(A_WIDTH) // A_HEAD_DIM
    row = lambda name: prm[name][i].reshape(1, -1)
    return {
        "w_mix": w_mix.astype(BF16), "b_mix": b_mix.reshape(1, -1),
        "w_kvt": w[:, _OFF_C_K:_OFF_C_Z].T.astype(BF16), "b_kvt": b[_OFF_C_K:_OFF_C_Z].reshape(-1, 1),
        "w_z": _cols(w, *z_ranges).astype(BF16), "b_z": _cols(b, *z_ranges).reshape(1, -1),
        "w_gates": w[:, _OFF_GATES:].astype(BF16), "b_gates": b[_OFF_GATES:].reshape(1, -1),
        "mu_a": row("mu_a"), "w0_a": row("w0_a"), "a0_a": row("a0_a"), "k_k": row("k_k"), "k_a": row("k_a"),
        "wup_pad": jnp.concatenate([prm["w_decay_up"][i], zeros_lora], axis=0).astype(BF16),
        "aup_pad": jnp.concatenate([zeros_lora, prm["w_iclr_up"][i]], axis=0).astype(BF16),
        "head_sum": (hid[:, None] == hid[None, :]).astype(BF16),
        "r_k": row("r_k"), "gn_a_g": row("gn_a_g"), "gn_a_b": row("gn_a_b"),
        "conv_b_w": prm["conv_b_w"][i], "conv_b_b": row("conv_b_b"), "hn_b_g": row("hn_b_g"),
        "w_branch": prm["w_branch"][i].astype(BF16), "w_out": prm["w_out"][i].astype(BF16),
        "ln_g": row("ln_g"), "ln_b": row("ln_b"),
        "w_ple": prm["w_ple"][i].astype(BF16), "w_ple_gate": prm["w_ple_gate"][i].astype(BF16),
    }


def _run_group(x, p, init, ln_in_g, ln_in_b, layers):
    bsz, t, d = x.shape
    n = bsz * t
    shift0, wkv0, conv0, c0, n0, m0, k_past, v_past = init
    xf = _layer_norm_call(x.reshape(n, d), ln_in_g, ln_in_b)
    new = [[] for _ in range(6)]
    kt = vt = None
    channel_major = lambda a: jnp.transpose(a, (0, 1, 3, 4, 2)).reshape(a.shape[0], a.shape[1], C_WIDTH, a.shape[2])
    if k_past is not None:
        k_past, v_past = channel_major(k_past), channel_major(v_past)
    for i, lw in enumerate(layers):
        a_cols, b_qk, b_v, b_o, c_q, b_if, kt, vt = _inproj_call(xf, lw, i, bsz, t, kt, vt)
        seq = lambda a: a.reshape(bsz, t, a.shape[-1])
        ya, last, s_pairs = _rwkv_call(seq(a_cols), shift0[i], _wkv_to_pairs(wkv0[i]), lw)
        buf0 = jnp.pad(conv0[i], ((0, 0), (SUBLANES - (B_CONV - 1), 0), (0, 0)))
        m0b = jnp.broadcast_to(m0[i][:, :, None, None], (bsz, B_HEADS, 1, LANES))
        yc, yb, buf, c1, n1, m1 = _sb_mlstm_call(seq(c_q), kt, vt, i, k_past, v_past,
                                                 seq(b_qk), seq(b_v), seq(b_if), seq(b_o), buf0, c0[i],
                                                 n0[i][:, :, None, :], m0b, lw)
        xf = _merge_call(xf, ya.reshape(n, A_WIDTH), yb.reshape(n, B_WIDTH), yc.reshape(n, C_WIDTH),
                         p.reshape(DEPTH, n, D_PLE), i, lw)
        states = (last[:, 0], _wkv_from_pairs(s_pairs), buf[:, SUBLANES - (B_CONV - 1):], c1, n1[:, :, 0],
                  m1[:, :, 0, 0])
        for lst, s in zip(new, states):
            lst.append(s)
    token_major = lambda a: jnp.transpose(a.reshape(DEPTH, bsz, C_HEADS, C_HEAD_DIM, t), (0, 1, 4, 2, 3))
    return xf.reshape(bsz, t, d), [jnp.stack(lst) for lst in new] + [token_major(kt), token_major(vt)]


def kernel(x_prompt, x_sample, state_shift_a, state_wkv, state_conv_b, state_mlstm_c, state_mlstm_n, state_mlstm_m, cache_sb_k, cache_sb_v, p_prompt, p_sample, ln_in_g, ln_in_b, w_in, b_in, mu_a, w0_a, w_decay_up, a0_a, w_iclr_up, k_k, k_a, r_k, gn_a_g, gn_a_b, conv_b_w, conv_b_b, hn_b_g, w_branch, w_out, ln_g, ln_b, w_ple, w_ple_gate):
    prm = dict(mu_a=mu_a, w0_a=w0_a, w_decay_up=w_decay_up, a0_a=a0_a, w_iclr_up=w_iclr_up, k_k=k_k, k_a=k_a,
               r_k=r_k, gn_a_g=gn_a_g, gn_a_b=gn_a_b, conv_b_w=conv_b_w, conv_b_b=conv_b_b, hn_b_g=hn_b_g,
               w_branch=w_branch, w_out=w_out, ln_g=ln_g, ln_b=ln_b, w_ple=w_ple, w_ple_gate=w_ple_gate)
    layers = [_layer_params(i, w_in, b_in, prm) for i in range(DEPTH)]
    bp = x_prompt.shape[0]
    zeros = lambda *s: jnp.zeros((DEPTH, bp) + s, F32)
    init_prompt = (zeros(A_SHIFT), zeros(A_HEADS, A_HEAD_DIM, A_HEAD_DIM), zeros(B_CONV - 1, 2 * B_WIDTH),
                   zeros(B_HEADS, B_HEAD_DIM, B_HEAD_DIM), zeros(B_HEADS, B_HEAD_DIM), zeros(B_HEADS), None, None)
    init_sample = (state_shift_a, state_wkv, state_conv_b, state_mlstm_c, state_mlstm_n, state_mlstm_m,
                   cache_sb_k, cache_sb_v)
    y_prompt, sp = _run_group(x_prompt, p_prompt, init_prompt, ln_in_g, ln_in_b, layers)
    y_sample, ss = _run_group(x_sample, p_sample, init_sample, ln_in_g, ln_in_b, layers)
    return (y_prompt, y_sample, sp[0], ss[0], sp[1], ss[1], sp[2], ss[2], sp[3], ss[3],
            sp[4], ss[4], sp[5], ss[5], sp[6], ss[6], sp[7], ss[7])
```

```python
import functools

import jax
import jax.numpy as jnp
from jax import lax
from jax.experimental import pallas as pl
from jax.experimental.pallas import tpu as pltpu

F32 = jnp.float32
BF16 = jnp.bfloat16

D_MODEL = 1024
DEPTH = 4
CHUNK = 64
D_PLE = 256
A_HEADS = 8
A_HEAD_DIM = 64
A_WIDTH = A_HEADS * A_HEAD_DIM
A_LORA = 64
A_SHIFT = 3 * A_WIDTH + 2 * A_LORA
B_HEADS = 4
B_HEAD_DIM = 128
B_WIDTH = B_HEADS * B_HEAD_DIM
B_CONV = 4
C_HEADS = 8
C_HEAD_DIM = 64
C_WIDTH = C_HEADS * C_HEAD_DIM
N_BRANCH = 3
DEEPNORM_ALPHA = (2 * DEPTH) ** 0.25
LN_EPS = 1e-5
GN_EPS_A = 64e-5
HN_EPS = 1e-6

LANES = 128
SUBLANES = 8
VMEM_LIMIT_BYTES = 56 * 1024 * 1024

_OFF_A_COLS = 0
_OFF_A_Z = _OFF_A_COLS + A_SHIFT
_OFF_B_QK = _OFF_A_Z + A_WIDTH
_OFF_B_V = _OFF_B_QK + 2 * B_WIDTH
_OFF_B_I = _OFF_B_V + B_WIDTH
_OFF_B_F = _OFF_B_I + B_HEADS
_OFF_B_O = _OFF_B_F + B_HEADS
_OFF_B_Z = _OFF_B_O + B_WIDTH
_OFF_C_Q = _OFF_B_Z + B_WIDTH
_OFF_C_K = _OFF_C_Q + C_WIDTH
_OFF_C_V = _OFF_C_K + C_WIDTH
_OFF_C_Z = _OFF_C_V + C_WIDTH
_OFF_GATES = _OFF_C_Z + C_WIDTH
_N_IN = _OFF_GATES + N_BRANCH * D_MODEL

_MIX_WIDTHS = (A_SHIFT, 2 * B_WIDTH, B_WIDTH, B_WIDTH, C_WIDTH, LANES)

NN = ((1,), (0,))
NT = ((1,), (1,))
TN = ((0,), (0,))


def _dg(a, b, dims=NN):
    return lax.dot_general(a, b, (dims, ((), ())), preferred_element_type=F32)


def _mm(a, b, dims=NN):
    return _dg(a.astype(BF16), b.astype(BF16), dims)


def _split3(x):
    h1 = x.astype(BF16)
    r1 = x - h1.astype(F32)
    h2 = r1.astype(BF16)
    h3 = (r1 - h2.astype(F32)).astype(BF16)
    return h1, h2, h3


def _mm_exact_rhs(lhs_bf16, x):
    h1, h2, h3 = _split3(x)
    return _dg(lhs_bf16, h1) + (_dg(lhs_bf16, h2) + _dg(lhs_bf16, h3))


def _mm_exact_lhs(x, rhs_bf16):
    h1, h2, h3 = _split3(x)
    return _dg(h1, rhs_bf16) + (_dg(h2, rhs_bf16) + _dg(h3, rhs_bf16))


def _softplus(x):
    return jnp.maximum(x, 0.0) + jnp.log(1.0 + jnp.exp(-jnp.abs(x)))


def _sigmoid(x):
    return jax.nn.sigmoid(x)


def _silu(x):
    return x * jax.nn.sigmoid(x)


def _norm_rows(x, eps):
    mu = jnp.mean(x, axis=-1, keepdims=True)
    xc = x - mu
    var = jnp.mean(xc * xc, axis=-1, keepdims=True)
    return xc * lax.rsqrt(var + eps)


def _tile(n, pref):
    if n <= pref:
        return n
    t = pref - pref % SUBLANES
    while n % t:
        t -= SUBLANES
    return t


def _params(*sem):
    return pltpu.CompilerParams(dimension_semantics=sem, vmem_limit_bytes=VMEM_LIMIT_BYTES)


def _const_spec(shape):
    nd = len(shape)
    return pl.BlockSpec(shape, lambda *_: (0,) * nd, pipeline_mode=pl.Buffered(1))


def _inproj_kernel(x_ref, w_ref, b_ref, wkv_ref, bkv_ref, *refs, pre_norm):
    n_out = len(_MIX_WIDTHS) + 2
    out_refs = refs[-n_out:]
    x = x_ref[...]
    if pre_norm:
        x = _norm_rows(x, LN_EPS) * refs[0][...] + refs[1][...]
    xb = x.astype(BF16)
    off = 0
    for o_ref, width in zip(out_refs, _MIX_WIDTHS):
        o_ref[...] = _dg(xb, w_ref[:, off:off + width]) + b_ref[:, off:off + width]
        off += width
    kvt = _dg(wkv_ref[...], xb, NT) + bkv_ref[...]
    out_refs[-2][0, 0] = kvt[:C_WIDTH]
    out_refs[-1][0, 0] = kvt[C_WIDTH:]


def _inproj_call(x, lw, layer, bsz, t, kt_stack=None, vt_stack=None, pre_norm=None):
    n, d = x.shape
    tm = _tile(t, 512)
    per_b = t // tm
    wtot = sum(_MIX_WIDTHS)
    stack = jax.ShapeDtypeStruct((DEPTH, bsz, C_WIDTH, t), F32)
    stack_spec = pl.BlockSpec((1, 1, C_WIDTH, tm), lambda i: (layer, i // per_b, 0, i % per_b))
    in_specs = [pl.BlockSpec((tm, d), lambda i: (i, 0)), _const_spec((d, wtot)), _const_spec((1, wtot)),
                _const_spec((2 * C_WIDTH, d)), _const_spec((2 * C_WIDTH, 1))]
    args = [x, lw["w_mix"], lw["b_mix"], lw["w_kvt"], lw["b_kvt"]]
    if pre_norm is not None:
        in_specs += [_const_spec((1, d))] * 2
        args += [a.reshape(1, d) for a in pre_norm]
    aliases = {}
    if kt_stack is not None:
        n_mix = len(_MIX_WIDTHS)
        aliases = {len(args): n_mix, len(args) + 1: n_mix + 1}
        in_specs += [pl.BlockSpec(memory_space=pl.ANY)] * 2
        args += [kt_stack, vt_stack]
    return pl.pallas_call(
        functools.partial(_inproj_kernel, pre_norm=pre_norm is not None),
        out_shape=tuple(jax.ShapeDtypeStruct((n, wd), F32) for wd in _MIX_WIDTHS) + (stack, stack),
        grid=(n // tm,),
        in_specs=in_specs,
        out_specs=tuple(pl.BlockSpec((tm, wd), lambda i: (i, 0)) for wd in _MIX_WIDTHS) + (stack_spec, stack_spec),
        input_output_aliases=aliases,
        compiler_params=_params("parallel"),
        name="inproj",
    )(*args)


RWKV_RING = 3


def _round_robin(*steps):
    steps = list(steps)
    while steps:
        for s in list(steps):
            try:
                next(s)
            except StopIteration:
                steps.remove(s)


def _rwkv_pipe_kernel(cols_ref, last_ref, s0_ref, mu_ref, w0_ref, wup_ref, a0_ref, aup_ref, kk_ref, ka_ref, hsum_ref,
                      rk_ref, gng_ref, gnb_ref, y_o, last_o, s_o,
                      prev_scr, s_scr, at_r, rt_r, bt_r, kt_r, btw_r, ktw_r, v_r, bon_r, w_r, u0_r, x1_r, mrb_r, wl_r):
    t = pl.program_id(1)
    L = CHUNK
    tt = cols_ref.shape[1]
    gr = min(tt, 2 * L)
    cpg = gr // L
    ng = tt // gr
    n_pairs = A_HEADS // 2
    rings = (at_r, rt_r, bt_r, kt_r, btw_r, ktw_r, v_r, bon_r, w_r, u0_r, x1_r, mrb_r, wl_r)

    @pl.when(t == 0)
    def _():
        prev_scr[...] = last_ref[0]
        s_scr[...] = s0_ref[0]
        if ng > 1:
            for ring in rings:
                ring[...] = jnp.zeros(ring.shape, F32)

    ri = lax.broadcasted_iota(jnp.int32, (L, L), 0)
    ci = lax.broadcasted_iota(jnp.int32, (L, L), 1)
    strict = ri > ci
    incl = ri >= ci
    tri = jnp.where(incl, 1.0, 0.0).astype(BF16)
    eye = jnp.where(ri == ci, 1.0, 0.0)
    lane = lax.broadcasted_iota(jnp.int32, (1, LANES), 1)
    head_masks = (jnp.where(lane < A_HEAD_DIM, 1.0, 0.0), jnp.where(lane >= A_HEAD_DIM, 1.0, 0.0))
    inv_d = 1.0 / A_HEAD_DIM
    pair = lambda p: slice(p * LANES, (p + 1) * LANES)
    chunk = lambda c: slice(c * L, (c + 1) * L)
    heads = [(p, hm) for p in range(n_pairs) for hm in head_masks]

    def prologue(g):
        gp = jnp.minimum(g, ng - 1)
        slot = lax.rem(gp, RWKV_RING)
        r0 = pl.multiple_of(gp * gr, gr)
        u = cols_ref[0, pl.ds(r0, gr), :]
        above = cols_ref[0, pl.ds(pl.multiple_of(jnp.maximum(r0 - SUBLANES, 0), SUBLANES), SUBLANES), :]
        before = jnp.where(gp == 0, prev_scr[...], above[SUBLANES - 1:SUBLANES, :])
        row = lax.broadcasted_iota(jnp.int32, u.shape, 0)
        prev = jnp.where(row == 0, before, pltpu.roll(u, 1, 0))
        xs = u + (prev - u) * mu_ref[...]
        yield
        r = xs[:, 0:A_WIDTH]
        k = xs[:, A_WIDTH:2 * A_WIDTH]
        v = xs[:, 2 * A_WIDTH:3 * A_WIDTH]
        lora = xs[:, 3 * A_WIDTH:]
        w_log = -_softplus(-(w0_ref[...] + _mm(jnp.tanh(lora), wup_ref[...]))) - 0.5
        a = _sigmoid(a0_ref[...] + _mm(lora, aup_ref[...]))
        yield
        kk = k * kk_ref[...]
        ss = _mm_exact_lhs(kk * kk, hsum_ref[...])
        kk = kk / jnp.maximum(jnp.sqrt(ss), 1e-12)
        yield
        kh = k * (1.0 + (a - 1.0) * ka_ref[...])
        lw = -jnp.exp(w_log)
        cin = jnp.concatenate([_mm_exact_rhs(tri, lw[chunk(c), :]) for c in range(cpg)], axis=0)
        yield
        ein = jnp.exp(cin)
        einv = jnp.exp(-cin)
        bt = kk * a * einv
        kt = kh * einv
        at_r[slot] = -kk * jnp.exp(cin - lw)
        rt_r[slot] = r * ein
        yield
        bt_r[slot] = bt
        kt_r[slot] = kt
        v_r[slot] = v
        bon_r[slot] = r * kh * rk_ref[...]
        yield
        for c in range(cpg):
            wl = ein[(c + 1) * L - 1:(c + 1) * L, :]
            wl_r[slot, c] = wl
            btw_r[slot, chunk(c), :] = bt[chunk(c), :] * wl
            ktw_r[slot, chunk(c), :] = kt[chunk(c), :] * wl
        yield

    def state_free(g):
        slot = lax.rem(jnp.clip(g - 1, 0, ng - 1), RWKV_RING)
        items = [(c, p, hm) for c in range(cpg) for p, hm in heads]
        load = lambda ring: [ring[slot, chunk(c), pair(p)] * hm for c, p, hm in items]
        at_h, rt_h, v_h = load(at_r), load(rt_r), load(v_r)
        lhs = [jnp.concatenate([x, y], axis=0).astype(BF16) for x, y in zip(at_h, rt_h)]
        yield
        sc_b = [_dg(x, bt_r[slot, chunk(c), pair(p)].astype(BF16), NT) for x, (c, p, _) in zip(lhs, items)]
        sc_k = [_dg(x, kt_r[slot, chunk(c), pair(p)].astype(BF16), NT) for x, (c, p, _) in zip(lhs, items)]
        yield
        a_ab = [jnp.where(strict, s[:L], 0.0) for s in sc_b]
        a_ak = [jnp.where(strict, s[:L], 0.0) for s in sc_k]
        m_rb = [jnp.where(incl, s[L:], 0.0) for s in sc_b]
        m_rk = [jnp.where(incl, s[L:], 0.0) for s in sc_k]
        tm = [eye + x for x in a_ab]
        ap = a_ab
        for _ in range(5):
            ap = [_mm(x, x) for x in ap]
            yield
            tm = [x + _mm(x, y) for x, y in zip(tm, ap)]
            yield
        x1 = [_mm(jnp.concatenate([x, y], axis=0), vh) for x, y, vh in zip(a_ak, m_rk, v_h)]
        yield
        wu = [_mm(x, jnp.concatenate([y, z[:L]], axis=1)) for x, y, z in zip(tm, at_h, x1)]
        yield
        for n in range(0, len(items), 2):
            c, p, _ = items[n]
            w_r[slot, chunk(c), pair(p)] = wu[n][:, :LANES] + wu[n + 1][:, :LANES]
            u0_r[slot, chunk(c), pair(p)] = wu[n][:, LANES:] + wu[n + 1][:, LANES:]
            x1_r[slot, chunk(c), pair(p)] = x1[n][L:] + x1[n + 1][L:]
            mrb_r[slot, c, 2 * p] = m_rb[n]
            mrb_r[slot, c, 2 * p + 1] = m_rb[n + 1]
        yield

    def state_step(g):
        valid = g >= 2
        g2 = jnp.clip(g - 2, 0, ng - 1)
        slot = lax.rem(g2, RWKV_RING)
        hm0, hm1 = head_masks
        y_pairs = [[] for _ in range(n_pairs)]
        for c in range(cpg):
            s_pairs = [s_scr[p] for p in range(n_pairs)]
            s_pairs_b = [s.astype(BF16) for s in s_pairs]
            w_h = [w_r[slot, chunk(c), pair(p)] * hm for p, hm in heads]
            rt_h = [rt_r[slot, chunk(c), pair(p)] * hm for p, hm in heads]
            st = [_dg(jnp.concatenate([x, y], axis=0).astype(BF16), s_pairs_b[p], NT)
                  for x, y, (p, _) in zip(w_h, rt_h, heads)]
            yield
            u_h = [u0_r[slot, chunk(c), pair(p)] * hm + s[:L] for s, (p, hm) in zip(st, heads)]
            y_h = [s[L:] + _mm(mrb_r[slot, c, n], x) for n, (s, x) in enumerate(zip(st, u_h))]
            upd = []
            for p in range(n_pairs):
                vp, bp, kp = v_r[slot, chunk(c), pair(p)], btw_r[slot, chunk(c), pair(p)], ktw_r[slot, chunk(c), pair(p)]
                upd.append(_mm(jnp.concatenate([u_h[2 * p], vp * hm0, u_h[2 * p + 1], vp * hm1], axis=0),
                               jnp.concatenate([bp * hm0, kp * hm0, bp * hm1, kp * hm1], axis=0), TN))
            yield
            wl = wl_r[slot, c]
            for p in range(n_pairs):
                s_scr[p] = jnp.where(valid, s_pairs[p] * wl[:, pair(p)] + upd[p], s_pairs[p])
                y_pairs[p].append(y_h[2 * p] + y_h[2 * p + 1] + x1_r[slot, chunk(c), pair(p)])
            yield
        rows = pl.ds(pl.multiple_of(g2 * gr, gr), gr)
        for p in range(n_pairs):
            y_pair = jnp.concatenate(y_pairs[p], axis=0)
            normed = jnp.zeros(y_pair.shape, F32)
            bonus = jnp.zeros(y_pair.shape, F32)
            for hm in head_masks:
                mu = jnp.sum(y_pair * hm, axis=-1, keepdims=True) * inv_d
                yc = (y_pair - mu) * hm
                var = jnp.sum(yc * yc, axis=-1, keepdims=True) * inv_d
                normed = normed + yc * lax.rsqrt(var + GN_EPS_A)
                bonus = bonus + (jnp.sum(bon_r[slot, :, pair(p)] * hm, axis=-1, keepdims=True)
                                 * (v_r[slot, :, pair(p)] * hm))
            y_o[0, rows, pair(p)] = (normed * gng_ref[:, pair(p)] + gnb_ref[:, pair(p)]) + bonus
            yield

    def trip(g, carry):
        _round_robin(prologue(g), state_free(g), state_step(g))
        return carry

    if ng == 1:
        for g, step in enumerate((prologue, state_free, state_step)):
            _round_robin(step(g))
    else:
        lax.fori_loop(0, ng + 2, trip, 0)
    prev_scr[...] = cols_ref[0, tt - 1:tt, :]
    last_o[0] = cols_ref[0, tt - 1:tt, :]
    s_o[0] = s_scr[...]


def _rwkv_call(cols, last, s0_pairs, lw):
    bsz, t, _ = cols.shape
    tt = _tile(t, 2048)
    gr = min(tt, 2 * CHUNK)
    blk = lambda w: pl.BlockSpec((1, tt, w), lambda b, i: (b, i, 0))
    row = pl.BlockSpec((1, 1, A_SHIFT), lambda b, i: (b, 0, 0))
    st = pl.BlockSpec((1, A_HEADS // 2, LANES, LANES), lambda b, i: (b, 0, 0, 0))
    ring = pltpu.VMEM((RWKV_RING, gr, A_WIDTH), F32)
    return pl.pallas_call(
        _rwkv_pipe_kernel,
        out_shape=(jax.ShapeDtypeStruct((bsz, t, A_WIDTH), F32),
                   jax.ShapeDtypeStruct((bsz, 1, A_SHIFT), F32),
                   jax.ShapeDtypeStruct((bsz, A_HEADS // 2, LANES, LANES), F32)),
        grid=(bsz, t // tt),
        in_specs=[blk(A_SHIFT), row, st, _const_spec((1, A_SHIFT)), _const_spec((1, A_WIDTH)),
                  _const_spec((2 * A_LORA, A_WIDTH)), _const_spec((1, A_WIDTH)), _const_spec((2 * A_LORA, A_WIDTH)),
                  _const_spec((1, A_WIDTH)), _const_spec((1, A_WIDTH)), _const_spec((A_WIDTH, A_WIDTH)),
                  _const_spec((1, A_WIDTH)), _const_spec((1, A_WIDTH)), _const_spec((1, A_WIDTH))],
        out_specs=(blk(A_WIDTH), row, st),
        scratch_shapes=[pltpu.VMEM((1, A_SHIFT), F32), pltpu.VMEM((A_HEADS // 2, LANES, LANES), F32)]
        + [ring] * 11
        + [pltpu.VMEM((RWKV_RING, gr // CHUNK, A_HEADS, CHUNK, CHUNK), F32),
           pltpu.VMEM((RWKV_RING, gr // CHUNK, 1, A_WIDTH), F32)],
        compiler_params=_params("parallel", "arbitrary"),
        name="rwkv",
    )(cols, last.reshape(bsz, 1, A_SHIFT), s0_pairs, lw["mu_a"], lw["w0_a"], lw["wup_pad"], lw["a0_a"],
      lw["aup_pad"], lw["k_k"], lw["k_a"], lw["head_sum"], lw["r_k"], lw["gn_a_g"], lw["gn_a_b"])


def _wkv_to_pairs(s):
    bsz = s.shape[0]
    s = s.reshape(bsz, A_HEADS // 2, 2, A_HEAD_DIM, A_HEAD_DIM)
    z = jnp.zeros_like(s[:, :, 0])
    top = jnp.concatenate([s[:, :, 0], z], axis=-1)
    bot = jnp.concatenate([z, s[:, :, 1]], axis=-1)
    return jnp.concatenate([top, bot], axis=-2)


def _wkv_from_pairs(sp):
    bsz = sp.shape[0]
    d = A_HEAD_DIM
    s = jnp.stack([sp[:, :, :d, :d], sp[:, :, d:, d:]], axis=2)
    return s.reshape(bsz, A_HEADS, d, d)


def _mlstm_steps(t, qk_ref, v_ref, if_ref, o_ref, buf0_ref, c0_ref, n0_ref, m0_ref, cw_ref, cb_ref, hng_ref,
                 y_o, buf_o, c_o, n_o, m_o, ext_scr, c_scr, n_scr, m_scr):
    tt = qk_ref.shape[1]
    L = min(tt, LANES)
    assert tt == L
    pad = SUBLANES

    @pl.when(t == 0)
    def _():
        ext_scr[0:pad, :] = buf0_ref[0]
        c_scr[...] = c0_ref[0]
        n_scr[...] = n0_ref[0]
        m_scr[...] = m0_ref[0]

    ext_scr[pad:pad + tt, :] = qk_ref[0]
    ri = lax.broadcasted_iota(jnp.int32, (L, L), 0)
    ci = lax.broadcasted_iota(jnp.int32, (L, L), 1)
    incl = ri >= ci
    eye = ri == ci
    tri = jnp.where(incl, 1.0, 0.0).astype(BF16)
    k_scale = B_HEAD_DIM ** -0.5

    hs = range(B_HEADS)
    sls = [slice(h * B_HEAD_DIM, (h + 1) * B_HEAD_DIM) for h in hs]

    x = ext_scr[...]
    conv = cb_ref[...] + x[pad:, :] * cw_ref[B_CONV - 1:B_CONV, :]
    for tap in range(B_CONV - 1):
        conv = conv + pltpu.roll(x, B_CONV - 1 - tap, 0)[pad:, :] * cw_ref[tap:tap + 1, :]
    yield
    qk = _silu(conv)
    yield
    ifg = if_ref[0]
    bcum = _mm_exact_rhs(tri, -_softplus(-ifg))
    yield
    q_h = [qk[:, sl] for sl in sls]
    k_h = [qk[:, B_WIDTH + h * B_HEAD_DIM:B_WIDTH + (h + 1) * B_HEAD_DIM] * k_scale for h in hs]
    v_h = [v_ref[0, :, sl] for sl in sls]
    icol = [ifg[:, h:h + 1] for h in hs]
    bcol = [bcum[:, B_HEADS + h:B_HEADS + h + 1] for h in hs]
    qk_t = [_mm(q, k, NT) for q, k in zip(q_h, k_h)]
    yield
    dcol = [i_ - b for i_, b in zip(icol, bcol)]
    drow = [jnp.sum(jnp.where(eye, d, 0.0), axis=0, keepdims=True) for d in dcol]
    dmat = [jnp.where(incl, b + d, -jnp.inf) for b, d in zip(bcol, drow)]
    yield
    dmax = [jnp.max(d, axis=-1, keepdims=True) for d in dmat]
    yield
    c_h = [c_scr[h] for h in hs]
    n_h = [n_scr[h] for h in hs]
    m_h = [m_scr[h][:, 0:1] for h in hs]
    qc = [_mm(q, c, NT) for q, c in zip(q_h, c_h)]
    yield
    inter = [b + m for b, m in zip(bcol, m_h)]
    m_t = [jnp.maximum(x_, d) for x_, d in zip(inter, dmax)]
    wmat = [jnp.exp(d - m) * s for d, m, s in zip(dmat, m_t, qk_t)]
    yield
    scale = [jnp.exp(x_ - m) for x_, m in zip(inter, m_t)]
    wv = [_mm(w, v) for w, v in zip(wmat, v_h)]
    yield
    m_new = [m[L - 1:L, :] for m in m_t]
    gcol = [jnp.exp(b[L - 1:L, :] + d - m) for b, d, m in zip(bcol, dcol, m_new)]
    c_sc = [jnp.exp(x_[L - 1:L, :] - m) for x_, m in zip(inter, m_new)]
    gvk = [_mm(g * v, k, TN) for g, v, k in zip(gcol, v_h, k_h)]
    yield
    for h in hs:
        num = wv[h] + scale[h] * qc[h]
        den = (jnp.sum(wmat[h], axis=-1, keepdims=True)
               + scale[h] * jnp.sum(q_h[h] * n_h[h], axis=-1, keepdims=True))
        hid = num / jnp.maximum(jnp.abs(den), jnp.exp(-m_t[h]))
        y_o[0, :, sls[h]] = _sigmoid(o_ref[0, :, sls[h]]) * (_norm_rows(hid, HN_EPS) * hng_ref[:, sls[h]])
        yield
    for h in hs:
        c_scr[h] = c_sc[h] * c_h[h] + gvk[h]
        n_scr[h] = c_sc[h] * n_h[h] + jnp.sum(gcol[h] * k_h[h], axis=0, keepdims=True)
        m_scr[h] = jnp.broadcast_to(m_new[h], (1, LANES))
    last = ext_scr[tt:tt + pad, :]
    ext_scr[0:pad, :] = last
    buf_o[0] = last
    c_o[0] = c_scr[...]
    n_o[0] = n_scr[...]
    m_o[0] = m_scr[...]


SB_SKIP_LOG = -105.0


_N_MLSTM_IN, _N_MLSTM_OUT, _N_SB_SCRATCH = 11, 5, 4


def _sb_mlstm_kernel(*refs, n_past_blocks):
    n_att = 5 if n_past_blocks else 3
    att_in, refs = refs[:n_att], refs[n_att:]
    ml_in, o_ref, refs = refs[:_N_MLSTM_IN], refs[_N_MLSTM_IN], refs[_N_MLSTM_IN + 1:]
    ml_out, refs = refs[:_N_MLSTM_OUT], refs[_N_MLSTM_OUT:]
    (q2_scr, c_scr, acc_scr, pad_scr), ml_scr = refs[:_N_SB_SCRATCH], refs[_N_SB_SCRATCH:]
    if n_past_blocks:
        q_ref, k_ref, v_ref, kp_ref, vp_ref = att_in
    else:
        q_ref, k_ref, v_ref = att_in
    mlstm = _mlstm_steps(pl.program_id(1), *ml_in, *ml_out, *ml_scr)
    qi = pl.program_id(1)
    qb = q_ref.shape[1]
    kb = LANES
    n_pairs = C_HEADS // 2
    n_rows = C_HEADS * qb

    rj = lax.broadcasted_iota(jnp.int32, (2 * kb, 2 * kb), 0) & (kb - 1)
    cj = lax.broadcasted_iota(jnp.int32, (2 * kb, 2 * kb), 1)
    cum2 = jnp.where((cj >= kb) | (rj > cj), 1.0, 0.0).astype(BF16)
    qpos = lax.broadcasted_iota(jnp.int32, (n_rows, kb), 0) & (qb - 1)
    kpos = lax.broadcasted_iota(jnp.int32, (n_rows, kb), 1)
    diag_mask = kpos < qpos
    lane = lax.broadcasted_iota(jnp.int32, (1, LANES), 1)
    lo_lanes = lane < C_HEAD_DIM
    q = q_ref[0] * (C_HEAD_DIM ** -0.5)
    for p in range(n_pairs):
        qs = q[:, p * LANES:(p + 1) * LANES]
        q2_scr[p] = jnp.concatenate([jnp.where(lo_lanes, qs, 0.0), jnp.where(lo_lanes, 0.0, qs)], axis=0).astype(BF16)

    def load_t(ref, slot, p, keys):
        x = ref[0, 0, p * LANES:(p + 1) * LANES, keys]
        if x.shape[1] == kb:
            return x.astype(BF16)
        pad_scr[slot] = jnp.zeros((LANES, kb), F32)
        pad_scr[slot, :, 0:x.shape[1]] = x
        return pad_scr[slot].astype(BF16)

    def half_block(pairs, kr, vr, keys, first, c_max):
        rows = slice(pairs[0] * 2 * qb, (pairs[-1] + 1) * 2 * qb)
        mask = diag_mask[:rows.stop - rows.start]
        z = jnp.concatenate([_dg(q2_scr[p], load_t(kr, 2 * p, p, keys)) for p in pairs], axis=0)
        yield
        sp = _softplus(z)
        lk = jnp.where(mask, -sp, 0.0) if first else -sp
        h1 = lk.astype(BF16)
        h2 = (lk - h1.astype(F32)).astype(BF16)
        yield
        cs = _dg(jnp.concatenate([h1, h2], axis=1), cum2)
        yield
        if first:
            a = jnp.where(mask, jnp.exp((z - sp) + cs[:, :kb]), 0.0)
            c_new = cs[:, kb:]
        else:
            c_old = c_scr[rows]
            a = jnp.exp((z - sp) + cs[:, :kb] + c_old)
            c_new = c_old + cs[:, kb:]
        c_scr[rows] = c_new
        c_max.append(jnp.max(c_new))
        ab = a.astype(BF16)
        yield
        for n, p in enumerate(pairs):
            rs = slice(p * 2 * qb, (p + 1) * 2 * qb)
            upd = _dg(ab[n * 2 * qb:(n + 1) * 2 * qb], load_t(vr, 2 * p + 1, p, keys), NT)
            acc_scr[rs] = upd if first else acc_scr[rs] + upd
        yield

    def block(kr, vr, keys, first, others=()):
        c_max = []
        lead = half_block(tuple(range(n_pairs // 2)), kr, vr, keys, first, c_max)
        lag = half_block(tuple(range(n_pairs // 2, n_pairs)), kr, vr, keys, first, c_max)
        next(lead)
        _round_robin(lead, lag, *others)
        return jnp.maximum(*c_max) > SB_SKIP_LOG

    own_keys = pl.ds(pl.multiple_of(qi * qb, qb), qb) if qb == kb else pl.ds(0, qb)
    go = block(k_ref, v_ref, own_keys, True, (mlstm,))

    def sweep(kr, vr, n_blocks, go):
        def cond(carry):
            j, more = carry
            return jnp.logical_and(j < n_blocks, more)

        def body(carry):
            j, _ = carry
            keys = pl.ds(pl.multiple_of((n_blocks - 1 - j) * kb, kb), kb)
            return j + 1, block(kr, vr, keys, False)

        return lax.while_loop(cond, body, (jnp.int32(0), go))[1]

    if qb == kb:
        go = sweep(k_ref, v_ref, qi, go)
    if n_past_blocks:
        go = sweep(kp_ref, vp_ref, n_past_blocks, go)

    for p in range(n_pairs):
        r0 = p * 2 * qb
        o_ref[0, :, p * LANES:(p + 1) * LANES] = jnp.where(lo_lanes, acc_scr[r0:r0 + qb], acc_scr[r0 + qb:r0 + 2 * qb])


def _sb_mlstm_call(q, kt, vt, layer, kt_past, vt_past, qk, v, ifg, o, buf0, c0, n0, m0, lw):
    bsz, t, _ = q.shape
    qb = min(LANES, t)
    assert t % qb == 0 and (qb == LANES or t == qb)
    n_past = 0 if kt_past is None else kt_past.shape[-1]
    assert n_past % LANES == 0
    blk = lambda w: pl.BlockSpec((1, qb, w), lambda b, i: (b, i, 0))
    full = lambda n: pl.BlockSpec((1, 1, C_WIDTH, n), lambda b, i: (layer, b, 0, 0))
    fix = lambda *s: pl.BlockSpec((1,) + s, lambda b, i: (b,) + (0,) * len(s))
    in_specs = [blk(C_WIDTH), full(t), full(t)]
    args = [q, kt, vt]
    if n_past:
        in_specs += [full(n_past), full(n_past)]
        args += [kt_past, vt_past]
    in_specs += [blk(2 * B_WIDTH), blk(B_WIDTH), blk(LANES), blk(B_WIDTH),
                 fix(SUBLANES, 2 * B_WIDTH), fix(B_HEADS, B_HEAD_DIM, B_HEAD_DIM), fix(B_HEADS, 1, B_HEAD_DIM),
                 fix(B_HEADS, 1, LANES),
                 _const_spec((B_CONV, 2 * B_WIDTH)), _const_spec((1, 2 * B_WIDTH)), _const_spec((1, B_WIDTH))]
    args += [qk, v, ifg, o, buf0, c0, n0, m0, lw["conv_b_w"], lw["conv_b_b"], lw["hn_b_g"]]
    assert len(in_specs) - (5 if n_past else 3) == _N_MLSTM_IN
    return pl.pallas_call(
        functools.partial(_sb_mlstm_kernel, n_past_blocks=n_past // LANES),
        out_shape=(jax.ShapeDtypeStruct((bsz, t, C_WIDTH), F32),
                   jax.ShapeDtypeStruct((bsz, t, B_WIDTH), F32),
                   jax.ShapeDtypeStruct((bsz, SUBLANES, 2 * B_WIDTH), F32),
                   jax.ShapeDtypeStruct((bsz, B_HEADS, B_HEAD_DIM, B_HEAD_DIM), F32),
                   jax.ShapeDtypeStruct((bsz, B_HEADS, 1, B_HEAD_DIM), F32),
                   jax.ShapeDtypeStruct((bsz, B_HEADS, 1, LANES), F32)),
        grid=(bsz, t // qb),
        in_specs=in_specs,
        out_specs=(blk(C_WIDTH), blk(B_WIDTH), fix(SUBLANES, 2 * B_WIDTH), fix(B_HEADS, B_HEAD_DIM, B_HEAD_DIM),
                   fix(B_HEADS, 1, B_HEAD_DIM), fix(B_HEADS, 1, LANES)),
        scratch_shapes=[pltpu.VMEM((C_HEADS // 2, 2 * qb, LANES), BF16),
                        pltpu.VMEM((C_HEADS * qb, LANES), F32), pltpu.VMEM((C_HEADS * qb, LANES), F32),
                        pltpu.VMEM((C_HEADS, LANES, LANES), F32),
                        pltpu.VMEM((qb + SUBLANES, 2 * B_WIDTH), F32),
                        pltpu.VMEM((B_HEADS, B_HEAD_DIM, B_HEAD_DIM), F32),
                        pltpu.VMEM((B_HEADS, 1, B_HEAD_DIM), F32),
                        pltpu.VMEM((B_HEADS, 1, LANES), F32)],
        compiler_params=_params("parallel", "arbitrary"),
        name="sb_mlstm",
    )(*args)


def _merge_kernel(x_ref, ya_ref, yb_ref, yc_ref, p_ref, wz_ref, bz_ref, wg_ref, bg_ref, wbr_ref, wout_ref,
                  lng_ref, lnb_ref, wple_ref, wpg_ref, *refs, pre_norm):
    o_ref = refs[-1]
    width = A_WIDTH
    half_rows = x_ref.shape[0] // 2

    def half(r):
        rows = slice(r * half_rows, (r + 1) * half_rows)
        x = x_ref[rows, :]
        if pre_norm:
            x = _norm_rows(x, LN_EPS) * refs[0][...] + refs[1][...]
        xb = x.astype(BF16)
        mix_pre = jnp.zeros(x.shape, F32)
        for n, y_ref in enumerate((ya_ref, yb_ref, yc_ref)):
            z = _dg(xb, wz_ref[:, n * width:(n + 1) * width]) + bz_ref[:, n * width:(n + 1) * width]
            yield
            ys = y_ref[rows, :] * _silu(z)
            br = _mm(ys, wbr_ref[n])
            yield
            gate = _dg(xb, wg_ref[:, n * D_MODEL:(n + 1) * D_MODEL]) + bg_ref[:, n * D_MODEL:(n + 1) * D_MODEL]
            yield
            mix_pre = mix_pre + _sigmoid(gate) * br
        mix = _mm(mix_pre, wout_ref[...])
        yield
        x1 = _norm_rows(DEEPNORM_ALPHA * x + mix, LN_EPS) * lng_ref[...] + lnb_ref[...]
        yield
        o_ref[rows, :] = x1 + _mm(p_ref[0, rows, :], wple_ref[...]) * _sigmoid(_mm(x1, wpg_ref[...]))
        yield

    lead, lag = half(0), half(1)
    next(lead)
    _round_robin(lead, lag)


def _merge_call(x, ya, yb, yc, p_all, layer, lw, pre_norm=None):
    n, d = x.shape
    tm = _tile(n, 512)
    rows = lambda w: pl.BlockSpec((tm, w), lambda i: (i, 0))
    in_specs = [rows(d), rows(A_WIDTH), rows(B_WIDTH), rows(C_WIDTH),
                pl.BlockSpec((1, tm, D_PLE), lambda i: (layer, i, 0)),
                _const_spec((d, N_BRANCH * A_WIDTH)), _const_spec((1, N_BRANCH * A_WIDTH)),
                _const_spec((d, N_BRANCH * d)), _const_spec((1, N_BRANCH * d)),
                _const_spec((N_BRANCH, A_WIDTH, d)), _const_spec((d, d)),
                _const_spec((1, d)), _const_spec((1, d)), _const_spec((D_PLE, d)), _const_spec((d, d))]
    args = [x, ya, yb, yc, p_all, lw["w_z"], lw["b_z"], lw["w_gates"], lw["b_gates"], lw["w_branch"], lw["w_out"],
            lw["ln_g"], lw["ln_b"], lw["w_ple"], lw["w_ple_gate"]]
    if pre_norm is not None:
        in_specs += [_const_spec((1, d))] * 2
        args += [a.reshape(1, d) for a in pre_norm]
    return pl.pallas_call(
        functools.partial(_merge_kernel, pre_norm=pre_norm is not None),
        out_shape=jax.ShapeDtypeStruct((n, d), F32),
        grid=(n // tm,),
        in_specs=in_specs,
        out_specs=rows(d),
        compiler_params=_params("parallel"),
        name="merge",
    )(*args)


def _cols(w, *ranges):
    return jnp.concatenate([w[..., a:b] for a, b in ranges], axis=-1)


def _layer_params(i, w_in, b_in, prm):
    w = w_in[i]
    b = b_in[i]
    mix_ranges = ((_OFF_A_COLS, _OFF_A_Z), (_OFF_B_QK, _OFF_B_V), (_OFF_B_V, _OFF_B_I), (_OFF_B_O, _OFF_B_Z),
                  (_OFF_C_Q, _OFF_C_K), (_OFF_B_I, _OFF_B_O))
    pad = LANES - 2 * B_HEADS
    w_mix = jnp.pad(_cols(w, *mix_ranges), ((0, 0), (0, pad)))
    b_mix = jnp.pad(_cols(b, *mix_ranges), ((0, pad),))
    z_ranges = ((_OFF_A_Z, _OFF_B_QK), (_OFF_B_Z, _OFF_C_Q), (_OFF_C_Z, _OFF_GATES))
    zeros_lora = jnp.zeros((A_LORA, A_WIDTH), F32)
    hid = jnp.arange(A_WIDTH) // A_HEAD_DIM
    row = lambda name: prm[name][i].reshape(1, -1)
    return {
        "w_mix": w_mix.astype(BF16), "b_mix": b_mix.reshape(1, -1),
        "w_kvt": w[:, _OFF_C_K:_OFF_C_Z].T.astype(BF16), "b_kvt": b[_OFF_C_K:_OFF_C_Z].reshape(-1, 1),
        "w_z": _cols(w, *z_ranges).astype(BF16), "b_z": _cols(b, *z_ranges).reshape(1, -1),
        "w_gates": w[:, _OFF_GATES:].astype(BF16), "b_gates": b[_OFF_GATES:].reshape(1, -1),
        "mu_a": row("mu_a"), "w0_a": row("w0_a"), "a0_a": row("a0_a"), "k_k": row("k_k"), "k_a": row("k_a"),
        "wup_pad": jnp.concatenate([prm["w_decay_up"][i], zeros_lora], axis=0).astype(BF16),
        "aup_pad": jnp.concatenate([zeros_lora, prm["w_iclr_up"][i]], axis=0).astype(BF16),
        "head_sum": (hid[:, None] == hid[None, :]).astype(BF16),
        "r_k": row("r_k"), "gn_a_g": row("gn_a_g"), "gn_a_b": row("gn_a_b"),
        "conv_b_w": prm["conv_b_w"][i], "conv_b_b": row("conv_b_b"), "hn_b_g": row("hn_b_g"),
        "w_branch": prm["w_branch"][i].astype(BF16), "w_out": prm["w_out"][i].astype(BF16),
        "ln_g": row("ln_g"), "ln_b": row("ln_b"),
        "w_ple": prm["w_ple"][i].astype(BF16), "w_ple_gate": prm["w_ple_gate"][i].astype(BF16),
    }


def _run_group(x, p, init, ln_in_g, ln_in_b, layers):
    bsz, t, d = x.shape
    n = bsz * t
    shift0, wkv0, conv0, c0, n0, m0, k_past, v_past = init
    xf = x.reshape(n, d)
    new = [[] for _ in range(6)]
    kt = vt = None
    channel_major = lambda a: jnp.transpose(a, (0, 1, 3, 4, 2)).reshape(a.shape[0], a.shape[1], C_WIDTH, a.shape[2])
    if k_past is not None:
        k_past, v_past = channel_major(k_past), channel_major(v_past)
    for i, lw in enumerate(layers):
        pre_norm = (ln_in_g, ln_in_b) if i == 0 else None
        a_cols, b_qk, b_v, b_o, c_q, b_if, kt, vt = _inproj_call(xf, lw, i, bsz, t, kt, vt, pre_norm)
        seq = lambda a: a.reshape(bsz, t, a.shape[-1])
        ya, last, s_pairs = _rwkv_call(seq(a_cols), shift0[i], _wkv_to_pairs(wkv0[i]), lw)
        buf0 = jnp.pad(conv0[i], ((0, 0), (SUBLANES - (B_CONV - 1), 0), (0, 0)))
        m0b = jnp.broadcast_to(m0[i][:, :, None, None], (bsz, B_HEADS, 1, LANES))
        yc, yb, buf, c1, n1, m1 = _sb_mlstm_call(seq(c_q), kt, vt, i, k_past, v_past,
                                                 seq(b_qk), seq(b_v), seq(b_if), seq(b_o), buf0, c0[i],
                                                 n0[i][:, :, None, :], m0b, lw)
        xf = _merge_call(xf, ya.reshape(n, A_WIDTH), yb.reshape(n, B_WIDTH), yc.reshape(n, C_WIDTH),
                         p.reshape(DEPTH, n, D_PLE), i, lw, pre_norm)
        states = (last[:, 0], _wkv_from_pairs(s_pairs), buf[:, SUBLANES - (B_CONV - 1):], c1, n1[:, :, 0],
                  m1[:, :, 0, 0])
        for lst, s in zip(new, states):
            lst.append(s)
    token_major = lambda a: jnp.transpose(a.reshape(DEPTH, bsz, C_HEADS, C_HEAD_DIM, t), (0, 1, 4, 2, 3))
    return xf.reshape(bsz, t, d), [jnp.stack(lst) for lst in new] + [token_major(kt), token_major(vt)]


def kernel(x_prompt, x_sample, state_shift_a, state_wkv, state_conv_b, state_mlstm_c, state_mlstm_n, state_mlstm_m, cache_sb_k, cache_sb_v, p_prompt, p_sample, ln_in_g, ln_in_b, w_in, b_in, mu_a, w0_a, w_decay_up, a0_a, w_iclr_up, k_k, k_a, r_k, gn_a_g, gn_a_b, conv_b_w, conv_b_b, hn_b_g, w_branch, w_out, ln_g, ln_b, w_ple, w_ple_gate):
    prm = dict(mu_a=mu_a, w0_a=w0_a, w_decay_up=w_decay_up, a0_a=a0_a, w_iclr_up=w_iclr_up, k_k=k_k, k_a=k_a,
               r_k=r_k, gn_a_g=gn_a_g, gn_a_b=gn_a_b, conv_b_w=conv_b_w, conv_b_b=conv_b_b, hn_b_g=hn_b_g,
               w_branch=w_branch, w_out=w_out, ln_g=ln_g, ln_b=ln_b, w_ple=w_ple, w_ple_gate=w_ple_gate)
    layers = [_layer_params(i, w_in, b_in, prm) for i in range(DEPTH)]
    bp = x_prompt.shape[0]
    zeros = lambda *s: jnp.zeros((DEPTH, bp) + s, F32)
    init_prompt = (zeros(A_SHIFT), zeros(A_HEADS, A_HEAD_DIM, A_HEAD_DIM), zeros(B_CONV - 1, 2 * B_WIDTH),
                   zeros(B_HEADS, B_HEAD_DIM, B_HEAD_DIM), zeros(B_HEADS, B_HEAD_DIM), zeros(B_HEADS), None, None)
    init_sample = (state_shift_a, state_wkv, state_conv_b, state_mlstm_c, state_mlstm_n, state_mlstm_m,
                   cache_sb_k, cache_sb_v)
    y_prompt, sp = _run_group(x_prompt, p_prompt, init_prompt, ln_in_g, ln_in_b, layers)
    y_sample, ss = _run_group(x_sample, p_sample, init_sample, ln_in_g, ln_in_b, layers)
    return (y_prompt, y_sample, sp[0], ss[0], sp[1], ss[1], sp[2], ss[2], sp[3], ss[3],
            sp[4], ss[4], sp[5], ss[5], sp[6], ss[6], sp[7], ss[7])
```

```python
import functools

import jax
import jax.numpy as jnp
from jax import lax
from jax.experimental import pallas as pl
from jax.experimental.pallas import tpu as pltpu

F32 = jnp.float32
BF16 = jnp.bfloat16

D_MODEL = 1024
DEPTH = 4
CHUNK = 64
D_PLE = 256
A_HEADS = 8
A_HEAD_DIM = 64
A_WIDTH = A_HEADS * A_HEAD_DIM
A_LORA = 64
A_SHIFT = 3 * A_WIDTH + 2 * A_LORA
B_HEADS = 4
B_HEAD_DIM = 128
B_WIDTH = B_HEADS * B_HEAD_DIM
B_CONV = 4
C_HEADS = 8
C_HEAD_DIM = 64
C_WIDTH = C_HEADS * C_HEAD_DIM
N_BRANCH = 3
DEEPNORM_ALPHA = (2 * DEPTH) ** 0.25
LN_EPS = 1e-5
GN_EPS_A = 64e-5
HN_EPS = 1e-6

LANES = 128
SUBLANES = 8
VMEM_LIMIT_BYTES = 56 * 1024 * 1024

_OFF_A_COLS = 0
_OFF_A_Z = _OFF_A_COLS + A_SHIFT
_OFF_B_QK = _OFF_A_Z + A_WIDTH
_OFF_B_V = _OFF_B_QK + 2 * B_WIDTH
_OFF_B_I = _OFF_B_V + B_WIDTH
_OFF_B_F = _OFF_B_I + B_HEADS
_OFF_B_O = _OFF_B_F + B_HEADS
_OFF_B_Z = _OFF_B_O + B_WIDTH
_OFF_C_Q = _OFF_B_Z + B_WIDTH
_OFF_C_K = _OFF_C_Q + C_WIDTH
_OFF_C_V = _OFF_C_K + C_WIDTH
_OFF_C_Z = _OFF_C_V + C_WIDTH
_OFF_GATES = _OFF_C_Z + C_WIDTH
_N_IN = _OFF_GATES + N_BRANCH * D_MODEL

_MIX_WIDTHS = (A_SHIFT, 2 * B_WIDTH, B_WIDTH, B_WIDTH, C_WIDTH, LANES)

NN = ((1,), (0,))
NT = ((1,), (1,))
TN = ((0,), (0,))


def _dg(a, b, dims=NN):
    return lax.dot_general(a, b, (dims, ((), ())), preferred_element_type=F32)


def _mm(a, b, dims=NN):
    return _dg(a.astype(BF16), b.astype(BF16), dims)


def _split3(x):
    h1 = x.astype(BF16)
    r1 = x - h1.astype(F32)
    h2 = r1.astype(BF16)
    h3 = (r1 - h2.astype(F32)).astype(BF16)
    return h1, h2, h3


def _mm_exact_rhs(lhs_bf16, x):
    h1, h2, h3 = _split3(x)
    return _dg(lhs_bf16, h1) + (_dg(lhs_bf16, h2) + _dg(lhs_bf16, h3))


def _mm_exact_lhs(x, rhs_bf16):
    h1, h2, h3 = _split3(x)
    return _dg(h1, rhs_bf16) + (_dg(h2, rhs_bf16) + _dg(h3, rhs_bf16))


def _softplus(x):
    return jnp.maximum(x, 0.0) + jnp.log(1.0 + jnp.exp(-jnp.abs(x)))


def _sigmoid(x):
    return jax.nn.sigmoid(x)


def _silu(x):
    return x * jax.nn.sigmoid(x)


def _norm_rows(x, eps):
    mu = jnp.mean(x, axis=-1, keepdims=True)
    xc = x - mu
    var = jnp.mean(xc * xc, axis=-1, keepdims=True)
    return xc * lax.rsqrt(var + eps)


def _tile(n, pref):
    if n <= pref:
        return n
    t = pref - pref % SUBLANES
    while n % t:
        t -= SUBLANES
    return t


def _params(*sem):
    return pltpu.CompilerParams(dimension_semantics=sem, vmem_limit_bytes=VMEM_LIMIT_BYTES)


def _const_spec(shape):
    nd = len(shape)
    return pl.BlockSpec(shape, lambda *_: (0,) * nd, pipeline_mode=pl.Buffered(1))


def _inproj_kernel(x_ref, w_ref, b_ref, wkv_ref, bkv_ref, *refs, pre_norm):
    n_out = len(_MIX_WIDTHS) + 2
    out_refs = refs[-n_out:]
    x = x_ref[...]
    if pre_norm:
        x = _norm_rows(x, LN_EPS) * refs[0][...] + refs[1][...]
    xb = x.astype(BF16)
    off = 0
    for o_ref, width in zip(out_refs, _MIX_WIDTHS):
        o_ref[...] = _dg(xb, w_ref[:, off:off + width]) + b_ref[:, off:off + width]
        off += width
    kvt = _dg(wkv_ref[...], xb, NT) + bkv_ref[...]
    out_refs[-2][0, 0] = kvt[:C_WIDTH]
    out_refs[-1][0, 0] = kvt[C_WIDTH:]


def _inproj_call(x, lw, layer, bsz, t, kt_stack=None, vt_stack=None, pre_norm=None):
    n, d = x.shape
    tm = _tile(t, 512)
    per_b = t // tm
    wtot = sum(_MIX_WIDTHS)
    stack = jax.ShapeDtypeStruct((DEPTH, bsz, C_WIDTH, t), F32)
    stack_spec = pl.BlockSpec((1, 1, C_WIDTH, tm), lambda i: (layer, i // per_b, 0, i % per_b))
    in_specs = [pl.BlockSpec((tm, d), lambda i: (i, 0)), _const_spec((d, wtot)), _const_spec((1, wtot)),
                _const_spec((2 * C_WIDTH, d)), _const_spec((2 * C_WIDTH, 1))]
    args = [x, lw["w_mix"], lw["b_mix"], lw["w_kvt"], lw["b_kvt"]]
    if pre_norm is not None:
        in_specs += [_const_spec((1, d))] * 2
        args += [a.reshape(1, d) for a in pre_norm]
    aliases = {}
    if kt_stack is not None:
        n_mix = len(_MIX_WIDTHS)
        aliases = {len(args): n_mix, len(args) + 1: n_mix + 1}
        in_specs += [pl.BlockSpec(memory_space=pl.ANY)] * 2
        args += [kt_stack, vt_stack]
    return pl.pallas_call(
        functools.partial(_inproj_kernel, pre_norm=pre_norm is not None),
        out_shape=tuple(jax.ShapeDtypeStruct((n, wd), F32) for wd in _MIX_WIDTHS) + (stack, stack),
        grid=(n // tm,),
        in_specs=in_specs,
        out_specs=tuple(pl.BlockSpec((tm, wd), lambda i: (i, 0)) for wd in _MIX_WIDTHS) + (stack_spec, stack_spec),
        input_output_aliases=aliases,
        compiler_params=_params("parallel"),
        name="inproj",
    )(*args)


RWKV_RING = 3


def _round_robin(*steps):
    steps = list(steps)
    while steps:
        for s in list(steps):
            try:
                next(s)
            except StopIteration:
                steps.remove(s)


def _rwkv_pipe_kernel(cols_ref, last_ref, s0_ref, mu_ref, w0_ref, wup_ref, a0_ref, aup_ref, kk_ref, ka_ref, hsum_ref,
                      rk_ref, gng_ref, gnb_ref, y_o, last_o, s_o,
                      prev_scr, s_scr, at_r, rt_r, bt_r, kt_r, btw_r, ktw_r, v_r, bon_r, w_r, u0_r, x1_r, mrb_r, wl_r):
    t = pl.program_id(1)
    L = CHUNK
    tt = cols_ref.shape[1]
    gr = min(tt, 2 * L)
    cpg = gr // L
    ng = tt // gr
    n_pairs = A_HEADS // 2

    @pl.when(t == 0)
    def _():
        prev_scr[...] = last_ref[0]
        s_scr[...] = s0_ref[0]

    ri = lax.broadcasted_iota(jnp.int32, (L, L), 0)
    ci = lax.broadcasted_iota(jnp.int32, (L, L), 1)
    strict = ri > ci
    incl = ri >= ci
    tri = jnp.where(incl, 1.0, 0.0).astype(BF16)
    eye = jnp.where(ri == ci, 1.0, 0.0)
    lane = lax.broadcasted_iota(jnp.int32, (1, LANES), 1)
    head_masks = (jnp.where(lane < A_HEAD_DIM, 1.0, 0.0), jnp.where(lane >= A_HEAD_DIM, 1.0, 0.0))
    inv_d = 1.0 / A_HEAD_DIM
    pair = lambda p: slice(p * LANES, (p + 1) * LANES)
    chunk = lambda c: slice(c * L, (c + 1) * L)
    heads = [(p, hm) for p in range(n_pairs) for hm in head_masks]

    def prologue(g):
        gp = g
        slot = lax.rem(gp, RWKV_RING)
        r0 = pl.multiple_of(gp * gr, gr)
        u = cols_ref[0, pl.ds(r0, gr), :]
        above = cols_ref[0, pl.ds(pl.multiple_of(jnp.maximum(r0 - SUBLANES, 0), SUBLANES), SUBLANES), :]
        before = jnp.where(gp == 0, prev_scr[...], above[SUBLANES - 1:SUBLANES, :])
        row = lax.broadcasted_iota(jnp.int32, u.shape, 0)
        prev = jnp.where(row == 0, before, pltpu.roll(u, 1, 0))
        xs = u + (prev - u) * mu_ref[...]
        yield
        r = xs[:, 0:A_WIDTH]
        k = xs[:, A_WIDTH:2 * A_WIDTH]
        v = xs[:, 2 * A_WIDTH:3 * A_WIDTH]
        lora = xs[:, 3 * A_WIDTH:]
        w_log = -_softplus(-(w0_ref[...] + _mm(jnp.tanh(lora), wup_ref[...]))) - 0.5
        a = _sigmoid(a0_ref[...] + _mm(lora, aup_ref[...]))
        yield
        kk = k * kk_ref[...]
        ss = _mm_exact_lhs(kk * kk, hsum_ref[...])
        kk = kk / jnp.maximum(jnp.sqrt(ss), 1e-12)
        yield
        kh = k * (1.0 + (a - 1.0) * ka_ref[...])
        lw = -jnp.exp(w_log)
        cin = jnp.concatenate([_mm_exact_rhs(tri, lw[chunk(c), :]) for c in range(cpg)], axis=0)
        yield
        ein = jnp.exp(cin)
        einv = jnp.exp(-cin)
        bt = kk * a * einv
        kt = kh * einv
        at_r[slot] = -kk * jnp.exp(cin - lw)
        rt_r[slot] = r * ein
        yield
        bt_r[slot] = bt
        kt_r[slot] = kt
        v_r[slot] = v
        bon_r[slot] = r * kh * rk_ref[...]
        yield
        for c in range(cpg):
            wl = ein[(c + 1) * L - 1:(c + 1) * L, :]
            wl_r[slot, c] = wl
            btw_r[slot, chunk(c), :] = bt[chunk(c), :] * wl
            ktw_r[slot, chunk(c), :] = kt[chunk(c), :] * wl
        yield

    def state_free(g):
        slot = lax.rem(g - 1, RWKV_RING)
        items = [(c, p, hm) for c in range(cpg) for p, hm in heads]
        load = lambda ring: [ring[slot, chunk(c), pair(p)] * hm for c, p, hm in items]
        at_h, rt_h, v_h = load(at_r), load(rt_r), load(v_r)
        lhs = [jnp.concatenate([x, y], axis=0).astype(BF16) for x, y in zip(at_h, rt_h)]
        yield
        sc_b = [_dg(x, bt_r[slot, chunk(c), pair(p)].astype(BF16), NT) for x, (c, p, _) in zip(lhs, items)]
        sc_k = [_dg(x, kt_r[slot, chunk(c), pair(p)].astype(BF16), NT) for x, (c, p, _) in zip(lhs, items)]
        yield
        a_ab = [jnp.where(strict, s[:L], 0.0) for s in sc_b]
        a_ak = [jnp.where(strict, s[:L], 0.0) for s in sc_k]
        m_rb = [jnp.where(incl, s[L:], 0.0) for s in sc_b]
        m_rk = [jnp.where(incl, s[L:], 0.0) for s in sc_k]
        tm = [eye + x for x in a_ab]
        ap = a_ab
        for _ in range(5):
            ap = [_mm(x, x) for x in ap]
            yield
            tm = [x + _mm(x, y) for x, y in zip(tm, ap)]
            yield
        x1 = [_mm(jnp.concatenate([x, y], axis=0), vh) for x, y, vh in zip(a_ak, m_rk, v_h)]
        yield
        wu = [_mm(x, jnp.concatenate([y, z[:L]], axis=1)) for x, y, z in zip(tm, at_h, x1)]
        yield
        for n in range(0, len(items), 2):
            c, p, _ = items[n]
            w_r[slot, chunk(c), pair(p)] = wu[n][:, :LANES] + wu[n + 1][:, :LANES]
            u0_r[slot, chunk(c), pair(p)] = wu[n][:, LANES:] + wu[n + 1][:, LANES:]
            x1_r[slot, chunk(c), pair(p)] = x1[n][L:] + x1[n + 1][L:]
            mrb_r[slot, c, 2 * p] = m_rb[n]
            mrb_r[slot, c, 2 * p + 1] = m_rb[n + 1]
        yield

    def state_step(g):
        g2 = g - 2
        slot = lax.rem(g2, RWKV_RING)
        hm0, hm1 = head_masks
        y_pairs = [[] for _ in range(n_pairs)]
        for c in range(cpg):
            s_pairs = [s_scr[p] for p in range(n_pairs)]
            s_pairs_b = [s.astype(BF16) for s in s_pairs]
            w_h = [w_r[slot, chunk(c), pair(p)] * hm for p, hm in heads]
            rt_h = [rt_r[slot, chunk(c), pair(p)] * hm for p, hm in heads]
            st = [_dg(jnp.concatenate([x, y], axis=0).astype(BF16), s_pairs_b[p], NT)
                  for x, y, (p, _) in zip(w_h, rt_h, heads)]
            yield
            u_h = [u0_r[slot, chunk(c), pair(p)] * hm + s[:L] for s, (p, hm) in zip(st, heads)]
            y_h = [s[L:] + _mm(mrb_r[slot, c, n], x) for n, (s, x) in enumerate(zip(st, u_h))]
            upd = []
            for p in range(n_pairs):
                vp, bp, kp = v_r[slot, chunk(c), pair(p)], btw_r[slot, chunk(c), pair(p)], ktw_r[slot, chunk(c), pair(p)]
                upd.append(_mm(jnp.concatenate([u_h[2 * p], vp * hm0, u_h[2 * p + 1], vp * hm1], axis=0),
                               jnp.concatenate([bp * hm0, kp * hm0, bp * hm1, kp * hm1], axis=0), TN))
            yield
            wl = wl_r[slot, c]
            for p in range(n_pairs):
                s_scr[p] = s_pairs[p] * wl[:, pair(p)] + upd[p]
                y_pairs[p].append(y_h[2 * p] + y_h[2 * p + 1] + x1_r[slot, chunk(c), pair(p)])
            yield
        rows = pl.ds(pl.multiple_of(g2 * gr, gr), gr)
        for p in range(n_pairs):
            y_pair = jnp.concatenate(y_pairs[p], axis=0)
            normed = jnp.zeros(y_pair.shape, F32)
            bonus = jnp.zeros(y_pair.shape, F32)
            for hm in head_masks:
                mu = jnp.sum(y_pair * hm, axis=-1, keepdims=True) * inv_d
                yc = (y_pair - mu) * hm
                var = jnp.sum(yc * yc, axis=-1, keepdims=True) * inv_d
                normed = normed + yc * lax.rsqrt(var + GN_EPS_A)
                bonus = bonus + (jnp.sum(bon_r[slot, :, pair(p)] * hm, axis=-1, keepdims=True)
                                 * (v_r[slot, :, pair(p)] * hm))
            y_o[0, rows, pair(p)] = (normed * gng_ref[:, pair(p)] + gnb_ref[:, pair(p)]) + bonus
            yield

    def trip(g, carry):
        _round_robin(prologue(g), state_free(g), state_step(g))
        return carry

    at = jnp.int32
    if ng == 1:
        for g, step in enumerate((prologue, state_free, state_step)):
            _round_robin(step(at(g)))
    else:
        _round_robin(prologue(at(0)))
        _round_robin(prologue(at(1)), state_free(at(1)))
        lax.fori_loop(2, ng, trip, 0)
        _round_robin(state_free(at(ng)), state_step(at(ng)))
        _round_robin(state_step(at(ng + 1)))
    prev_scr[...] = cols_ref[0, tt - 1:tt, :]
    last_o[0] = cols_ref[0, tt - 1:tt, :]
    s_o[0] = s_scr[...]


def _rwkv_call(cols, last, s0_pairs, lw):
    bsz, t, _ = cols.shape
    tt = _tile(t, 2048)
    gr = min(tt, 2 * CHUNK)
    blk = lambda w: pl.BlockSpec((1, tt, w), lambda b, i: (b, i, 0))
    row = pl.BlockSpec((1, 1, A_SHIFT), lambda b, i: (b, 0, 0))
    st = pl.BlockSpec((1, A_HEADS // 2, LANES, LANES), lambda b, i: (b, 0, 0, 0))
    ring = pltpu.VMEM((RWKV_RING, gr, A_WIDTH), F32)
    return pl.pallas_call(
        _rwkv_pipe_kernel,
        out_shape=(jax.ShapeDtypeStruct((bsz, t, A_WIDTH), F32),
                   jax.ShapeDtypeStruct((bsz, 1, A_SHIFT), F32),
                   jax.ShapeDtypeStruct((bsz, A_HEADS // 2, LANES, LANES), F32)),
        grid=(bsz, t // tt),
        in_specs=[blk(A_SHIFT), row, st, _const_spec((1, A_SHIFT)), _const_spec((1, A_WIDTH)),
                  _const_spec((2 * A_LORA, A_WIDTH)), _const_spec((1, A_WIDTH)), _const_spec((2 * A_LORA, A_WIDTH)),
                  _const_spec((1, A_WIDTH)), _const_spec((1, A_WIDTH)), _const_spec((A_WIDTH, A_WIDTH)),
                  _const_spec((1, A_WIDTH)), _const_spec((1, A_WIDTH)), _const_spec((1, A_WIDTH))],
        out_specs=(blk(A_WIDTH), row, st),
        scratch_shapes=[pltpu.VMEM((1, A_SHIFT), F32), pltpu.VMEM((A_HEADS // 2, LANES, LANES), F32)]
        + [ring] * 11
        + [pltpu.VMEM((RWKV_RING, gr // CHUNK, A_HEADS, CHUNK, CHUNK), F32),
           pltpu.VMEM((RWKV_RING, gr // CHUNK, 1, A_WIDTH), F32)],
        compiler_params=_params("parallel", "arbitrary"),
        name="rwkv",
    )(cols, last.reshape(bsz, 1, A_SHIFT), s0_pairs, lw["mu_a"], lw["w0_a"], lw["wup_pad"], lw["a0_a"],
      lw["aup_pad"], lw["k_k"], lw["k_a"], lw["head_sum"], lw["r_k"], lw["gn_a_g"], lw["gn_a_b"])


def _wkv_to_pairs(s):
    bsz = s.shape[0]
    s = s.reshape(bsz, A_HEADS // 2, 2, A_HEAD_DIM, A_HEAD_DIM)
    z = jnp.zeros_like(s[:, :, 0])
    top = jnp.concatenate([s[:, :, 0], z], axis=-1)
    bot = jnp.concatenate([z, s[:, :, 1]], axis=-1)
    return jnp.concatenate([top, bot], axis=-2)


def _wkv_from_pairs(sp):
    bsz = sp.shape[0]
    d = A_HEAD_DIM
    s = jnp.stack([sp[:, :, :d, :d], sp[:, :, d:, d:]], axis=2)
    return s.reshape(bsz, A_HEADS, d, d)


def _mlstm_steps(t, qk_ref, v_ref, if_ref, o_ref, buf0_ref, c0_ref, n0_ref, m0_ref, cw_ref, cb_ref, hng_ref,
                 y_o, buf_o, c_o, n_o, m_o, ext_scr, c_scr, n_scr, m_scr):
    tt = qk_ref.shape[1]
    L = min(tt, LANES)
    assert tt == L
    pad = SUBLANES

    @pl.when(t == 0)
    def _():
        ext_scr[0:pad, :] = buf0_ref[0]
        c_scr[...] = c0_ref[0]
        n_scr[...] = n0_ref[0]
        m_scr[...] = m0_ref[0]

    ext_scr[pad:pad + tt, :] = qk_ref[0]
    ri = lax.broadcasted_iota(jnp.int32, (L, L), 0)
    ci = lax.broadcasted_iota(jnp.int32, (L, L), 1)
    incl = ri >= ci
    eye = ri == ci
    tri = jnp.where(incl, 1.0, 0.0).astype(BF16)
    k_scale = B_HEAD_DIM ** -0.5

    hs = range(B_HEADS)
    sls = [slice(h * B_HEAD_DIM, (h + 1) * B_HEAD_DIM) for h in hs]

    x = ext_scr[...]
    conv = cb_ref[...] + x[pad:, :] * cw_ref[B_CONV - 1:B_CONV, :]
    for tap in range(B_CONV - 1):
        conv = conv + pltpu.roll(x, B_CONV - 1 - tap, 0)[pad:, :] * cw_ref[tap:tap + 1, :]
    yield
    qk = _silu(conv)
    yield
    ifg = if_ref[0]
    bcum = _mm_exact_rhs(tri, -_softplus(-ifg))
    yield
    q_h = [qk[:, sl] for sl in sls]
    k_h = [qk[:, B_WIDTH + h * B_HEAD_DIM:B_WIDTH + (h + 1) * B_HEAD_DIM] * k_scale for h in hs]
    v_h = [v_ref[0, :, sl] for sl in sls]
    icol = [ifg[:, h:h + 1] for h in hs]
    bcol = [bcum[:, B_HEADS + h:B_HEADS + h + 1] for h in hs]
    qk_t = [_mm(q, k, NT) for q, k in zip(q_h, k_h)]
    yield
    dcol = [i_ - b for i_, b in zip(icol, bcol)]
    drow = [jnp.sum(jnp.where(eye, d, 0.0), axis=0, keepdims=True) for d in dcol]
    dmat = [jnp.where(incl, b + d, -jnp.inf) for b, d in zip(bcol, drow)]
    yield
    dmax = [jnp.max(d, axis=-1, keepdims=True) for d in dmat]
    yield
    c_h = [c_scr[h] for h in hs]
    n_h = [n_scr[h] for h in hs]
    m_h = [m_scr[h][:, 0:1] for h in hs]
    qc = [_mm(q, c, NT) for q, c in zip(q_h, c_h)]
    yield
    inter = [b + m for b, m in zip(bcol, m_h)]
    m_t = [jnp.maximum(x_, d) for x_, d in zip(inter, dmax)]
    wmat = [jnp.exp(d - m) * s for d, m, s in zip(dmat, m_t, qk_t)]
    yield
    scale = [jnp.exp(x_ - m) for x_, m in zip(inter, m_t)]
    wv = [_mm(w, v) for w, v in zip(wmat, v_h)]
    yield
    m_new = [m[L - 1:L, :] for m in m_t]
    gcol = [jnp.exp(b[L - 1:L, :] + d - m) for b, d, m in zip(bcol, dcol, m_new)]
    c_sc = [jnp.exp(x_[L - 1:L, :] - m) for x_, m in zip(inter, m_new)]
    gvk = [_mm(g * v, k, TN) for g, v, k in zip(gcol, v_h, k_h)]
    yield
    for h in hs:
        num = wv[h] + scale[h] * qc[h]
        den = (jnp.sum(wmat[h], axis=-1, keepdims=True)
               + scale[h] * jnp.sum(q_h[h] * n_h[h], axis=-1, keepdims=True))
        hid = num / jnp.maximum(jnp.abs(den), jnp.exp(-m_t[h]))
        y_o[0, :, sls[h]] = _sigmoid(o_ref[0, :, sls[h]]) * (_norm_rows(hid, HN_EPS) * hng_ref[:, sls[h]])
        yield
    for h in hs:
        c_scr[h] = c_sc[h] * c_h[h] + gvk[h]
        n_scr[h] = c_sc[h] * n_h[h] + jnp.sum(gcol[h] * k_h[h], axis=0, keepdims=True)
        m_scr[h] = jnp.broadcast_to(m_new[h], (1, LANES))
    last = ext_scr[tt:tt + pad, :]
    ext_scr[0:pad, :] = last
    buf_o[0] = last
    c_o[0] = c_scr[...]
    n_o[0] = n_scr[...]
    m_o[0] = m_scr[...]


SB_SKIP_LOG = -105.0


_N_MLSTM_IN, _N_MLSTM_OUT, _N_SB_SCRATCH = 11, 5, 4


def _sb_mlstm_kernel(*refs, n_past_blocks):
    n_att = 5 if n_past_blocks else 3
    att_in, refs = refs[:n_att], refs[n_att:]
    ml_in, o_ref, refs = refs[:_N_MLSTM_IN], refs[_N_MLSTM_IN], refs[_N_MLSTM_IN + 1:]
    ml_out, refs = refs[:_N_MLSTM_OUT], refs[_N_MLSTM_OUT:]
    (q2_scr, c_scr, acc_scr, pad_scr), ml_scr = refs[:_N_SB_SCRATCH], refs[_N_SB_SCRATCH:]
    if n_past_blocks:
        q_ref, k_ref, v_ref, kp_ref, vp_ref = att_in
    else:
        q_ref, k_ref, v_ref = att_in
    mlstm = _mlstm_steps(pl.program_id(1), *ml_in, *ml_out, *ml_scr)
    qi = pl.program_id(1)
    qb = q_ref.shape[1]
    kb = LANES
    n_pairs = C_HEADS // 2
    n_rows = C_HEADS * qb

    rj = lax.broadcasted_iota(jnp.int32, (2 * kb, 2 * kb), 0) & (kb - 1)
    cj = lax.broadcasted_iota(jnp.int32, (2 * kb, 2 * kb), 1)
    cum2 = jnp.where((cj >= kb) | (rj > cj), 1.0, 0.0).astype(BF16)
    qpos = lax.broadcasted_iota(jnp.int32, (n_rows, kb), 0) & (qb - 1)
    kpos = lax.broadcasted_iota(jnp.int32, (n_rows, kb), 1)
    diag_mask = kpos < qpos
    lane = lax.broadcasted_iota(jnp.int32, (1, LANES), 1)
    lo_lanes = lane < C_HEAD_DIM
    q = q_ref[0] * (C_HEAD_DIM ** -0.5)
    for p in range(n_pairs):
        qs = q[:, p * LANES:(p + 1) * LANES]
        q2_scr[p] = jnp.concatenate([jnp.where(lo_lanes, qs, 0.0), jnp.where(lo_lanes, 0.0, qs)], axis=0).astype(BF16)

    def load_t(ref, slot, p, keys):
        x = ref[0, 0, p * LANES:(p + 1) * LANES, keys]
        if x.shape[1] == kb:
            return x.astype(BF16)
        pad_scr[slot] = jnp.zeros((LANES, kb), F32)
        pad_scr[slot, :, 0:x.shape[1]] = x
        return pad_scr[slot].astype(BF16)

    def half_block(pairs, kr, vr, keys, first, c_max):
        rows = slice(pairs[0] * 2 * qb, (pairs[-1] + 1) * 2 * qb)
        mask = diag_mask[:rows.stop - rows.start]
        z = jnp.concatenate([_dg(q2_scr[p], load_t(kr, 2 * p, p, keys)) for p in pairs], axis=0)
        yield
        sp = _softplus(z)
        lk = jnp.where(mask, -sp, 0.0) if first else -sp
        h1 = lk.astype(BF16)
        h2 = (lk - h1.astype(F32)).astype(BF16)
        yield
        cs = _dg(jnp.concatenate([h1, h2], axis=1), cum2)
        yield
        if first:
            a = jnp.where(mask, jnp.exp((z - sp) + cs[:, :kb]), 0.0)
            c_new = cs[:, kb:]
        else:
            c_old = c_scr[rows]
            a = jnp.exp((z - sp) + cs[:, :kb] + c_old)
            c_new = c_old + cs[:, kb:]
        c_scr[rows] = c_new
        c_max.append(jnp.max(c_new))
        ab = a.astype(BF16)
        yield
        for n, p in enumerate(pairs):
            rs = slice(p * 2 * qb, (p + 1) * 2 * qb)
            upd = _dg(ab[n * 2 * qb:(n + 1) * 2 * qb], load_t(vr, 2 * p + 1, p, keys), NT)
            acc_scr[rs] = upd if first else acc_scr[rs] + upd
        yield

    def block(kr, vr, keys, first, others=()):
        c_max = []
        lead = half_block(tuple(range(n_pairs // 2)), kr, vr, keys, first, c_max)
        lag = half_block(tuple(range(n_pairs // 2, n_pairs)), kr, vr, keys, first, c_max)
        next(lead)
        _round_robin(lead, lag, *others)
        return jnp.maximum(*c_max) > SB_SKIP_LOG

    own_keys = pl.ds(pl.multiple_of(qi * qb, qb), qb) if qb == kb else pl.ds(0, qb)
    go = block(k_ref, v_ref, own_keys, True, (mlstm,))

    def sweep(kr, vr, n_blocks, go):
        def cond(carry):
            j, more = carry
            return jnp.logical_and(j < n_blocks, more)

        def body(carry):
            j, _ = carry
            keys = pl.ds(pl.multiple_of((n_blocks - 1 - j) * kb, kb), kb)
            return j + 1, block(kr, vr, keys, False)

        return lax.while_loop(cond, body, (jnp.int32(0), go))[1]

    if qb == kb:
        go = sweep(k_ref, v_ref, qi, go)
    if n_past_blocks:
        go = sweep(kp_ref, vp_ref, n_past_blocks, go)

    for p in range(n_pairs):
        r0 = p * 2 * qb
        o_ref[0, :, p * LANES:(p + 1) * LANES] = jnp.where(lo_lanes, acc_scr[r0:r0 + qb], acc_scr[r0 + qb:r0 + 2 * qb])


def _sb_mlstm_call(q, kt, vt, layer, kt_past, vt_past, qk, v, ifg, o, buf0, c0, n0, m0, lw):
    bsz, t, _ = q.shape
    qb = min(LANES, t)
    assert t % qb == 0 and (qb == LANES or t == qb)
    n_past = 0 if kt_past is None else kt_past.shape[-1]
    assert n_past % LANES == 0
    blk = lambda w: pl.BlockSpec((1, qb, w), lambda b, i: (b, i, 0))
    full = lambda n: pl.BlockSpec((1, 1, C_WIDTH, n), lambda b, i: (layer, b, 0, 0))
    fix = lambda *s: pl.BlockSpec((1,) + s, lambda b, i: (b,) + (0,) * len(s))
    in_specs = [blk(C_WIDTH), full(t), full(t)]
    args = [q, kt, vt]
    if n_past:
        in_specs += [full(n_past), full(n_past)]
        args += [kt_past, vt_past]
    in_specs += [blk(2 * B_WIDTH), blk(B_WIDTH), blk(LANES), blk(B_WIDTH),
                 fix(SUBLANES, 2 * B_WIDTH), fix(B_HEADS, B_HEAD_DIM, B_HEAD_DIM), fix(B_HEADS, 1, B_HEAD_DIM),
                 fix(B_HEADS, 1, LANES),
                 _const_spec((B_CONV, 2 * B_WIDTH)), _const_spec((1, 2 * B_WIDTH)), _const_spec((1, B_WIDTH))]
    args += [qk, v, ifg, o, buf0, c0, n0, m0, lw["conv_b_w"], lw["conv_b_b"], lw["hn_b_g"]]
    assert len(in_specs) - (5 if n_past else 3) == _N_MLSTM_IN
    return pl.pallas_call(
        functools.partial(_sb_mlstm_kernel, n_past_blocks=n_past // LANES),
        out_shape=(jax.ShapeDtypeStruct((bsz, t, C_WIDTH), F32),
                   jax.ShapeDtypeStruct((bsz, t, B_WIDTH), F32),
                   jax.ShapeDtypeStruct((bsz, SUBLANES, 2 * B_WIDTH), F32),
                   jax.ShapeDtypeStruct((bsz, B_HEADS, B_HEAD_DIM, B_HEAD_DIM), F32),
                   jax.ShapeDtypeStruct((bsz, B_HEADS, 1, B_HEAD_DIM), F32),
                   jax.ShapeDtypeStruct((bsz, B_HEADS, 1, LANES), F32)),
        grid=(bsz, t // qb),
        in_specs=in_specs,
        out_specs=(blk(C_WIDTH), blk(B_WIDTH), fix(SUBLANES, 2 * B_WIDTH), fix(B_HEADS, B_HEAD_DIM, B_HEAD_DIM),
                   fix(B_HEADS, 1, B_HEAD_DIM), fix(B_HEADS, 1, LANES)),
        scratch_shapes=[pltpu.VMEM((C_HEADS // 2, 2 * qb, LANES), BF16),
                        pltpu.VMEM((C_HEADS * qb, LANES), F32), pltpu.VMEM((C_HEADS * qb, LANES), F32),
                        pltpu.VMEM((C_HEADS, LANES, LANES), F32),
                        pltpu.VMEM((qb + SUBLANES, 2 * B_WIDTH), F32),
                        pltpu.VMEM((B_HEADS, B_HEAD_DIM, B_HEAD_DIM), F32),
                        pltpu.VMEM((B_HEADS, 1, B_HEAD_DIM), F32),
                        pltpu.VMEM((B_HEADS, 1, LANES), F32)],
        compiler_params=_params("parallel", "arbitrary"),
        name="sb_mlstm",
    )(*args)


def _merge_kernel(x_ref, ya_ref, yb_ref, yc_ref, p_ref, wz_ref, bz_ref, wg_ref, bg_ref, wbr_ref, wout_ref,
                  lng_ref, lnb_ref, wple_ref, wpg_ref, *refs, pre_norm):
    o_ref = refs[-1]
    width = A_WIDTH
    half_rows = x_ref.shape[0] // 2

    def half(r):
        rows = slice(r * half_rows, (r + 1) * half_rows)
        x = x_ref[rows, :]
        if pre_norm:
            x = _norm_rows(x, LN_EPS) * refs[0][...] + refs[1][...]
        xb = x.astype(BF16)
        mix_pre = jnp.zeros(x.shape, F32)
        for n, y_ref in enumerate((ya_ref, yb_ref, yc_ref)):
            z = _dg(xb, wz_ref[:, n * width:(n + 1) * width]) + bz_ref[:, n * width:(n + 1) * width]
            yield
            ys = y_ref[rows, :] * _silu(z)
            br = _mm(ys, wbr_ref[n])
            yield
            gate = _dg(xb, wg_ref[:, n * D_MODEL:(n + 1) * D_MODEL]) + bg_ref[:, n * D_MODEL:(n + 1) * D_MODEL]
            yield
            mix_pre = mix_pre + _sigmoid(gate) * br
        mix = _mm(mix_pre, wout_ref[...])
        yield
        x1 = _norm_rows(DEEPNORM_ALPHA * x + mix, LN_EPS) * lng_ref[...] + lnb_ref[...]
        yield
        o_ref[rows, :] = x1 + _mm(p_ref[0, rows, :], wple_ref[...]) * _sigmoid(_mm(x1, wpg_ref[...]))
        yield

    lead, lag = half(0), half(1)
    next(lead)
    _round_robin(lead, lag)


def _merge_call(x, ya, yb, yc, p_all, layer, lw, pre_norm=None):
    n, d = x.shape
    tm = _tile(n, 512)
    rows = lambda w: pl.BlockSpec((tm, w), lambda i: (i, 0))
    in_specs = [rows(d), rows(A_WIDTH), rows(B_WIDTH), rows(C_WIDTH),
                pl.BlockSpec((1, tm, D_PLE), lambda i: (layer, i, 0)),
                _const_spec((d, N_BRANCH * A_WIDTH)), _const_spec((1, N_BRANCH * A_WIDTH)),
                _const_spec((d, N_BRANCH * d)), _const_spec((1, N_BRANCH * d)),
                _const_spec((N_BRANCH, A_WIDTH, d)), _const_spec((d, d)),
                _const_spec((1, d)), _const_spec((1, d)), _const_spec((D_PLE, d)), _const_spec((d, d))]
    args = [x, ya, yb, yc, p_all, lw["w_z"], lw["b_z"], lw["w_gates"], lw["b_gates"], lw["w_branch"], lw["w_out"],
            lw["ln_g"], lw["ln_b"], lw["w_ple"], lw["w_ple_gate"]]
    if pre_norm is not None:
        in_specs += [_const_spec((1, d))] * 2
        args += [a.reshape(1, d) for a in pre_norm]
    return pl.pallas_call(
        functools.partial(_merge_kernel, pre_norm=pre_norm is not None),
        out_shape=jax.ShapeDtypeStruct((n, d), F32),
        grid=(n // tm,),
        in_specs=in_specs,
        out_specs=rows(d),
        compiler_params=_params("parallel"),
        name="merge",
    )(*args)


def _cols(w, *ranges):
    return jnp.concatenate([w[..., a:b] for a, b in ranges], axis=-1)


def _layer_params(i, w_in, b_in, prm):
    w = w_in[i]
    b = b_in[i]
    mix_ranges = ((_OFF_A_COLS, _OFF_A_Z), (_OFF_B_QK, _OFF_B_V), (_OFF_B_V, _OFF_B_I), (_OFF_B_O, _OFF_B_Z),
                  (_OFF_C_Q, _OFF_C_K), (_OFF_B_I, _OFF_B_O))
    pad = LANES - 2 * B_HEADS
    w_mix = jnp.pad(_cols(w, *mix_ranges), ((0, 0), (0, pad)))
    b_mix = jnp.pad(_cols(b, *mix_ranges), ((0, pad),))
    z_ranges = ((_OFF_A_Z, _OFF_B_QK), (_OFF_B_Z, _OFF_C_Q), (_OFF_C_Z, _OFF_GATES))
    zeros_lora = jnp.zeros((A_LORA, A_WIDTH), F32)
    hid = jnp.arange(A_WIDTH) // A_HEAD_DIM
    row = lambda name: prm[name][i].reshape(1, -1)
    return {
        "w_mix": w_mix.astype(BF16), "b_mix": b_mix.reshape(1, -1),
        "w_kvt": w[:, _OFF_C_K:_OFF_C_Z].T.astype(BF16), "b_kvt": b[_OFF_C_K:_OFF_C_Z].reshape(-1, 1),
        "w_z": _cols(w, *z_ranges).astype(BF16), "b_z": _cols(b, *z_ranges).reshape(1, -1),
        "w_gates": w[:, _OFF_GATES:].astype(BF16), "b_gates": b[_OFF_GATES:].reshape(1, -1),
        "mu_a": row("mu_a"), "w0_a": row("w0_a"), "a0_a": row("a0_a"), "k_k": row("k_k"), "k_a": row("k_a"),
        "wup_pad": jnp.concatenate([prm["w_decay_up"][i], zeros_lora], axis=0).astype(BF16),
        "aup_pad": jnp.concatenate([zeros_lora, prm["w_iclr_up"][i]], axis=0).astype(BF16),
        "head_sum": (hid[:, None] == hid[None, :]).astype(BF16),
        "r_k": row("r_k"), "gn_a_g": row("gn_a_g"), "gn_a_b": row("gn_a_b"),
        "conv_b_w": prm["conv_b_w"][i], "conv_b_b": row("conv_b_b"), "hn_b_g": row("hn_b_g"),
        "w_branch": prm["w_branch"][i].astype(BF16), "w_out": prm["w_out"][i].astype(BF16),
        "ln_g": row("ln_g"), "ln_b": row("ln_b"),
        "w_ple": prm["w_ple"][i].astype(BF16), "w_ple_gate": prm["w_ple_gate"][i].astype(BF16),
    }


def _run_group(x, p, init, ln_in_g, ln_in_b, layers):
    bsz, t, d = x.shape
    n = bsz * t
    shift0, wkv0, conv0, c0, n0, m0, k_past, v_past = init
    xf = x.reshape(n, d)
    new = [[] for _ in range(6)]
    kt = vt = None
    channel_major = lambda a: jnp.transpose(a, (0, 1, 3, 4, 2)).reshape(a.shape[0], a.shape[1], C_WIDTH, a.shape[2])
    if k_past is not None:
        k_past, v_past = channel_major(k_past), channel_major(v_past)
    for i, lw in enumerate(layers):
        pre_norm = (ln_in_g, ln_in_b) if i == 0 else None
        a_cols, b_qk, b_v, b_o, c_q, b_if, kt, vt = _inproj_call(xf, lw, i, bsz, t, kt, vt, pre_norm)
        seq = lambda a: a.reshape(bsz, t, a.shape[-1])
        ya, last, s_pairs = _rwkv_call(seq(a_cols), shift0[i], _wkv_to_pairs(wkv0[i]), lw)
        buf0 = jnp.pad(conv0[i], ((0, 0), (SUBLANES - (B_CONV - 1), 0), (0, 0)))
        m0b = jnp.broadcast_to(m0[i][:, :, None, None], (bsz, B_HEADS, 1, LANES))
        yc, yb, buf, c1, n1, m1 = _sb_mlstm_call(seq(c_q), kt, vt, i, k_past, v_past,
                                                 seq(b_qk), seq(b_v), seq(b_if), seq(b_o), buf0, c0[i],
                                                 n0[i][:, :, None, :], m0b, lw)
        xf = _merge_call(xf, ya.reshape(n, A_WIDTH), yb.reshape(n, B_WIDTH), yc.reshape(n, C_WIDTH),
                         p.reshape(DEPTH, n, D_PLE), i, lw, pre_norm)
        states = (last[:, 0], _wkv_from_pairs(s_pairs), buf[:, SUBLANES - (B_CONV - 1):], c1, n1[:, :, 0],
                  m1[:, :, 0, 0])
        for lst, s in zip(new, states):
            lst.append(s)
    token_major = lambda a: jnp.transpose(a.reshape(DEPTH, bsz, C_HEADS, C_HEAD_DIM, t), (0, 1, 4, 2, 3))
    return xf.reshape(bsz, t, d), [jnp.stack(lst) for lst in new] + [token_major(kt), token_major(vt)]


def kernel(x_prompt, x_sample, state_shift_a, state_wkv, state_conv_b, state_mlstm_c, state_mlstm_n, state_mlstm_m, cache_sb_k, cache_sb_v, p_prompt, p_sample, ln_in_g, ln_in_b, w_in, b_in, mu_a, w0_a, w_decay_up, a0_a, w_iclr_up, k_k, k_a, r_k, gn_a_g, gn_a_b, conv_b_w, conv_b_b, hn_b_g, w_branch, w_out, ln_g, ln_b, w_ple, w_ple_gate):
    prm = dict(mu_a=mu_a, w0_a=w0_a, w_decay_up=w_decay_up, a0_a=a0_a, w_iclr_up=w_iclr_up, k_k=k_k, k_a=k_a,
               r_k=r_k, gn_a_g=gn_a_g, gn_a_b=gn_a_b, conv_b_w=conv_b_w, conv_b_b=conv_b_b, hn_b_g=hn_b_g,
               w_branch=w_branch, w_out=w_out, ln_g=ln_g, ln_b=ln_b, w_ple=w_ple, w_ple_gate=w_ple_gate)
    layers = [_layer_params(i, w_in, b_in, prm) for i in range(DEPTH)]
    bp = x_prompt.shape[0]
    zeros = lambda *s: jnp.zeros((DEPTH, bp) + s, F32)
    init_prompt = (zeros(A_SHIFT), zeros(A_HEADS, A_HEAD_DIM, A_HEAD_DIM), zeros(B_CONV - 1, 2 * B_WIDTH),
                   zeros(B_HEADS, B_HEAD_DIM, B_HEAD_DIM), zeros(B_HEADS, B_HEAD_DIM), zeros(B_HEADS), None, None)
    init_sample = (state_shift_a, state_wkv, state_conv_b, state_mlstm_c, state_mlstm_n, state_mlstm_m,
                   cache_sb_k, cache_sb_v)
    y_prompt, sp = _run_group(x_prompt, p_prompt, init_prompt, ln_in_g, ln_in_b, layers)
    y_sample, ss = _run_group(x_sample, p_sample, init_sample, ln_in_g, ln_in_b, layers)
    return (y_prompt, y_sample, sp[0], ss[0], sp[1], ss[1], sp[2], ss[2], sp[3], ss[3],
            sp[4], ss[4], sp[5], ss[5], sp[6], ss[6], sp[7], ss[7])
```

```python
import functools

import jax
import jax.numpy as jnp
from jax import lax
from jax.experimental import pallas as pl
from jax.experimental.pallas import tpu as pltpu

F32 = jnp.float32
BF16 = jnp.bfloat16

D_MODEL = 1024
DEPTH = 4
CHUNK = 64
D_PLE = 256
A_HEADS = 8
A_HEAD_DIM = 64
A_WIDTH = A_HEADS * A_HEAD_DIM
A_LORA = 64
A_SHIFT = 3 * A_WIDTH + 2 * A_LORA
B_HEADS = 4
B_HEAD_DIM = 128
B_WIDTH = B_HEADS * B_HEAD_DIM
B_CONV = 4
C_HEADS = 8
C_HEAD_DIM = 64
C_WIDTH = C_HEADS * C_HEAD_DIM
N_BRANCH = 3
DEEPNORM_ALPHA = (2 * DEPTH) ** 0.25
LN_EPS = 1e-5
GN_EPS_A = 64e-5
HN_EPS = 1e-6

LANES = 128
SUBLANES = 8
VMEM_LIMIT_BYTES = 56 * 1024 * 1024

MATMUL_TILE_ROWS = 512
RWKV_TILE_ROWS = 2048
RWKV_GROUP_CHUNKS = 2

_OFF_A_COLS = 0
_OFF_A_Z = _OFF_A_COLS + A_SHIFT
_OFF_B_QK = _OFF_A_Z + A_WIDTH
_OFF_B_V = _OFF_B_QK + 2 * B_WIDTH
_OFF_B_I = _OFF_B_V + B_WIDTH
_OFF_B_F = _OFF_B_I + B_HEADS
_OFF_B_O = _OFF_B_F + B_HEADS
_OFF_B_Z = _OFF_B_O + B_WIDTH
_OFF_C_Q = _OFF_B_Z + B_WIDTH
_OFF_C_K = _OFF_C_Q + C_WIDTH
_OFF_C_V = _OFF_C_K + C_WIDTH
_OFF_C_Z = _OFF_C_V + C_WIDTH
_OFF_GATES = _OFF_C_Z + C_WIDTH
_N_IN = _OFF_GATES + N_BRANCH * D_MODEL

_MIX_WIDTHS = (A_SHIFT, 2 * B_WIDTH, B_WIDTH, B_WIDTH, C_WIDTH, LANES)

NN = ((1,), (0,))
NT = ((1,), (1,))
TN = ((0,), (0,))


def _dg(a, b, dims=NN):
    return lax.dot_general(a, b, (dims, ((), ())), preferred_element_type=F32)


def _mm(a, b, dims=NN):
    return _dg(a.astype(BF16), b.astype(BF16), dims)


def _split3(x):
    h1 = x.astype(BF16)
    r1 = x - h1.astype(F32)
    h2 = r1.astype(BF16)
    h3 = (r1 - h2.astype(F32)).astype(BF16)
    return h1, h2, h3


def _mm_exact_rhs(lhs_bf16, x):
    h1, h2, h3 = _split3(x)
    return _dg(lhs_bf16, h1) + (_dg(lhs_bf16, h2) + _dg(lhs_bf16, h3))


def _mm_exact_lhs(x, rhs_bf16):
    h1, h2, h3 = _split3(x)
    return _dg(h1, rhs_bf16) + (_dg(h2, rhs_bf16) + _dg(h3, rhs_bf16))


def _softplus(x):
    return jnp.maximum(x, 0.0) + jnp.log(1.0 + jnp.exp(-jnp.abs(x)))


def _sigmoid(x):
    return jax.nn.sigmoid(x)


def _silu(x):
    return x * jax.nn.sigmoid(x)


def _norm_rows(x, eps):
    mu = jnp.mean(x, axis=-1, keepdims=True)
    xc = x - mu
    var = jnp.mean(xc * xc, axis=-1, keepdims=True)
    return xc * lax.rsqrt(var + eps)


def _tile(n, pref):
    if n <= pref:
        return n
    t = pref - pref % SUBLANES
    while n % t:
        t -= SUBLANES
    return t


def _params(*sem):
    return pltpu.CompilerParams(dimension_semantics=sem, vmem_limit_bytes=VMEM_LIMIT_BYTES)


def _const_spec(shape):
    nd = len(shape)
    return pl.BlockSpec(shape, lambda *_: (0,) * nd, pipeline_mode=pl.Buffered(1))


def _inproj_kernel(x_ref, w_ref, b_ref, wkv_ref, bkv_ref, *refs, pre_norm):
    n_out = len(_MIX_WIDTHS) + 2
    out_refs = refs[-n_out:]
    x = x_ref[...]
    if pre_norm:
        x = _norm_rows(x, LN_EPS) * refs[0][...] + refs[1][...]
    xb = x.astype(BF16)
    off = 0
    for o_ref, width in zip(out_refs, _MIX_WIDTHS):
        o_ref[...] = _dg(xb, w_ref[:, off:off + width]) + b_ref[:, off:off + width]
        off += width
    kvt = _dg(wkv_ref[...], xb, NT) + bkv_ref[...]
    out_refs[-2][0, 0] = kvt[:C_WIDTH]
    out_refs[-1][0, 0] = kvt[C_WIDTH:]


def _inproj_call(x, lw, layer, bsz, t, kt_stack=None, vt_stack=None, pre_norm=None):
    n, d = x.shape
    tm = _tile(t, MATMUL_TILE_ROWS)
    per_b = t // tm
    wtot = sum(_MIX_WIDTHS)
    stack = jax.ShapeDtypeStruct((DEPTH, bsz, C_WIDTH, t), F32)
    stack_spec = pl.BlockSpec((1, 1, C_WIDTH, tm), lambda i: (layer, i // per_b, 0, i % per_b))
    in_specs = [pl.BlockSpec((tm, d), lambda i: (i, 0)), _const_spec((d, wtot)), _const_spec((1, wtot)),
                _const_spec((2 * C_WIDTH, d)), _const_spec((2 * C_WIDTH, 1))]
    args = [x, lw["w_mix"], lw["b_mix"], lw["w_kvt"], lw["b_kvt"]]
    if pre_norm is not None:
        in_specs += [_const_spec((1, d))] * 2
        args += [a.reshape(1, d) for a in pre_norm]
    aliases = {}
    if kt_stack is not None:
        n_mix = len(_MIX_WIDTHS)
        aliases = {len(args): n_mix, len(args) + 1: n_mix + 1}
        in_specs += [pl.BlockSpec(memory_space=pl.ANY)] * 2
        args += [kt_stack, vt_stack]
    return pl.pallas_call(
        functools.partial(_inproj_kernel, pre_norm=pre_norm is not None),
        out_shape=tuple(jax.ShapeDtypeStruct((n, wd), F32) for wd in _MIX_WIDTHS) + (stack, stack),
        grid=(n // tm,),
        in_specs=in_specs,
        out_specs=tuple(pl.BlockSpec((tm, wd), lambda i: (i, 0)) for wd in _MIX_WIDTHS) + (stack_spec, stack_spec),
        input_output_aliases=aliases,
        compiler_params=_params("parallel"),
        name="inproj",
    )(*args)


RWKV_RING = 3


def _round_robin(*steps):
    steps = list(steps)
    while steps:
        for s in list(steps):
            try:
                next(s)
            except StopIteration:
                steps.remove(s)


def _rwkv_pipe_kernel(cols_ref, last_ref, s0_ref, mu_ref, w0_ref, wup_ref, a0_ref, aup_ref, kk_ref, ka_ref, hsum_ref,
                      rk_ref, gng_ref, gnb_ref, y_o, last_o, s_o,
                      prev_scr, s_scr, at_r, rt_r, bt_r, kt_r, btw_r, ktw_r, v_r, bon_r, w_r, u0_r, x1_r, mrb_r, wl_r):
    t = pl.program_id(1)
    L = CHUNK
    tt = cols_ref.shape[1]
    gr = min(tt, RWKV_GROUP_CHUNKS * L)
    cpg = gr // L
    ng = tt // gr
    n_pairs = A_HEADS // 2

    @pl.when(t == 0)
    def _():
        prev_scr[...] = last_ref[0]
        s_scr[...] = s0_ref[0]

    ri = lax.broadcasted_iota(jnp.int32, (L, L), 0)
    ci = lax.broadcasted_iota(jnp.int32, (L, L), 1)
    strict = ri > ci
    incl = ri >= ci
    tri = jnp.where(incl, 1.0, 0.0).astype(BF16)
    eye = jnp.where(ri == ci, 1.0, 0.0)
    lane = lax.broadcasted_iota(jnp.int32, (1, LANES), 1)
    head_masks = (jnp.where(lane < A_HEAD_DIM, 1.0, 0.0), jnp.where(lane >= A_HEAD_DIM, 1.0, 0.0))
    inv_d = 1.0 / A_HEAD_DIM
    pair = lambda p: slice(p * LANES, (p + 1) * LANES)
    chunk = lambda c: slice(c * L, (c + 1) * L)
    heads = [(p, hm) for p in range(n_pairs) for hm in head_masks]

    def prologue(g):
        gp = g
        slot = lax.rem(gp, RWKV_RING)
        r0 = pl.multiple_of(gp * gr, gr)
        u = cols_ref[0, pl.ds(r0, gr), :]
        above = cols_ref[0, pl.ds(pl.multiple_of(jnp.maximum(r0 - SUBLANES, 0), SUBLANES), SUBLANES), :]
        before = jnp.where(gp == 0, prev_scr[...], above[SUBLANES - 1:SUBLANES, :])
        row = lax.broadcasted_iota(jnp.int32, u.shape, 0)
        prev = jnp.where(row == 0, before, pltpu.roll(u, 1, 0))
        xs = u + (prev - u) * mu_ref[...]
        yield
        r = xs[:, 0:A_WIDTH]
        k = xs[:, A_WIDTH:2 * A_WIDTH]
        v = xs[:, 2 * A_WIDTH:3 * A_WIDTH]
        lora = xs[:, 3 * A_WIDTH:]
        w_log = -_softplus(-(w0_ref[...] + _mm(jnp.tanh(lora), wup_ref[...]))) - 0.5
        a = _sigmoid(a0_ref[...] + _mm(lora, aup_ref[...]))
        yield
        kk = k * kk_ref[...]
        ss = _mm_exact_lhs(kk * kk, hsum_ref[...])
        kk = kk / jnp.maximum(jnp.sqrt(ss), 1e-12)
        yield
        kh = k * (1.0 + (a - 1.0) * ka_ref[...])
        lw = -jnp.exp(w_log)
        cin = jnp.concatenate([_mm_exact_rhs(tri, lw[chunk(c), :]) for c in range(cpg)], axis=0)
        yield
        ein = jnp.exp(cin)
        einv = jnp.exp(-cin)
        bt = kk * a * einv
        kt = kh * einv
        at_r[slot] = -kk * jnp.exp(cin - lw)
        rt_r[slot] = r * ein
        yield
        bt_r[slot] = bt
        kt_r[slot] = kt
        v_r[slot] = v
        bon_r[slot] = r * kh * rk_ref[...]
        yield
        for c in range(cpg):
            wl = ein[(c + 1) * L - 1:(c + 1) * L, :]
            wl_r[slot, c] = wl
            btw_r[slot, chunk(c), :] = bt[chunk(c), :] * wl
            ktw_r[slot, chunk(c), :] = kt[chunk(c), :] * wl
        yield

    def state_free(g):
        slot = lax.rem(g - 1, RWKV_RING)
        items = [(c, p, hm) for c in range(cpg) for p, hm in heads]
        load = lambda ring: [ring[slot, chunk(c), pair(p)] * hm for c, p, hm in items]
        at_h, rt_h, v_h = load(at_r), load(rt_r), load(v_r)
        lhs = [jnp.concatenate([x, y], axis=0).astype(BF16) for x, y in zip(at_h, rt_h)]
        yield
        sc_b = [_dg(x, bt_r[slot, chunk(c), pair(p)].astype(BF16), NT) for x, (c, p, _) in zip(lhs, items)]
        sc_k = [_dg(x, kt_r[slot, chunk(c), pair(p)].astype(BF16), NT) for x, (c, p, _) in zip(lhs, items)]
        yield
        a_ab = [jnp.where(strict, s[:L], 0.0) for s in sc_b]
        a_ak = [jnp.where(strict, s[:L], 0.0) for s in sc_k]
        m_rb = [jnp.where(incl, s[L:], 0.0) for s in sc_b]
        m_rk = [jnp.where(incl, s[L:], 0.0) for s in sc_k]
        tm = [eye + x for x in a_ab]
        ap = a_ab
        for _ in range(5):
            ap = [_mm(x, x) for x in ap]
            yield
            tm = [x + _mm(x, y) for x, y in zip(tm, ap)]
            yield
        x1 = [_mm(jnp.concatenate([x, y], axis=0), vh) for x, y, vh in zip(a_ak, m_rk, v_h)]
        yield
        wu = [_mm(x, jnp.concatenate([y, z[:L]], axis=1)) for x, y, z in zip(tm, at_h, x1)]
        yield
        for n in range(0, len(items), 2):
            c, p, _ = items[n]
            w_r[slot, chunk(c), pair(p)] = wu[n][:, :LANES] + wu[n + 1][:, :LANES]
            u0_r[slot, chunk(c), pair(p)] = wu[n][:, LANES:] + wu[n + 1][:, LANES:]
            x1_r[slot, chunk(c), pair(p)] = x1[n][L:] + x1[n + 1][L:]
            mrb_r[slot, c, 2 * p] = m_rb[n]
            mrb_r[slot, c, 2 * p + 1] = m_rb[n + 1]
        yield

    def state_step(g):
        g2 = g - 2
        slot = lax.rem(g2, RWKV_RING)
        hm0, hm1 = head_masks
        y_pairs = [[] for _ in range(n_pairs)]
        for c in range(cpg):
            s_pairs = [s_scr[p] for p in range(n_pairs)]
            s_pairs_b = [s.astype(BF16) for s in s_pairs]
            w_h = [w_r[slot, chunk(c), pair(p)] * hm for p, hm in heads]
            rt_h = [rt_r[slot, chunk(c), pair(p)] * hm for p, hm in heads]
            st = [_dg(jnp.concatenate([x, y], axis=0).astype(BF16), s_pairs_b[p], NT)
                  for x, y, (p, _) in zip(w_h, rt_h, heads)]
            yield
            u_h = [u0_r[slot, chunk(c), pair(p)] * hm + s[:L] for s, (p, hm) in zip(st, heads)]
            y_h = [s[L:] + _mm(mrb_r[slot, c, n], x) for n, (s, x) in enumerate(zip(st, u_h))]
            upd = []
            for p in range(n_pairs):
                vp, bp, kp = v_r[slot, chunk(c), pair(p)], btw_r[slot, chunk(c), pair(p)], ktw_r[slot, chunk(c), pair(p)]
                upd.append(_mm(jnp.concatenate([u_h[2 * p], vp * hm0, u_h[2 * p + 1], vp * hm1], axis=0),
                               jnp.concatenate([bp * hm0, kp * hm0, bp * hm1, kp * hm1], axis=0), TN))
            yield
            wl = wl_r[slot, c]
            for p in range(n_pairs):
                s_scr[p] = s_pairs[p] * wl[:, pair(p)] + upd[p]
                y_pairs[p].append(y_h[2 * p] + y_h[2 * p + 1] + x1_r[slot, chunk(c), pair(p)])
            yield
        rows = pl.ds(pl.multiple_of(g2 * gr, gr), gr)
        for p in range(n_pairs):
            y_pair = jnp.concatenate(y_pairs[p], axis=0)
            normed = jnp.zeros(y_pair.shape, F32)
            bonus = jnp.zeros(y_pair.shape, F32)
            for hm in head_masks:
                mu = jnp.sum(y_pair * hm, axis=-1, keepdims=True) * inv_d
                yc = (y_pair - mu) * hm
                var = jnp.sum(yc * yc, axis=-1, keepdims=True) * inv_d
                normed = normed + yc * lax.rsqrt(var + GN_EPS_A)
                bonus = bonus + (jnp.sum(bon_r[slot, :, pair(p)] * hm, axis=-1, keepdims=True)
                                 * (v_r[slot, :, pair(p)] * hm))
            y_o[0, rows, pair(p)] = (normed * gng_ref[:, pair(p)] + gnb_ref[:, pair(p)]) + bonus
            yield

    def trip(g, carry):
        _round_robin(prologue(g), state_free(g), state_step(g))
        return carry

    at = jnp.int32
    if ng == 1:
        for g, step in enumerate((prologue, state_free, state_step)):
            _round_robin(step(at(g)))
    else:
        _round_robin(prologue(at(0)))
        _round_robin(prologue(at(1)), state_free(at(1)))
        lax.fori_loop(2, ng, trip, 0)
        _round_robin(state_free(at(ng)), state_step(at(ng)))
        _round_robin(state_step(at(ng + 1)))
    prev_scr[...] = cols_ref[0, tt - 1:tt, :]
    last_o[0] = cols_ref[0, tt - 1:tt, :]
    s_o[0] = s_scr[...]


def _rwkv_call(cols, last, s0_pairs, lw):
    bsz, t, _ = cols.shape
    tt = _tile(t, RWKV_TILE_ROWS)
    gr = min(tt, RWKV_GROUP_CHUNKS * CHUNK)
    blk = lambda w: pl.BlockSpec((1, tt, w), lambda b, i: (b, i, 0))
    row = pl.BlockSpec((1, 1, A_SHIFT), lambda b, i: (b, 0, 0))
    st = pl.BlockSpec((1, A_HEADS // 2, LANES, LANES), lambda b, i: (b, 0, 0, 0))
    ring = pltpu.VMEM((RWKV_RING, gr, A_WIDTH), F32)
    return pl.pallas_call(
        _rwkv_pipe_kernel,
        out_shape=(jax.ShapeDtypeStruct((bsz, t, A_WIDTH), F32),
                   jax.ShapeDtypeStruct((bsz, 1, A_SHIFT), F32),
                   jax.ShapeDtypeStruct((bsz, A_HEADS // 2, LANES, LANES), F32)),
        grid=(bsz, t // tt),
        in_specs=[blk(A_SHIFT), row, st, _const_spec((1, A_SHIFT)), _const_spec((1, A_WIDTH)),
                  _const_spec((2 * A_LORA, A_WIDTH)), _const_spec((1, A_WIDTH)), _const_spec((2 * A_LORA, A_WIDTH)),
                  _const_spec((1, A_WIDTH)), _const_spec((1, A_WIDTH)), _const_spec((A_WIDTH, A_WIDTH)),
                  _const_spec((1, A_WIDTH)), _const_spec((1, A_WIDTH)), _const_spec((1, A_WIDTH))],
        out_specs=(blk(A_WIDTH), row, st),
        scratch_shapes=[pltpu.VMEM((1, A_SHIFT), F32), pltpu.VMEM((A_HEADS // 2, LANES, LANES), F32)]
        + [ring] * 11
        + [pltpu.VMEM((RWKV_RING, gr // CHUNK, A_HEADS, CHUNK, CHUNK), F32),
           pltpu.VMEM((RWKV_RING, gr // CHUNK, 1, A_WIDTH), F32)],
        compiler_params=_params("parallel", "arbitrary"),
        name="rwkv",
    )(cols, last.reshape(bsz, 1, A_SHIFT), s0_pairs, lw["mu_a"], lw["w0_a"], lw["wup_pad"], lw["a0_a"],
      lw["aup_pad"], lw["k_k"], lw["k_a"], lw["head_sum"], lw["r_k"], lw["gn_a_g"], lw["gn_a_b"])


def _wkv_to_pairs(s):
    bsz = s.shape[0]
    s = s.reshape(bsz, A_HEADS // 2, 2, A_HEAD_DIM, A_HEAD_DIM)
    z = jnp.zeros_like(s[:, :, 0])
    top = jnp.concatenate([s[:, :, 0], z], axis=-1)
    bot = jnp.concatenate([z, s[:, :, 1]], axis=-1)
    return jnp.concatenate([top, bot], axis=-2)


def _wkv_from_pairs(sp):
    bsz = sp.shape[0]
    d = A_HEAD_DIM
    s = jnp.stack([sp[:, :, :d, :d], sp[:, :, d:, d:]], axis=2)
    return s.reshape(bsz, A_HEADS, d, d)


def _mlstm_steps(t, qk_ref, v_ref, if_ref, o_ref, buf0_ref, c0_ref, n0_ref, m0_ref, cw_ref, cb_ref, hng_ref,
                 y_o, buf_o, c_o, n_o, m_o, ext_scr, c_scr, n_scr, m_scr):
    tt = qk_ref.shape[1]
    L = min(tt, LANES)
    assert tt == L
    pad = SUBLANES

    @pl.when(t == 0)
    def _():
        ext_scr[0:pad, :] = buf0_ref[0]
        c_scr[...] = c0_ref[0]
        n_scr[...] = n0_ref[0]
        m_scr[...] = m0_ref[0]

    ext_scr[pad:pad + tt, :] = qk_ref[0]
    ri = lax.broadcasted_iota(jnp.int32, (L, L), 0)
    ci = lax.broadcasted_iota(jnp.int32, (L, L), 1)
    incl = ri >= ci
    eye = ri == ci
    tri = jnp.where(incl, 1.0, 0.0).astype(BF16)
    k_scale = B_HEAD_DIM ** -0.5

    hs = range(B_HEADS)
    sls = [slice(h * B_HEAD_DIM, (h + 1) * B_HEAD_DIM) for h in hs]

    x = ext_scr[...]
    conv = cb_ref[...] + x[pad:, :] * cw_ref[B_CONV - 1:B_CONV, :]
    for tap in range(B_CONV - 1):
        conv = conv + pltpu.roll(x, B_CONV - 1 - tap, 0)[pad:, :] * cw_ref[tap:tap + 1, :]
    yield
    qk = _silu(conv)
    yield
    ifg = if_ref[0]
    bcum = _mm_exact_rhs(tri, -_softplus(-ifg))
    yield
    q_h = [qk[:, sl] for sl in sls]
    k_h = [qk[:, B_WIDTH + h * B_HEAD_DIM:B_WIDTH + (h + 1) * B_HEAD_DIM] * k_scale for h in hs]
    v_h = [v_ref[0, :, sl] for sl in sls]
    icol = [ifg[:, h:h + 1] for h in hs]
    bcol = [bcum[:, B_HEADS + h:B_HEADS + h + 1] for h in hs]
    qk_t = [_mm(q, k, NT) for q, k in zip(q_h, k_h)]
    yield
    dcol = [i_ - b for i_, b in zip(icol, bcol)]
    drow = [jnp.sum(jnp.where(eye, d, 0.0), axis=0, keepdims=True) for d in dcol]
    dmat = [jnp.where(incl, b + d, -jnp.inf) for b, d in zip(bcol, drow)]
    yield
    dmax = [jnp.max(d, axis=-1, keepdims=True) for d in dmat]
    yield
    c_h = [c_scr[h] for h in hs]
    n_h = [n_scr[h] for h in hs]
    m_h = [m_scr[h][:, 0:1] for h in hs]
    qc = [_mm(q, c, NT) for q, c in zip(q_h, c_h)]
    yield
    inter = [b + m for b, m in zip(bcol, m_h)]
    m_t = [jnp.maximum(x_, d) for x_, d in zip(inter, dmax)]
    wmat = [jnp.exp(d - m) * s for d, m, s in zip(dmat, m_t, qk_t)]
    yield
    scale = [jnp.exp(x_ - m) for x_, m in zip(inter, m_t)]
    wv = [_mm(w, v) for w, v in zip(wmat, v_h)]
    yield
    m_new = [m[L - 1:L, :] for m in m_t]
    gcol = [jnp.exp(b[L - 1:L, :] + d - m) for b, d, m in zip(bcol, dcol, m_new)]
    c_sc = [jnp.exp(x_[L - 1:L, :] - m) for x_, m in zip(inter, m_new)]
    gvk = [_mm(g * v, k, TN) for g, v, k in zip(gcol, v_h, k_h)]
    yield
    for h in hs:
        num = wv[h] + scale[h] * qc[h]
        den = (jnp.sum(wmat[h], axis=-1, keepdims=True)
               + scale[h] * jnp.sum(q_h[h] * n_h[h], axis=-1, keepdims=True))
        hid = num / jnp.maximum(jnp.abs(den), jnp.exp(-m_t[h]))
        y_o[0, :, sls[h]] = _sigmoid(o_ref[0, :, sls[h]]) * (_norm_rows(hid, HN_EPS) * hng_ref[:, sls[h]])
        yield
    for h in hs:
        c_scr[h] = c_sc[h] * c_h[h] + gvk[h]
        n_scr[h] = c_sc[h] * n_h[h] + jnp.sum(gcol[h] * k_h[h], axis=0, keepdims=True)
        m_scr[h] = jnp.broadcast_to(m_new[h], (1, LANES))
    last = ext_scr[tt:tt + pad, :]
    ext_scr[0:pad, :] = last
    buf_o[0] = last
    c_o[0] = c_scr[...]
    n_o[0] = n_scr[...]
    m_o[0] = m_scr[...]


SB_SKIP_LOG = -105.0


_N_MLSTM_IN, _N_MLSTM_OUT, _N_SB_SCRATCH = 11, 5, 4


def _sb_mlstm_kernel(*refs, n_past_blocks):
    n_att = 5 if n_past_blocks else 3
    att_in, refs = refs[:n_att], refs[n_att:]
    ml_in, o_ref, refs = refs[:_N_MLSTM_IN], refs[_N_MLSTM_IN], refs[_N_MLSTM_IN + 1:]
    ml_out, refs = refs[:_N_MLSTM_OUT], refs[_N_MLSTM_OUT:]
    (q2_scr, c_scr, acc_scr, pad_scr), ml_scr = refs[:_N_SB_SCRATCH], refs[_N_SB_SCRATCH:]
    if n_past_blocks:
        q_ref, k_ref, v_ref, kp_ref, vp_ref = att_in
    else:
        q_ref, k_ref, v_ref = att_in
    mlstm = _mlstm_steps(pl.program_id(1), *ml_in, *ml_out, *ml_scr)
    qi = pl.program_id(1)
    qb = q_ref.shape[1]
    kb = LANES
    n_pairs = C_HEADS // 2
    n_rows = C_HEADS * qb

    rj = lax.broadcasted_iota(jnp.int32, (2 * kb, 2 * kb), 0) & (kb - 1)
    cj = lax.broadcasted_iota(jnp.int32, (2 * kb, 2 * kb), 1)
    cum2 = jnp.where((cj >= kb) | (rj > cj), 1.0, 0.0).astype(BF16)
    qpos = lax.broadcasted_iota(jnp.int32, (n_rows, kb), 0) & (qb - 1)
    kpos = lax.broadcasted_iota(jnp.int32, (n_rows, kb), 1)
    diag_mask = kpos < qpos
    lane = lax.broadcasted_iota(jnp.int32, (1, LANES), 1)
    lo_lanes = lane < C_HEAD_DIM
    q = q_ref[0] * (C_HEAD_DIM ** -0.5)
    for p in range(n_pairs):
        qs = q[:, p * LANES:(p + 1) * LANES]
        q2_scr[p] = jnp.concatenate([jnp.where(lo_lanes, qs, 0.0), jnp.where(lo_lanes, 0.0, qs)], axis=0).astype(BF16)

    def load_t(ref, slot, p, keys):
        x = ref[0, 0, p * LANES:(p + 1) * LANES, keys]
        if x.shape[1] == kb:
            return x.astype(BF16)
        pad_scr[slot] = jnp.zeros((LANES, kb), F32)
        pad_scr[slot, :, 0:x.shape[1]] = x
        return pad_scr[slot].astype(BF16)

    def half_block(pairs, kr, vr, keys, first, c_max):
        rows = slice(pairs[0] * 2 * qb, (pairs[-1] + 1) * 2 * qb)
        mask = diag_mask[:rows.stop - rows.start]
        z = jnp.concatenate([_dg(q2_scr[p], load_t(kr, 2 * p, p, keys)) for p in pairs], axis=0)
        yield
        sp = _softplus(z)
        lk = jnp.where(mask, -sp, 0.0) if first else -sp
        h1 = lk.astype(BF16)
        h2 = (lk - h1.astype(F32)).astype(BF16)
        yield
        cs = _dg(jnp.concatenate([h1, h2], axis=1), cum2)
        yield
        if first:
            a = jnp.where(mask, jnp.exp((z - sp) + cs[:, :kb]), 0.0)
            c_new = cs[:, kb:]
        else:
            c_old = c_scr[rows]
            a = jnp.exp((z - sp) + cs[:, :kb] + c_old)
            c_new = c_old + cs[:, kb:]
        c_scr[rows] = c_new
        c_max.append(jnp.max(c_new))
        ab = a.astype(BF16)
        yield
        for n, p in enumerate(pairs):
            rs = slice(p * 2 * qb, (p + 1) * 2 * qb)
            upd = _dg(ab[n * 2 * qb:(n + 1) * 2 * qb], load_t(vr, 2 * p + 1, p, keys), NT)
            acc_scr[rs] = upd if first else acc_scr[rs] + upd
        yield

    def block(kr, vr, keys, first, others=()):
        c_max = []
        lead = half_block(tuple(range(n_pairs // 2)), kr, vr, keys, first, c_max)
        lag = half_block(tuple(range(n_pairs // 2, n_pairs)), kr, vr, keys, first, c_max)
        next(lead)
        _round_robin(lead, lag, *others)
        return jnp.maximum(*c_max) > SB_SKIP_LOG

    own_keys = pl.ds(pl.multiple_of(qi * qb, qb), qb) if qb == kb else pl.ds(0, qb)
    go = block(k_ref, v_ref, own_keys, True, (mlstm,))

    def sweep(kr, vr, n_blocks, go):
        def cond(carry):
            j, more = carry
            return jnp.logical_and(j < n_blocks, more)

        def body(carry):
            j, _ = carry
            keys = pl.ds(pl.multiple_of((n_blocks - 1 - j) * kb, kb), kb)
            return j + 1, block(kr, vr, keys, False)

        return lax.while_loop(cond, body, (jnp.int32(0), go))[1]

    if qb == kb:
        go = sweep(k_ref, v_ref, qi, go)
    if n_past_blocks:
        go = sweep(kp_ref, vp_ref, n_past_blocks, go)

    for p in range(n_pairs):
        r0 = p * 2 * qb
        o_ref[0, :, p * LANES:(p + 1) * LANES] = jnp.where(lo_lanes, acc_scr[r0:r0 + qb], acc_scr[r0 + qb:r0 + 2 * qb])


def _sb_mlstm_call(q, kt, vt, layer, kt_past, vt_past, qk, v, ifg, o, buf0, c0, n0, m0, lw):
    bsz, t, _ = q.shape
    qb = min(LANES, t)
    assert t % qb == 0 and (qb == LANES or t == qb)
    n_past = 0 if kt_past is None else kt_past.shape[-1]
    assert n_past % LANES == 0
    blk = lambda w: pl.BlockSpec((1, qb, w), lambda b, i: (b, i, 0))
    full = lambda n: pl.BlockSpec((1, 1, C_WIDTH, n), lambda b, i: (layer, b, 0, 0))
    fix = lambda *s: pl.BlockSpec((1,) + s, lambda b, i: (b,) + (0,) * len(s))
    in_specs = [blk(C_WIDTH), full(t), full(t)]
    args = [q, kt, vt]
    if n_past:
        in_specs += [full(n_past), full(n_past)]
        args += [kt_past, vt_past]
    in_specs += [blk(2 * B_WIDTH), blk(B_WIDTH), blk(LANES), blk(B_WIDTH),
                 fix(SUBLANES, 2 * B_WIDTH), fix(B_HEADS, B_HEAD_DIM, B_HEAD_DIM), fix(B_HEADS, 1, B_HEAD_DIM),
                 fix(B_HEADS, 1, LANES),
                 _const_spec((B_CONV, 2 * B_WIDTH)), _const_spec((1, 2 * B_WIDTH)), _const_spec((1, B_WIDTH))]
    args += [qk, v, ifg, o, buf0, c0, n0, m0, lw["conv_b_w"], lw["conv_b_b"], lw["hn_b_g"]]
    assert len(in_specs) - (5 if n_past else 3) == _N_MLSTM_IN
    return pl.pallas_call(
        functools.partial(_sb_mlstm_kernel, n_past_blocks=n_past // LANES),
        out_shape=(jax.ShapeDtypeStruct((bsz, t, C_WIDTH), F32),
                   jax.ShapeDtypeStruct((bsz, t, B_WIDTH), F32),
                   jax.ShapeDtypeStruct((bsz, SUBLANES, 2 * B_WIDTH), F32),
                   jax.ShapeDtypeStruct((bsz, B_HEADS, B_HEAD_DIM, B_HEAD_DIM), F32),
                   jax.ShapeDtypeStruct((bsz, B_HEADS, 1, B_HEAD_DIM), F32),
                   jax.ShapeDtypeStruct((bsz, B_HEADS, 1, LANES), F32)),
        grid=(bsz, t // qb),
        in_specs=in_specs,
        out_specs=(blk(C_WIDTH), blk(B_WIDTH), fix(SUBLANES, 2 * B_WIDTH), fix(B_HEADS, B_HEAD_DIM, B_HEAD_DIM),
                   fix(B_HEADS, 1, B_HEAD_DIM), fix(B_HEADS, 1, LANES)),
        scratch_shapes=[pltpu.VMEM((C_HEADS // 2, 2 * qb, LANES), BF16),
                        pltpu.VMEM((C_HEADS * qb, LANES), F32), pltpu.VMEM((C_HEADS * qb, LANES), F32),
                        pltpu.VMEM((C_HEADS, LANES, LANES), F32),
                        pltpu.VMEM((qb + SUBLANES, 2 * B_WIDTH), F32),
                        pltpu.VMEM((B_HEADS, B_HEAD_DIM, B_HEAD_DIM), F32),
                        pltpu.VMEM((B_HEADS, 1, B_HEAD_DIM), F32),
                        pltpu.VMEM((B_HEADS, 1, LANES), F32)],
        compiler_params=_params("parallel", "arbitrary"),
        name="sb_mlstm",
    )(*args)


def _merge_kernel(x_ref, ya_ref, yb_ref, yc_ref, p_ref, wz_ref, bz_ref, wg_ref, bg_ref, wbr_ref, wout_ref,
                  lng_ref, lnb_ref, wple_ref, wpg_ref, *refs, pre_norm):
    o_ref = refs[-1]
    width = A_WIDTH
    half_rows = x_ref.shape[0] // 2

    def half(r):
        rows = slice(r * half_rows, (r + 1) * half_rows)
        x = x_ref[rows, :]
        if pre_norm:
            x = _norm_rows(x, LN_EPS) * refs[0][...] + refs[1][...]
        xb = x.astype(BF16)
        mix_pre = jnp.zeros(x.shape, F32)
        for n, y_ref in enumerate((ya_ref, yb_ref, yc_ref)):
            z = _dg(xb, wz_ref[:, n * width:(n + 1) * width]) + bz_ref[:, n * width:(n + 1) * width]
            yield
            ys = y_ref[rows, :] * _silu(z)
            br = _mm(ys, wbr_ref[n])
            yield
            gate = _dg(xb, wg_ref[:, n * D_MODEL:(n + 1) * D_MODEL]) + bg_ref[:, n * D_MODEL:(n + 1) * D_MODEL]
            yield
            mix_pre = mix_pre + _sigmoid(gate) * br
        mix = _mm(mix_pre, wout_ref[...])
        yield
        x1 = _norm_rows(DEEPNORM_ALPHA * x + mix, LN_EPS) * lng_ref[...] + lnb_ref[...]
        yield
        o_ref[rows, :] = x1 + _mm(p_ref[0, rows, :], wple_ref[...]) * _sigmoid(_mm(x1, wpg_ref[...]))
        yield

    lead, lag = half(0), half(1)
    next(lead)
    _round_robin(lead, lag)


def _merge_call(x, ya, yb, yc, p_all, layer, lw, pre_norm=None):
    n, d = x.shape
    tm = _tile(n, MATMUL_TILE_ROWS)
    rows = lambda w: pl.BlockSpec((tm, w), lambda i: (i, 0))
    in_specs = [rows(d), rows(A_WIDTH), rows(B_WIDTH), rows(C_WIDTH),
                pl.BlockSpec((1, tm, D_PLE), lambda i: (layer, i, 0)),
                _const_spec((d, N_BRANCH * A_WIDTH)), _const_spec((1, N_BRANCH * A_WIDTH)),
                _const_spec((d, N_BRANCH * d)), _const_spec((1, N_BRANCH * d)),
                _const_spec((N_BRANCH, A_WIDTH, d)), _const_spec((d, d)),
                _const_spec((1, d)), _const_spec((1, d)), _const_spec((D_PLE, d)), _const_spec((d, d))]
    args = [x, ya, yb, yc, p_all, lw["w_z"], lw["b_z"], lw["w_gates"], lw["b_gates"], lw["w_branch"], lw["w_out"],
            lw["ln_g"], lw["ln_b"], lw["w_ple"], lw["w_ple_gate"]]
    if pre_norm is not None:
        in_specs += [_const_spec((1, d))] * 2
        args += [a.reshape(1, d) for a in pre_norm]
    return pl.pallas_call(
        functools.partial(_merge_kernel, pre_norm=pre_norm is not None),
        out_shape=jax.ShapeDtypeStruct((n, d), F32),
        grid=(n // tm,),
        in_specs=in_specs,
        out_specs=rows(d),
        compiler_params=_params("parallel"),
        name="merge",
    )(*args)


def _cols(w, *ranges):
    return jnp.concatenate([w[..., a:b] for a, b in ranges], axis=-1)


def _layer_params(i, w_in, b_in, prm):
    w = w_in[i]
    b = b_in[i]
    mix_ranges = ((_OFF_A_COLS, _OFF_A_Z), (_OFF_B_QK, _OFF_B_V), (_OFF_B_V, _OFF_B_I), (_OFF_B_O, _OFF_B_Z),
                  (_OFF_C_Q, _OFF_C_K), (_OFF_B_I, _OFF_B_O))
    pad = LANES - 2 * B_HEADS
    w_mix = jnp.pad(_cols(w, *mix_ranges), ((0, 0), (0, pad)))
    b_mix = jnp.pad(_cols(b, *mix_ranges), ((0, pad),))
    z_ranges = ((_OFF_A_Z, _OFF_B_QK), (_OFF_B_Z, _OFF_C_Q), (_OFF_C_Z, _OFF_GATES))
    zeros_lora = jnp.zeros((A_LORA, A_WIDTH), F32)
    hid = jnp.arange(A_WIDTH) // A_HEAD_DIM
    row = lambda name: prm[name][i].reshape(1, -1)
    return {
        "w_mix": w_mix.astype(BF16), "b_mix": b_mix.reshape(1, -1),
        "w_kvt": w[:, _OFF_C_K:_OFF_C_Z].T.astype(BF16), "b_kvt": b[_OFF_C_K:_OFF_C_Z].reshape(-1, 1),
        "w_z": _cols(w, *z_ranges).astype(BF16), "b_z": _cols(b, *z_ranges).reshape(1, -1),
        "w_gates": w[:, _OFF_GATES:].astype(BF16), "b_gates": b[_OFF_GATES:].reshape(1, -1),
        "mu_a": row("mu_a"), "w0_a": row("w0_a"), "a0_a": row("a0_a"), "k_k": row("k_k"), "k_a": row("k_a"),
        "wup_pad": jnp.concatenate([prm["w_decay_up"][i], zeros_lora], axis=0).astype(BF16),
        "aup_pad": jnp.concatenate([zeros_lora, prm["w_iclr_up"][i]], axis=0).astype(BF16),
        "head_sum": (hid[:, None] == hid[None, :]).astype(BF16),
        "r_k": row("r_k"), "gn_a_g": row("gn_a_g"), "gn_a_b": row("gn_a_b"),
        "conv_b_w": prm["conv_b_w"][i], "conv_b_b": row("conv_b_b"), "hn_b_g": row("hn_b_g"),
        "w_branch": prm["w_branch"][i].astype(BF16), "w_out": prm["w_out"][i].astype(BF16),
        "ln_g": row("ln_g"), "ln_b": row("ln_b"),
        "w_ple": prm["w_ple"][i].astype(BF16), "w_ple_gate": prm["w_ple_gate"][i].astype(BF16),
    }


def _run_group(x, p, init, ln_in_g, ln_in_b, layers):
    bsz, t, d = x.shape
    n = bsz * t
    shift0, wkv0, conv0, c0, n0, m0, k_past, v_past = init
    xf = x.reshape(n, d)
    new = [[] for _ in range(6)]
    kt = vt = None
    channel_major = lambda a: jnp.transpose(a, (0, 1, 3, 4, 2)).reshape(a.shape[0], a.shape[1], C_WIDTH, a.shape[2])
    if k_past is not None:
        k_past, v_past = channel_major(k_past), channel_major(v_past)
    for i, lw in enumerate(layers):
        pre_norm = (ln_in_g, ln_in_b) if i == 0 else None
        a_cols, b_qk, b_v, b_o, c_q, b_if, kt, vt = _inproj_call(xf, lw, i, bsz, t, kt, vt, pre_norm)
        seq = lambda a: a.reshape(bsz, t, a.shape[-1])
        ya, last, s_pairs = _rwkv_call(seq(a_cols), shift0[i], _wkv_to_pairs(wkv0[i]), lw)
        buf0 = jnp.pad(conv0[i], ((0, 0), (SUBLANES - (B_CONV - 1), 0), (0, 0)))
        m0b = jnp.broadcast_to(m0[i][:, :, None, None], (bsz, B_HEADS, 1, LANES))
        yc, yb, buf, c1, n1, m1 = _sb_mlstm_call(seq(c_q), kt, vt, i, k_past, v_past,
                                                 seq(b_qk), seq(b_v), seq(b_if), seq(b_o), buf0, c0[i],
                                                 n0[i][:, :, None, :], m0b, lw)
        xf = _merge_call(xf, ya.reshape(n, A_WIDTH), yb.reshape(n, B_WIDTH), yc.reshape(n, C_WIDTH),
                         p.reshape(DEPTH, n, D_PLE), i, lw, pre_norm)
        states = (last[:, 0], _wkv_from_pairs(s_pairs), buf[:, SUBLANES - (B_CONV - 1):], c1, n1[:, :, 0],
                  m1[:, :, 0, 0])
        for lst, s in zip(new, states):
            lst.append(s)
    token_major = lambda a: jnp.transpose(a.reshape(DEPTH, bsz, C_HEADS, C_HEAD_DIM, t), (0, 1, 4, 2, 3))
    return xf.reshape(bsz, t, d), [jnp.stack(lst) for lst in new] + [token_major(kt), token_major(vt)]


def kernel(x_prompt, x_sample, state_shift_a, state_wkv, state_conv_b, state_mlstm_c, state_mlstm_n, state_mlstm_m, cache_sb_k, cache_sb_v, p_prompt, p_sample, ln_in_g, ln_in_b, w_in, b_in, mu_a, w0_a, w_decay_up, a0_a, w_iclr_up, k_k, k_a, r_k, gn_a_g, gn_a_b, conv_b_w, conv_b_b, hn_b_g, w_branch, w_out, ln_g, ln_b, w_ple, w_ple_gate):
    prm = dict(mu_a=mu_a, w0_a=w0_a, w_decay_up=w_decay_up, a0_a=a0_a, w_iclr_up=w_iclr_up, k_k=k_k, k_a=k_a,
               r_k=r_k, gn_a_g=gn_a_g, gn_a_b=gn_a_b, conv_b_w=conv_b_w, conv_b_b=conv_b_b, hn_b_g=hn_b_g,
               w_branch=w_branch, w_out=w_out, ln_g=ln_g, ln_b=ln_b, w_ple=w_ple, w_ple_gate=w_ple_gate)
    layers = [_layer_params(i, w_in, b_in, prm) for i in range(DEPTH)]
    bp = x_prompt.shape[0]
    zeros = lambda *s: jnp.zeros((DEPTH, bp) + s, F32)
    init_prompt = (zeros(A_SHIFT), zeros(A_HEADS, A_HEAD_DIM, A_HEAD_DIM), zeros(B_CONV - 1, 2 * B_WIDTH),
                   zeros(B_HEADS, B_HEAD_DIM, B_HEAD_DIM), zeros(B_HEADS, B_HEAD_DIM), zeros(B_HEADS), None, None)
    init_sample = (state_shift_a, state_wkv, state_conv_b, state_mlstm_c, state_mlstm_n, state_mlstm_m,
                   cache_sb_k, cache_sb_v)
    y_prompt, sp = _run_group(x_prompt, p_prompt, init_prompt, ln_in_g, ln_in_b, layers)
    y_sample, ss = _run_group(x_sample, p_sample, init_sample, ln_in_g, ln_in_b, layers)
    return (y_prompt, y_sample, sp[0], ss[0], sp[1], ss[1], sp[2], ss[2], sp[3], ss[3],
            sp[4], ss[4], sp[5], ss[5], sp[6], ss[6], sp[7], ss[7])
```

```python
import functools

import jax
import jax.numpy as jnp
from jax import lax
from jax.experimental import pallas as pl
from jax.experimental.pallas import tpu as pltpu

F32 = jnp.float32
BF16 = jnp.bfloat16

D_MODEL = 1024
DEPTH = 4
CHUNK = 64
D_PLE = 256
A_HEADS = 8
A_HEAD_DIM = 64
A_WIDTH = A_HEADS * A_HEAD_DIM
A_LORA = 64
A_SHIFT = 3 * A_WIDTH + 2 * A_LORA
B_HEADS = 4
B_HEAD_DIM = 128
B_WIDTH = B_HEADS * B_HEAD_DIM
B_CONV = 4
C_HEADS = 8
C_HEAD_DIM = 64
C_WIDTH = C_HEADS * C_HEAD_DIM
N_BRANCH = 3
DEEPNORM_ALPHA = (2 * DEPTH) ** 0.25
LN_EPS = 1e-5
GN_EPS_A = 64e-5
HN_EPS = 1e-6

LANES = 128
SUBLANES = 8
VMEM_LIMIT_BYTES = 56 * 1024 * 1024

MATMUL_TILE_ROWS = 512
RWKV_TILE_ROWS = 2048
RWKV_GROUP_CHUNKS = 2

_OFF_A_COLS = 0
_OFF_A_Z = _OFF_A_COLS + A_SHIFT
_OFF_B_QK = _OFF_A_Z + A_WIDTH
_OFF_B_V = _OFF_B_QK + 2 * B_WIDTH
_OFF_B_I = _OFF_B_V + B_WIDTH
_OFF_B_F = _OFF_B_I + B_HEADS
_OFF_B_O = _OFF_B_F + B_HEADS
_OFF_B_Z = _OFF_B_O + B_WIDTH
_OFF_C_Q = _OFF_B_Z + B_WIDTH
_OFF_C_K = _OFF_C_Q + C_WIDTH
_OFF_C_V = _OFF_C_K + C_WIDTH
_OFF_C_Z = _OFF_C_V + C_WIDTH
_OFF_GATES = _OFF_C_Z + C_WIDTH
_N_IN = _OFF_GATES + N_BRANCH * D_MODEL

_MIX_WIDTHS = (A_SHIFT, 2 * B_WIDTH, B_WIDTH, B_WIDTH, C_WIDTH, LANES)

NN = ((1,), (0,))
NT = ((1,), (1,))
TN = ((0,), (0,))


def _dg(a, b, dims=NN):
    return lax.dot_general(a, b, (dims, ((), ())), preferred_element_type=F32)


def _mm(a, b, dims=NN):
    return _dg(a.astype(BF16), b.astype(BF16), dims)


def _split3(x):
    h1 = x.astype(BF16)
    r1 = x - h1.astype(F32)
    h2 = r1.astype(BF16)
    h3 = (r1 - h2.astype(F32)).astype(BF16)
    return h1, h2, h3


def _mm_exact_rhs(lhs_bf16, x):
    h1, h2, h3 = _split3(x)
    return _dg(lhs_bf16, h1) + (_dg(lhs_bf16, h2) + _dg(lhs_bf16, h3))


def _mm_exact_lhs(x, rhs_bf16):
    h1, h2, h3 = _split3(x)
    return _dg(h1, rhs_bf16) + (_dg(h2, rhs_bf16) + _dg(h3, rhs_bf16))


def _softplus(x):
    return jnp.maximum(x, 0.0) + jnp.log(1.0 + jnp.exp(-jnp.abs(x)))


def _sigmoid(x):
    return jax.nn.sigmoid(x)


def _silu(x):
    return x * jax.nn.sigmoid(x)


def _norm_rows(x, eps):
    mu = jnp.mean(x, axis=-1, keepdims=True)
    xc = x - mu
    var = jnp.mean(xc * xc, axis=-1, keepdims=True)
    return xc * lax.rsqrt(var + eps)


def _tile(n, pref):
    if n <= pref:
        return n
    t = pref - pref % SUBLANES
    while n % t:
        t -= SUBLANES
    return t


def _params(*sem):
    return pltpu.CompilerParams(dimension_semantics=sem, vmem_limit_bytes=VMEM_LIMIT_BYTES)


def _const_spec(shape):
    nd = len(shape)
    return pl.BlockSpec(shape, lambda *_: (0,) * nd, pipeline_mode=pl.Buffered(1))


def _inproj_kernel(x_ref, w_ref, b_ref, wkv_ref, bkv_ref, *refs, pre_norm):
    n_out = len(_MIX_WIDTHS) + 2
    out_refs = refs[-n_out:]
    x = x_ref[...]
    if pre_norm:
        x = _norm_rows(x, LN_EPS) * refs[0][...] + refs[1][...]
    xb = x.astype(BF16)
    off = 0
    for o_ref, width in zip(out_refs, _MIX_WIDTHS):
        o_ref[...] = _dg(xb, w_ref[:, off:off + width]) + b_ref[:, off:off + width]
        off += width
    kvt = _dg(wkv_ref[...], xb, NT) + bkv_ref[...]
    out_refs[-2][0, 0] = kvt[:C_WIDTH]
    out_refs[-1][0, 0] = kvt[C_WIDTH:]


def _inproj_call(x, lw, layer, bsz, t, kt_stack=None, vt_stack=None, pre_norm=None):
    n, d = x.shape
    tm = _tile(t, MATMUL_TILE_ROWS)
    per_b = t // tm
    wtot = sum(_MIX_WIDTHS)
    stack = jax.ShapeDtypeStruct((DEPTH, bsz, C_WIDTH, t), F32)
    stack_spec = pl.BlockSpec((1, 1, C_WIDTH, tm), lambda i: (layer, i // per_b, 0, i % per_b))
    in_specs = [pl.BlockSpec((tm, d), lambda i: (i, 0)), _const_spec((d, wtot)), _const_spec((1, wtot)),
                _const_spec((2 * C_WIDTH, d)), _const_spec((2 * C_WIDTH, 1))]
    args = [x, lw["w_mix"], lw["b_mix"], lw["w_kvt"], lw["b_kvt"]]
    if pre_norm is not None:
        in_specs += [_const_spec((1, d))] * 2
        args += [a.reshape(1, d) for a in pre_norm]
    aliases = {}
    if kt_stack is not None:
        n_mix = len(_MIX_WIDTHS)
        aliases = {len(args): n_mix, len(args) + 1: n_mix + 1}
        in_specs += [pl.BlockSpec(memory_space=pl.ANY)] * 2
        args += [kt_stack, vt_stack]
    return pl.pallas_call(
        functools.partial(_inproj_kernel, pre_norm=pre_norm is not None),
        out_shape=tuple(jax.ShapeDtypeStruct((n, wd), F32) for wd in _MIX_WIDTHS) + (stack, stack),
        grid=(n // tm,),
        in_specs=in_specs,
        out_specs=tuple(pl.BlockSpec((tm, wd), lambda i: (i, 0)) for wd in _MIX_WIDTHS) + (stack_spec, stack_spec),
        input_output_aliases=aliases,
        compiler_params=_params("parallel"),
        name="inproj",
    )(*args)


RWKV_RING = 3


def _round_robin(*steps):
    steps = list(steps)
    while steps:
        for s in list(steps):
            try:
                next(s)
            except StopIteration:
                steps.remove(s)


def _rwkv_pipe_kernel(cols_ref, last_ref, s0_ref, mu_ref, w0_ref, wup_ref, a0_ref, aup_ref, kk_ref, ka_ref, hsum_ref,
                      rk_ref, gng_ref, gnb_ref, y_o, last_o, s_o,
                      prev_scr, s_scr, at_r, rt_r, bt_r, kt_r, btw_r, ktw_r, v_r, bon_r, w_r, u0_r, x1_r, mrb_r, wl_r):
    t = pl.program_id(1)
    L = CHUNK
    tt = cols_ref.shape[1]
    gr = min(tt, RWKV_GROUP_CHUNKS * L)
    cpg = gr // L
    ng = tt // gr
    n_pairs = A_HEADS // 2

    @pl.when(t == 0)
    def _():
        prev_scr[...] = last_ref[0]
        s_scr[...] = s0_ref[0]

    ri = lax.broadcasted_iota(jnp.int32, (L, L), 0)
    ci = lax.broadcasted_iota(jnp.int32, (L, L), 1)
    strict = ri > ci
    incl = ri >= ci
    tri = jnp.where(incl, 1.0, 0.0).astype(BF16)
    eye = jnp.where(ri == ci, 1.0, 0.0)
    lane = lax.broadcasted_iota(jnp.int32, (1, LANES), 1)
    head_masks = (jnp.where(lane < A_HEAD_DIM, 1.0, 0.0), jnp.where(lane >= A_HEAD_DIM, 1.0, 0.0))
    inv_d = 1.0 / A_HEAD_DIM
    pair = lambda p: slice(p * LANES, (p + 1) * LANES)
    chunk = lambda c: slice(c * L, (c + 1) * L)
    heads = [(p, hm) for p in range(n_pairs) for hm in head_masks]

    def prologue(g):
        gp = g
        slot = lax.rem(gp, RWKV_RING)
        r0 = pl.multiple_of(gp * gr, gr)
        u = cols_ref[0, pl.ds(r0, gr), :]
        above = cols_ref[0, pl.ds(pl.multiple_of(jnp.maximum(r0 - SUBLANES, 0), SUBLANES), SUBLANES), :]
        before = jnp.where(gp == 0, prev_scr[...], above[SUBLANES - 1:SUBLANES, :])
        row = lax.broadcasted_iota(jnp.int32, u.shape, 0)
        prev = jnp.where(row == 0, before, pltpu.roll(u, 1, 0))
        xs = u + (prev - u) * mu_ref[...]
        yield
        r = xs[:, 0:A_WIDTH]
        k = xs[:, A_WIDTH:2 * A_WIDTH]
        v = xs[:, 2 * A_WIDTH:3 * A_WIDTH]
        lora = xs[:, 3 * A_WIDTH:]
        w_log = -_softplus(-(w0_ref[...] + _mm(jnp.tanh(lora), wup_ref[...]))) - 0.5
        a = _sigmoid(a0_ref[...] + _mm(lora, aup_ref[...]))
        yield
        kk = k * kk_ref[...]
        ss = _mm_exact_lhs(kk * kk, hsum_ref[...])
        kk = kk / jnp.maximum(jnp.sqrt(ss), 1e-12)
        yield
        kh = k * (1.0 + (a - 1.0) * ka_ref[...])
        lw = -jnp.exp(w_log)
        cin = jnp.concatenate([_mm_exact_rhs(tri, lw[chunk(c), :]) for c in range(cpg)], axis=0)
        yield
        ein = jnp.exp(cin)
        einv = jnp.exp(-cin)
        bt = kk * a * einv
        kt = kh * einv
        at_r[slot] = -kk * jnp.exp(cin - lw)
        rt_r[slot] = r * ein
        yield
        bt_r[slot] = bt
        kt_r[slot] = kt
        v_r[slot] = v
        bon_r[slot] = r * kh * rk_ref[...]
        yield
        for c in range(cpg):
            wl = ein[(c + 1) * L - 1:(c + 1) * L, :]
            wl_r[slot, c] = wl
            btw_r[slot, chunk(c), :] = bt[chunk(c), :] * wl
            ktw_r[slot, chunk(c), :] = kt[chunk(c), :] * wl
        yield

    def state_free(g):
        slot = lax.rem(g - 1, RWKV_RING)
        items = [(c, p, hm) for c in range(cpg) for p, hm in heads]
        load = lambda ring: [ring[slot, chunk(c), pair(p)] * hm for c, p, hm in items]
        at_h, rt_h, v_h = load(at_r), load(rt_r), load(v_r)
        lhs = [jnp.concatenate([x, y], axis=0).astype(BF16) for x, y in zip(at_h, rt_h)]
        yield
        sc_b = [_dg(x, bt_r[slot, chunk(c), pair(p)].astype(BF16), NT) for x, (c, p, _) in zip(lhs, items)]
        sc_k = [_dg(x, kt_r[slot, chunk(c), pair(p)].astype(BF16), NT) for x, (c, p, _) in zip(lhs, items)]
        yield
        a_ab = [jnp.where(strict, s[:L], 0.0) for s in sc_b]
        a_ak = [jnp.where(strict, s[:L], 0.0) for s in sc_k]
        m_rb = [jnp.where(incl, s[L:], 0.0) for s in sc_b]
        m_rk = [jnp.where(incl, s[L:], 0.0) for s in sc_k]
        tm = [eye + x for x in a_ab]
        ap = a_ab
        for _ in range(5):
            ap = [_mm(x, x) for x in ap]
            yield
            tm = [x + _mm(x, y) for x, y in zip(tm, ap)]
            yield
        x1 = [_mm(jnp.concatenate([x, y], axis=0), vh) for x, y, vh in zip(a_ak, m_rk, v_h)]
        yield
        wu = [_mm(x, jnp.concatenate([y, z[:L]], axis=1)) for x, y, z in zip(tm, at_h, x1)]
        yield
        for n in range(0, len(items), 2):
            c, p, _ = items[n]
            w_r[slot, chunk(c), pair(p)] = wu[n][:, :LANES] + wu[n + 1][:, :LANES]
            u0_r[slot, chunk(c), pair(p)] = wu[n][:, LANES:] + wu[n + 1][:, LANES:]
            x1_r[slot, chunk(c), pair(p)] = x1[n][L:] + x1[n + 1][L:]
            mrb_r[slot, c, 2 * p] = m_rb[n]
            mrb_r[slot, c, 2 * p + 1] = m_rb[n + 1]
        yield

    def state_step(g):
        g2 = g - 2
        slot = lax.rem(g2, RWKV_RING)
        hm0, hm1 = head_masks
        y_pairs = [[] for _ in range(n_pairs)]
        for c in range(cpg):
            s_pairs = [s_scr[p] for p in range(n_pairs)]
            s_pairs_b = [s.astype(BF16) for s in s_pairs]
            w_h = [w_r[slot, chunk(c), pair(p)] * hm for p, hm in heads]
            rt_h = [rt_r[slot, chunk(c), pair(p)] * hm for p, hm in heads]
            st = [_dg(jnp.concatenate([x, y], axis=0).astype(BF16), s_pairs_b[p], NT)
                  for x, y, (p, _) in zip(w_h, rt_h, heads)]
            yield
            u_h = [u0_r[slot, chunk(c), pair(p)] * hm + s[:L] for s, (p, hm) in zip(st, heads)]
            y_h = [s[L:] + _mm(mrb_r[slot, c, n], x) for n, (s, x) in enumerate(zip(st, u_h))]
            upd = []
            for p in range(n_pairs):
                vp, bp, kp = v_r[slot, chunk(c), pair(p)], btw_r[slot, chunk(c), pair(p)], ktw_r[slot, chunk(c), pair(p)]
                upd.append(_mm(jnp.concatenate([u_h[2 * p], vp * hm0, u_h[2 * p + 1], vp * hm1], axis=0),
                               jnp.concatenate([bp * hm0, kp * hm0, bp * hm1, kp * hm1], axis=0), TN))
            yield
            wl = wl_r[slot, c]
            for p in range(n_pairs):
                s_scr[p] = s_pairs[p] * wl[:, pair(p)] + upd[p]
                y_pairs[p].append(y_h[2 * p] + y_h[2 * p + 1] + x1_r[slot, chunk(c), pair(p)])
            yield
        rows = pl.ds(pl.multiple_of(g2 * gr, gr), gr)
        for p in range(n_pairs):
            y_pair = jnp.concatenate(y_pairs[p], axis=0)
            normed = jnp.zeros(y_pair.shape, F32)
            bonus = jnp.zeros(y_pair.shape, F32)
            for hm in head_masks:
                mu = jnp.sum(y_pair * hm, axis=-1, keepdims=True) * inv_d
                yc = (y_pair - mu) * hm
                var = jnp.sum(yc * yc, axis=-1, keepdims=True) * inv_d
                normed = normed + yc * lax.rsqrt(var + GN_EPS_A)
                bonus = bonus + (jnp.sum(bon_r[slot, :, pair(p)] * hm, axis=-1, keepdims=True)
                                 * (v_r[slot, :, pair(p)] * hm))
            y_o[0, rows, pair(p)] = (normed * gng_ref[:, pair(p)] + gnb_ref[:, pair(p)]) + bonus
            yield

    def trip(g, carry):
        _round_robin(prologue(g), state_free(g), state_step(g))
        return carry

    at = jnp.int32
    if ng == 1:
        for g, step in enumerate((prologue, state_free, state_step)):
            _round_robin(step(at(g)))
    else:
        _round_robin(prologue(at(0)))
        _round_robin(prologue(at(1)), state_free(at(1)))
        lax.fori_loop(2, ng, trip, 0)
        _round_robin(state_free(at(ng)), state_step(at(ng)))
        _round_robin(state_step(at(ng + 1)))
    prev_scr[...] = cols_ref[0, tt - 1:tt, :]
    last_o[0] = cols_ref[0, tt - 1:tt, :]
    s_o[0] = s_scr[...]


def _rwkv_call(cols, last, s0_pairs, lw):
    bsz, t, _ = cols.shape
    tt = _tile(t, RWKV_TILE_ROWS)
    gr = min(tt, RWKV_GROUP_CHUNKS * CHUNK)
    blk = lambda w: pl.BlockSpec((1, tt, w), lambda b, i: (b, i, 0))
    row = pl.BlockSpec((1, 1, A_SHIFT), lambda b, i: (b, 0, 0))
    st = pl.BlockSpec((1, A_HEADS // 2, LANES, LANES), lambda b, i: (b, 0, 0, 0))
    ring = pltpu.VMEM((RWKV_RING, gr, A_WIDTH), F32)
    return pl.pallas_call(
        _rwkv_pipe_kernel,
        out_shape=(jax.ShapeDtypeStruct((bsz, t, A_WIDTH), F32),
                   jax.ShapeDtypeStruct((bsz, 1, A_SHIFT), F32),
                   jax.ShapeDtypeStruct((bsz, A_HEADS // 2, LANES, LANES), F32)),
        grid=(bsz, t // tt),
        in_specs=[blk(A_SHIFT), row, st, _const_spec((1, A_SHIFT)), _const_spec((1, A_WIDTH)),
                  _const_spec((2 * A_LORA, A_WIDTH)), _const_spec((1, A_WIDTH)), _const_spec((2 * A_LORA, A_WIDTH)),
                  _const_spec((1, A_WIDTH)), _const_spec((1, A_WIDTH)), _const_spec((A_WIDTH, A_WIDTH)),
                  _const_spec((1, A_WIDTH)), _const_spec((1, A_WIDTH)), _const_spec((1, A_WIDTH))],
        out_specs=(blk(A_WIDTH), row, st),
        scratch_shapes=[pltpu.VMEM((1, A_SHIFT), F32), pltpu.VMEM((A_HEADS // 2, LANES, LANES), F32)]
        + [ring] * 11
        + [pltpu.VMEM((RWKV_RING, gr // CHUNK, A_HEADS, CHUNK, CHUNK), F32),
           pltpu.VMEM((RWKV_RING, gr // CHUNK, 1, A_WIDTH), F32)],
        compiler_params=_params("parallel", "arbitrary"),
        name="rwkv",
    )(cols, last.reshape(bsz, 1, A_SHIFT), s0_pairs, lw["mu_a"], lw["w0_a"], lw["wup_pad"], lw["a0_a"],
      lw["aup_pad"], lw["k_k"], lw["k_a"], lw["head_sum"], lw["r_k"], lw["gn_a_g"], lw["gn_a_b"])


def _wkv_to_pairs(s):
    bsz = s.shape[0]
    s = s.reshape(bsz, A_HEADS // 2, 2, A_HEAD_DIM, A_HEAD_DIM)
    z = jnp.zeros_like(s[:, :, 0])
    top = jnp.concatenate([s[:, :, 0], z], axis=-1)
    bot = jnp.concatenate([z, s[:, :, 1]], axis=-1)
    return jnp.concatenate([top, bot], axis=-2)


def _wkv_from_pairs(sp):
    bsz = sp.shape[0]
    d = A_HEAD_DIM
    s = jnp.stack([sp[:, :, :d, :d], sp[:, :, d:, d:]], axis=2)
    return s.reshape(bsz, A_HEADS, d, d)


def _mlstm_steps(t, qk_ref, v_ref, if_ref, o_ref, buf0_ref, c0_ref, n0_ref, m0_ref, cw_ref, cb_ref, hng_ref,
                 y_o, buf_o, c_o, n_o, m_o, ext_scr, c_scr, n_scr, m_scr):
    tt = qk_ref.shape[1]
    L = min(tt, LANES)
    assert tt == L
    pad = SUBLANES

    @pl.when(t == 0)
    def _():
        ext_scr[0:pad, :] = buf0_ref[0]
        c_scr[...] = c0_ref[0]
        n_scr[...] = n0_ref[0]
        m_scr[...] = m0_ref[0]

    ext_scr[pad:pad + tt, :] = qk_ref[0]
    ri = lax.broadcasted_iota(jnp.int32, (L, L), 0)
    ci = lax.broadcasted_iota(jnp.int32, (L, L), 1)
    incl = ri >= ci
    eye = ri == ci
    tri = jnp.where(incl, 1.0, 0.0).astype(BF16)
    k_scale = B_HEAD_DIM ** -0.5

    hs = range(B_HEADS)
    sls = [slice(h * B_HEAD_DIM, (h + 1) * B_HEAD_DIM) for h in hs]

    x = ext_scr[...]
    conv = cb_ref[...] + x[pad:, :] * cw_ref[B_CONV - 1:B_CONV, :]
    for tap in range(B_CONV - 1):
        conv = conv + pltpu.roll(x, B_CONV - 1 - tap, 0)[pad:, :] * cw_ref[tap:tap + 1, :]
    yield
    qk = _silu(conv)
    yield
    ifg = if_ref[0]
    bcum = _mm_exact_rhs(tri, -_softplus(-ifg))
    yield
    q_h = [qk[:, sl] for sl in sls]
    k_h = [qk[:, B_WIDTH + h * B_HEAD_DIM:B_WIDTH + (h + 1) * B_HEAD_DIM] * k_scale for h in hs]
    v_h = [v_ref[0, :, sl] for sl in sls]
    icol = [ifg[:, h:h + 1] for h in hs]
    bcol = [bcum[:, B_HEADS + h:B_HEADS + h + 1] for h in hs]
    qk_t = [_mm(q, k, NT) for q, k in zip(q_h, k_h)]
    yield
    dcol = [i_ - b for i_, b in zip(icol, bcol)]
    drow = [jnp.sum(jnp.where(eye, d, 0.0), axis=0, keepdims=True) for d in dcol]
    dmat = [jnp.where(incl, b + d, -jnp.inf) for b, d in zip(bcol, drow)]
    yield
    dmax = [jnp.max(d, axis=-1, keepdims=True) for d in dmat]
    yield
    c_h = [c_scr[h] for h in hs]
    n_h = [n_scr[h] for h in hs]
    m_h = [m_scr[h][:, 0:1] for h in hs]
    qc = [_mm(q, c, NT) for q, c in zip(q_h, c_h)]
    yield
    inter = [b + m for b, m in zip(bcol, m_h)]
    m_t = [jnp.maximum(x_, d) for x_, d in zip(inter, dmax)]
    wmat = [jnp.exp(d - m) * s for d, m, s in zip(dmat, m_t, qk_t)]
    yield
    scale = [jnp.exp(x_ - m) for x_, m in zip(inter, m_t)]
    wv = [_mm(w, v) for w, v in zip(wmat, v_h)]
    yield
    m_new = [m[L - 1:L, :] for m in m_t]
    gcol = [jnp.exp(b[L - 1:L, :] + d - m) for b, d, m in zip(bcol, dcol, m_new)]
    c_sc = [jnp.exp(x_[L - 1:L, :] - m) for x_, m in zip(inter, m_new)]
    gvk = [_mm(g * v, k, TN) for g, v, k in zip(gcol, v_h, k_h)]
    yield
    for h in hs:
        num = wv[h] + scale[h] * qc[h]
        den = (jnp.sum(wmat[h], axis=-1, keepdims=True)
               + scale[h] * jnp.sum(q_h[h] * n_h[h], axis=-1, keepdims=True))
        hid = num / jnp.maximum(jnp.abs(den), jnp.exp(-m_t[h]))
        y_o[0, :, sls[h]] = _sigmoid(o_ref[0, :, sls[h]]) * (_norm_rows(hid, HN_EPS) * hng_ref[:, sls[h]])
        yield
    for h in hs:
        c_scr[h] = c_sc[h] * c_h[h] + gvk[h]
        n_scr[h] = c_sc[h] * n_h[h] + jnp.sum(gcol[h] * k_h[h], axis=0, keepdims=True)
        m_scr[h] = jnp.broadcast_to(m_new[h], (1, LANES))
    last = ext_scr[tt:tt + pad, :]
    ext_scr[0:pad, :] = last
    buf_o[0] = last
    c_o[0] = c_scr[...]
    n_o[0] = n_scr[...]
    m_o[0] = m_scr[...]


SB_SKIP_LOG = -105.0


_N_MLSTM_IN, _N_MLSTM_OUT, _N_SB_SCRATCH = 11, 5, 4


def _sb_mlstm_kernel(*refs, n_past_blocks):
    n_att = 5 if n_past_blocks else 3
    att_in, refs = refs[:n_att], refs[n_att:]
    ml_in, o_ref, refs = refs[:_N_MLSTM_IN], refs[_N_MLSTM_IN], refs[_N_MLSTM_IN + 1:]
    ml_out, refs = refs[:_N_MLSTM_OUT], refs[_N_MLSTM_OUT:]
    (q2_scr, c_scr, acc_scr, pad_scr), ml_scr = refs[:_N_SB_SCRATCH], refs[_N_SB_SCRATCH:]
    if n_past_blocks:
        q_ref, k_ref, v_ref, kp_ref, vp_ref = att_in
    else:
        q_ref, k_ref, v_ref = att_in
    mlstm = _mlstm_steps(pl.program_id(1), *ml_in, *ml_out, *ml_scr)
    qi = pl.program_id(1)
    qb = q_ref.shape[1]
    kb = LANES
    n_pairs = C_HEADS // 2
    n_rows = C_HEADS * qb

    rj = lax.broadcasted_iota(jnp.int32, (2 * kb, 2 * kb), 0) & (kb - 1)
    cj = lax.broadcasted_iota(jnp.int32, (2 * kb, 2 * kb), 1)
    cum2 = jnp.where((cj >= kb) | (rj > cj), 1.0, 0.0).astype(BF16)
    qpos = lax.broadcasted_iota(jnp.int32, (n_rows, kb), 0) & (qb - 1)
    kpos = lax.broadcasted_iota(jnp.int32, (n_rows, kb), 1)
    diag_mask = kpos < qpos
    lane = lax.broadcasted_iota(jnp.int32, (1, LANES), 1)
    lo_lanes = lane < C_HEAD_DIM
    q = q_ref[0] * (C_HEAD_DIM ** -0.5)
    for p in range(n_pairs):
        qs = q[:, p * LANES:(p + 1) * LANES]
        q2_scr[p] = jnp.concatenate([jnp.where(lo_lanes, qs, 0.0), jnp.where(lo_lanes, 0.0, qs)], axis=0).astype(BF16)

    def load_t(ref, slot, p, keys):
        x = ref[0, 0, p * LANES:(p + 1) * LANES, keys]
        if x.shape[1] == kb:
            return x.astype(BF16)
        pad_scr[slot] = jnp.zeros((LANES, kb), F32)
        pad_scr[slot, :, 0:x.shape[1]] = x
        return pad_scr[slot].astype(BF16)

    def half_block(pairs, kr, vr, keys, first, c_max):
        rows = slice(pairs[0] * 2 * qb, (pairs[-1] + 1) * 2 * qb)
        mask = diag_mask[:rows.stop - rows.start]
        z = jnp.concatenate([_dg(q2_scr[p], load_t(kr, 2 * p, p, keys)) for p in pairs], axis=0)
        yield
        sp = _softplus(z)
        lk = jnp.where(mask, -sp, 0.0) if first else -sp
        h1 = lk.astype(BF16)
        h2 = (lk - h1.astype(F32)).astype(BF16)
        yield
        cs = _dg(jnp.concatenate([h1, h2], axis=1), cum2)
        yield
        if first:
            a = jnp.where(mask, jnp.exp((z - sp) + cs[:, :kb]), 0.0)
            c_new = cs[:, kb:]
        else:
            c_old = c_scr[rows]
            a = jnp.exp((z - sp) + cs[:, :kb] + c_old)
            c_new = c_old + cs[:, kb:]
        c_scr[rows] = c_new
        c_max.append(jnp.max(c_new))
        ab = a.astype(BF16)
        yield
        for n, p in enumerate(pairs):
            rs = slice(p * 2 * qb, (p + 1) * 2 * qb)
            upd = _dg(ab[n * 2 * qb:(n + 1) * 2 * qb], load_t(vr, 2 * p + 1, p, keys), NT)
            acc_scr[rs] = upd if first else acc_scr[rs] + upd
        yield

    def block(kr, vr, keys, first, others=()):
        c_max = []
        lead = half_block(tuple(range(n_pairs // 2)), kr, vr, keys, first, c_max)
        lag = half_block(tuple(range(n_pairs // 2, n_pairs)), kr, vr, keys, first, c_max)
        next(lead)
        _round_robin(lead, lag, *others)
        return jnp.maximum(*c_max) > SB_SKIP_LOG

    own_keys = pl.ds(pl.multiple_of(qi * qb, qb), qb) if qb == kb else pl.ds(0, qb)
    go = block(k_ref, v_ref, own_keys, True, (mlstm,))

    def sweep(kr, vr, n_blocks, go):
        def cond(carry):
            j, more = carry
            return jnp.logical_and(j < n_blocks, more)

        def keys_at(j):
            return pl.ds(pl.multiple_of((n_blocks - 1 - j) * kb, kb), kb)

        def pair_cond(carry):
            j, more = carry
            return jnp.logical_and(jnp.logical_and(j == 0, n_blocks >= 2), more)

        def pair_body(carry):
            j, _ = carry
            c_max = []
            halves = [half_block(pairs, kr, vr, keys_at(j + n), False, c_max)
                      for n in range(2) for pairs in (tuple(range(n_pairs // 2)), tuple(range(n_pairs // 2, n_pairs)))]
            next(halves[0])
            next(halves[0])
            next(halves[1])
            _round_robin(halves[0], halves[2], halves[1], halves[3])
            return j + 2, functools.reduce(jnp.maximum, c_max) > SB_SKIP_LOG

        def body(carry):
            j, _ = carry
            return j + 1, block(kr, vr, keys_at(j), False)

        return lax.while_loop(cond, body, lax.while_loop(pair_cond, pair_body, (jnp.int32(0), go)))[1]

    if qb == kb:
        go = sweep(k_ref, v_ref, qi, go)
    if n_past_blocks:
        go = sweep(kp_ref, vp_ref, n_past_blocks, go)

    for p in range(n_pairs):
        r0 = p * 2 * qb
        o_ref[0, :, p * LANES:(p + 1) * LANES] = jnp.where(lo_lanes, acc_scr[r0:r0 + qb], acc_scr[r0 + qb:r0 + 2 * qb])


def _sb_mlstm_call(q, kt, vt, layer, kt_past, vt_past, qk, v, ifg, o, buf0, c0, n0, m0, lw):
    bsz, t, _ = q.shape
    qb = min(LANES, t)
    assert t % qb == 0 and (qb == LANES or t == qb)
    n_past = 0 if kt_past is None else kt_past.shape[-1]
    assert n_past % LANES == 0
    blk = lambda w: pl.BlockSpec((1, qb, w), lambda b, i: (b, i, 0))
    full = lambda n: pl.BlockSpec((1, 1, C_WIDTH, n), lambda b, i: (layer, b, 0, 0))
    fix = lambda *s: pl.BlockSpec((1,) + s, lambda b, i: (b,) + (0,) * len(s))
    in_specs = [blk(C_WIDTH), full(t), full(t)]
    args = [q, kt, vt]
    if n_past:
        in_specs += [full(n_past), full(n_past)]
        args += [kt_past, vt_past]
    in_specs += [blk(2 * B_WIDTH), blk(B_WIDTH), blk(LANES), blk(B_WIDTH),
                 fix(SUBLANES, 2 * B_WIDTH), fix(B_HEADS, B_HEAD_DIM, B_HEAD_DIM), fix(B_HEADS, 1, B_HEAD_DIM),
                 fix(B_HEADS, 1, LANES),
                 _const_spec((B_CONV, 2 * B_WIDTH)), _const_spec((1, 2 * B_WIDTH)), _const_spec((1, B_WIDTH))]
    args += [qk, v, ifg, o, buf0, c0, n0, m0, lw["conv_b_w"], lw["conv_b_b"], lw["hn_b_g"]]
    assert len(in_specs) - (5 if n_past else 3) == _N_MLSTM_IN
    return pl.pallas_call(
        functools.partial(_sb_mlstm_kernel, n_past_blocks=n_past // LANES),
        out_shape=(jax.ShapeDtypeStruct((bsz, t, C_WIDTH), F32),
                   jax.ShapeDtypeStruct((bsz, t, B_WIDTH), F32),
                   jax.ShapeDtypeStruct((bsz, SUBLANES, 2 * B_WIDTH), F32),
                   jax.ShapeDtypeStruct((bsz, B_HEADS, B_HEAD_DIM, B_HEAD_DIM), F32),
                   jax.ShapeDtypeStruct((bsz, B_HEADS, 1, B_HEAD_DIM), F32),
                   jax.ShapeDtypeStruct((bsz, B_HEADS, 1, LANES), F32)),
        grid=(bsz, t // qb),
        in_specs=in_specs,
        out_specs=(blk(C_WIDTH), blk(B_WIDTH), fix(SUBLANES, 2 * B_WIDTH), fix(B_HEADS, B_HEAD_DIM, B_HEAD_DIM),
                   fix(B_HEADS, 1, B_HEAD_DIM), fix(B_HEADS, 1, LANES)),
        scratch_shapes=[pltpu.VMEM((C_HEADS // 2, 2 * qb, LANES), BF16),
                        pltpu.VMEM((C_HEADS * qb, LANES), F32), pltpu.VMEM((C_HEADS * qb, LANES), F32),
                        pltpu.VMEM((C_HEADS, LANES, LANES), F32),
                        pltpu.VMEM((qb + SUBLANES, 2 * B_WIDTH), F32),
                        pltpu.VMEM((B_HEADS, B_HEAD_DIM, B_HEAD_DIM), F32),
                        pltpu.VMEM((B_HEADS, 1, B_HEAD_DIM), F32),
                        pltpu.VMEM((B_HEADS, 1, LANES), F32)],
        compiler_params=_params("parallel", "arbitrary"),
        name="sb_mlstm",
    )(*args)


def _merge_kernel(x_ref, ya_ref, yb_ref, yc_ref, p_ref, wz_ref, bz_ref, wg_ref, bg_ref, wbr_ref, wout_ref,
                  lng_ref, lnb_ref, wple_ref, wpg_ref, *refs, pre_norm):
    o_ref = refs[-1]
    width = A_WIDTH
    half_rows = x_ref.shape[0] // 2

    def half(r):
        rows = slice(r * half_rows, (r + 1) * half_rows)
        x = x_ref[rows, :]
        if pre_norm:
            x = _norm_rows(x, LN_EPS) * refs[0][...] + refs[1][...]
        xb = x.astype(BF16)
        mix_pre = jnp.zeros(x.shape, F32)
        for n, y_ref in enumerate((ya_ref, yb_ref, yc_ref)):
            z = _dg(xb, wz_ref[:, n * width:(n + 1) * width]) + bz_ref[:, n * width:(n + 1) * width]
            yield
            ys = y_ref[rows, :] * _silu(z)
            br = _mm(ys, wbr_ref[n])
            yield
            gate = _dg(xb, wg_ref[:, n * D_MODEL:(n + 1) * D_MODEL]) + bg_ref[:, n * D_MODEL:(n + 1) * D_MODEL]
            yield
            mix_pre = mix_pre + _sigmoid(gate) * br
        mix = _mm(mix_pre, wout_ref[...])
        yield
        x1 = _norm_rows(DEEPNORM_ALPHA * x + mix, LN_EPS) * lng_ref[...] + lnb_ref[...]
        yield
        o_ref[rows, :] = x1 + _mm(p_ref[0, rows, :], wple_ref[...]) * _sigmoid(_mm(x1, wpg_ref[...]))
        yield

    lead, lag = half(0), half(1)
    next(lead)
    _round_robin(lead, lag)


def _merge_call(x, ya, yb, yc, p_all, layer, lw, pre_norm=None):
    n, d = x.shape
    tm = _tile(n, MATMUL_TILE_ROWS)
    rows = lambda w: pl.BlockSpec((tm, w), lambda i: (i, 0))
    in_specs = [rows(d), rows(A_WIDTH), rows(B_WIDTH), rows(C_WIDTH),
                pl.BlockSpec((1, tm, D_PLE), lambda i: (layer, i, 0)),
                _const_spec((d, N_BRANCH * A_WIDTH)), _const_spec((1, N_BRANCH * A_WIDTH)),
                _const_spec((d, N_BRANCH * d)), _const_spec((1, N_BRANCH * d)),
                _const_spec((N_BRANCH, A_WIDTH, d)), _const_spec((d, d)),
                _const_spec((1, d)), _const_spec((1, d)), _const_spec((D_PLE, d)), _const_spec((d, d))]
    args = [x, ya, yb, yc, p_all, lw["w_z"], lw["b_z"], lw["w_gates"], lw["b_gates"], lw["w_branch"], lw["w_out"],
            lw["ln_g"], lw["ln_b"], lw["w_ple"], lw["w_ple_gate"]]
    if pre_norm is not None:
        in_specs += [_const_spec((1, d))] * 2
        args += [a.reshape(1, d) for a in pre_norm]
    return pl.pallas_call(
        functools.partial(_merge_kernel, pre_norm=pre_norm is not None),
        out_shape=jax.ShapeDtypeStruct((n, d), F32),
        grid=(n // tm,),
        in_specs=in_specs,
        out_specs=rows(d),
        compiler_params=_params("parallel"),
        name="merge",
    )(*args)


def _cols(w, *ranges):
    return jnp.concatenate([w[..., a:b] for a, b in ranges], axis=-1)


def _layer_params(i, w_in, b_in, prm):
    w = w_in[i]
    b = b_in[i]
    mix_ranges = ((_OFF_A_COLS, _OFF_A_Z), (_OFF_B_QK, _OFF_B_V), (_OFF_B_V, _OFF_B_I), (_OFF_B_O, _OFF_B_Z),
                  (_OFF_C_Q, _OFF_C_K), (_OFF_B_I, _OFF_B_O))
    pad = LANES - 2 * B_HEADS
    w_mix = jnp.pad(_cols(w, *mix_ranges), ((0, 0), (0, pad)))
    b_mix = jnp.pad(_cols(b, *mix_ranges), ((0, pad),))
    z_ranges = ((_OFF_A_Z, _OFF_B_QK), (_OFF_B_Z, _OFF_C_Q), (_OFF_C_Z, _OFF_GATES))
    zeros_lora = jnp.zeros((A_LORA, A_WIDTH), F32)
    hid = jnp.arange(A_WIDTH) // A_HEAD_DIM
    row = lambda name: prm[name][i].reshape(1, -1)
    return {
        "w_mix": w_mix.astype(BF16), "b_mix": b_mix.reshape(1, -1),
        "w_kvt": w[:, _OFF_C_K:_OFF_C_Z].T.astype(BF16), "b_kvt": b[_OFF_C_K:_OFF_C_Z].reshape(-1, 1),
        "w_z": _cols(w, *z_ranges).astype(BF16), "b_z": _cols(b, *z_ranges).reshape(1, -1),
        "w_gates": w[:, _OFF_GATES:].astype(BF16), "b_gates": b[_OFF_GATES:].reshape(1, -1),
        "mu_a": row("mu_a"), "w0_a": row("w0_a"), "a0_a": row("a0_a"), "k_k": row("k_k"), "k_a": row("k_a"),
        "wup_pad": jnp.concatenate([prm["w_decay_up"][i], zeros_lora], axis=0).astype(BF16),
        "aup_pad": jnp.concatenate([zeros_lora, prm["w_iclr_up"][i]], axis=0).astype(BF16),
        "head_sum": (hid[:, None] == hid[None, :]).astype(BF16),
        "r_k": row("r_k"), "gn_a_g": row("gn_a_g"), "gn_a_b": row("gn_a_b"),
        "conv_b_w": prm["conv_b_w"][i], "conv_b_b": row("conv_b_b"), "hn_b_g": row("hn_b_g"),
        "w_branch": prm["w_branch"][i].astype(BF16), "w_out": prm["w_out"][i].astype(BF16),
        "ln_g": row("ln_g"), "ln_b": row("ln_b"),
        "w_ple": prm["w_ple"][i].astype(BF16), "w_ple_gate": prm["w_ple_gate"][i].astype(BF16),
    }


def _run_group(x, p, init, ln_in_g, ln_in_b, layers):
    bsz, t, d = x.shape
    n = bsz * t
    shift0, wkv0, conv0, c0, n0, m0, k_past, v_past = init
    xf = x.reshape(n, d)
    new = [[] for _ in range(6)]
    kt = vt = None
    channel_major = lambda a: jnp.transpose(a, (0, 1, 3, 4, 2)).reshape(a.shape[0], a.shape[1], C_WIDTH, a.shape[2])
    if k_past is not None:
        k_past, v_past = channel_major(k_past), channel_major(v_past)
    for i, lw in enumerate(layers):
        pre_norm = (ln_in_g, ln_in_b) if i == 0 else None
        a_cols, b_qk, b_v, b_o, c_q, b_if, kt, vt = _inproj_call(xf, lw, i, bsz, t, kt, vt, pre_norm)
        seq = lambda a: a.reshape(bsz, t, a.shape[-1])
        ya, last, s_pairs = _rwkv_call(seq(a_cols), shift0[i], _wkv_to_pairs(wkv0[i]), lw)
        buf0 = jnp.pad(conv0[i], ((0, 0), (SUBLANES - (B_CONV - 1), 0), (0, 0)))
        m0b = jnp.broadcast_to(m0[i][:, :, None, None], (bsz, B_HEADS, 1, LANES))
        yc, yb, buf, c1, n1, m1 = _sb_mlstm_call(seq(c_q), kt, vt, i, k_past, v_past,
                                                 seq(b_qk), seq(b_v), seq(b_if), seq(b_o), buf0, c0[i],
                                                 n0[i][:, :, None, :], m0b, lw)
        xf = _merge_call(xf, ya.reshape(n, A_WIDTH), yb.reshape(n, B_WIDTH), yc.reshape(n, C_WIDTH),
                         p.reshape(DEPTH, n, D_PLE), i, lw, pre_norm)
        states = (last[:, 0], _wkv_from_pairs(s_pairs), buf[:, SUBLANES - (B_CONV - 1):], c1, n1[:, :, 0],
                  m1[:, :, 0, 0])
        for lst, s in zip(new, states):
            lst.append(s)
    token_major = lambda a: jnp.transpose(a.reshape(DEPTH, bsz, C_HEADS, C_HEAD_DIM, t), (0, 1, 4, 2, 3))
    return xf.reshape(bsz, t, d), [jnp.stack(lst) for lst in new] + [token_major(kt), token_major(vt)]


def kernel(x_prompt, x_sample, state_shift_a, state_wkv, state_conv_b, state_mlstm_c, state_mlstm_n, state_mlstm_m, cache_sb_k, cache_sb_v, p_prompt, p_sample, ln_in_g, ln_in_b, w_in, b_in, mu_a, w0_a, w_decay_up, a0_a, w_iclr_up, k_k, k_a, r_k, gn_a_g, gn_a_b, conv_b_w, conv_b_b, hn_b_g, w_branch, w_out, ln_g, ln_b, w_ple, w_ple_gate):
    prm = dict(mu_a=mu_a, w0_a=w0_a, w_decay_up=w_decay_up, a0_a=a0_a, w_iclr_up=w_iclr_up, k_k=k_k, k_a=k_a,
               r_k=r_k, gn_a_g=gn_a_g, gn_a_b=gn_a_b, conv_b_w=conv_b_w, conv_b_b=conv_b_b, hn_b_g=hn_b_g,
               w_branch=w_branch, w_out=w_out, ln_g=ln_g, ln_b=ln_b, w_ple=w_ple, w_ple_gate=w_ple_gate)
    layers = [_layer_params(i, w_in, b_in, prm) for i in range(DEPTH)]
    bp = x_prompt.shape[0]
    zeros = lambda *s: jnp.zeros((DEPTH, bp) + s, F32)
    init_prompt = (zeros(A_SHIFT), zeros(A_HEADS, A_HEAD_DIM, A_HEAD_DIM), zeros(B_CONV - 1, 2 * B_WIDTH),
                   zeros(B_HEADS, B_HEAD_DIM, B_HEAD_DIM), zeros(B_HEADS, B_HEAD_DIM), zeros(B_HEADS), None, None)
    init_sample = (state_shift_a, state_wkv, state_conv_b, state_mlstm_c, state_mlstm_n, state_mlstm_m,
                   cache_sb_k, cache_sb_v)
    y_prompt, sp = _run_group(x_prompt, p_prompt, init_prompt, ln_in_g, ln_in_b, layers)
    y_sample, ss = _run_group(x_sample, p_sample, init_sample, ln_in_g, ln_in_b, layers)
    return (y_prompt, y_sample, sp[0], ss[0], sp[1], ss[1], sp[2], ss[2], sp[3], ss[3],
            sp[4], ss[4], sp[5], ss[5], sp[6], ss[6], sp[7], ss[7])
```

```python
import functools

import jax
import jax.numpy as jnp
from jax import lax
from jax.experimental import pallas as pl
from jax.experimental.pallas import tpu as pltpu

F32 = jnp.float32
BF16 = jnp.bfloat16

D_MODEL = 1024
DEPTH = 4
CHUNK = 64
D_PLE = 256
A_HEADS = 8
A_HEAD_DIM = 64
A_WIDTH = A_HEADS * A_HEAD_DIM
A_LORA = 64
A_SHIFT = 3 * A_WIDTH + 2 * A_LORA
B_HEADS = 4
B_HEAD_DIM = 128
B_WIDTH = B_HEADS * B_HEAD_DIM
B_CONV = 4
C_HEADS = 8
C_HEAD_DIM = 64
C_WIDTH = C_HEADS * C_HEAD_DIM
N_BRANCH = 3
DEEPNORM_ALPHA = (2 * DEPTH) ** 0.25
LN_EPS = 1e-5
GN_EPS_A = 64e-5
HN_EPS = 1e-6

LANES = 128
SUBLANES = 8
VMEM_LIMIT_BYTES = 56 * 1024 * 1024

MATMUL_TILE_ROWS = 512
RWKV_TILE_ROWS = 2048
RWKV_GROUP_CHUNKS = 2

_OFF_A_COLS = 0
_OFF_A_Z = _OFF_A_COLS + A_SHIFT
_OFF_B_QK = _OFF_A_Z + A_WIDTH
_OFF_B_V = _OFF_B_QK + 2 * B_WIDTH
_OFF_B_I = _OFF_B_V + B_WIDTH
_OFF_B_F = _OFF_B_I + B_HEADS
_OFF_B_O = _OFF_B_F + B_HEADS
_OFF_B_Z = _OFF_B_O + B_WIDTH
_OFF_C_Q = _OFF_B_Z + B_WIDTH
_OFF_C_K = _OFF_C_Q + C_WIDTH
_OFF_C_V = _OFF_C_K + C_WIDTH
_OFF_C_Z = _OFF_C_V + C_WIDTH
_OFF_GATES = _OFF_C_Z + C_WIDTH
_N_IN = _OFF_GATES + N_BRANCH * D_MODEL

_MIX_WIDTHS = (A_SHIFT, 2 * B_WIDTH, B_WIDTH, B_WIDTH, C_WIDTH, LANES)

NN = ((1,), (0,))
NT = ((1,), (1,))
TN = ((0,), (0,))


def _dg(a, b, dims=NN):
    return lax.dot_general(a, b, (dims, ((), ())), preferred_element_type=F32)


def _mm(a, b, dims=NN):
    return _dg(a.astype(BF16), b.astype(BF16), dims)


def _split3(x):
    h1 = x.astype(BF16)
    r1 = x - h1.astype(F32)
    h2 = r1.astype(BF16)
    h3 = (r1 - h2.astype(F32)).astype(BF16)
    return h1, h2, h3


def _mm_exact_rhs(lhs_bf16, x):
    h1, h2, h3 = _split3(x)
    return _dg(lhs_bf16, h1) + (_dg(lhs_bf16, h2) + _dg(lhs_bf16, h3))


def _mm_exact_lhs(x, rhs_bf16):
    h1, h2, h3 = _split3(x)
    return _dg(h1, rhs_bf16) + (_dg(h2, rhs_bf16) + _dg(h3, rhs_bf16))


def _softplus(x):
    return jnp.maximum(x, 0.0) + jnp.log(1.0 + jnp.exp(-jnp.abs(x)))


def _sigmoid(x):
    return jax.nn.sigmoid(x)


def _silu(x):
    return x * jax.nn.sigmoid(x)


def _norm_rows(x, eps):
    mu = jnp.mean(x, axis=-1, keepdims=True)
    xc = x - mu
    var = jnp.mean(xc * xc, axis=-1, keepdims=True)
    return xc * lax.rsqrt(var + eps)


def _tile(n, pref):
    if n <= pref:
        return n
    t = pref - pref % SUBLANES
    while n % t:
        t -= SUBLANES
    return t


def _params(*sem):
    return pltpu.CompilerParams(dimension_semantics=sem, vmem_limit_bytes=VMEM_LIMIT_BYTES)


def _const_spec(shape):
    nd = len(shape)
    return pl.BlockSpec(shape, lambda *_: (0,) * nd, pipeline_mode=pl.Buffered(1))


def _inproj_kernel(x_ref, w_ref, b_ref, wkv_ref, bkv_ref, *refs, pre_norm):
    n_out = len(_MIX_WIDTHS) + 2
    out_refs = refs[-n_out:]
    x = x_ref[...]
    if pre_norm:
        x = _norm_rows(x, LN_EPS) * refs[0][...] + refs[1][...]
    xb = x.astype(BF16)
    off = 0
    for o_ref, width in zip(out_refs, _MIX_WIDTHS):
        o_ref[...] = _dg(xb, w_ref[:, off:off + width]) + b_ref[:, off:off + width]
        off += width
    kvt = _dg(wkv_ref[...], xb, NT) + bkv_ref[...]
    out_refs[-2][0, 0] = kvt[:C_WIDTH]
    out_refs[-1][0, 0] = kvt[C_WIDTH:]


def _inproj_call(x, lw, layer, bsz, t, kt_stack, vt_stack, pre_norm=None):
    n, d = x.shape
    tm = _tile(t, MATMUL_TILE_ROWS)
    per_b = t // tm
    wtot = sum(_MIX_WIDTHS)
    stack = jax.ShapeDtypeStruct((DEPTH, bsz, C_WIDTH, t), F32)
    stack_spec = pl.BlockSpec((1, 1, C_WIDTH, tm), lambda i: (layer, i // per_b, 0, i % per_b))
    in_specs = [pl.BlockSpec((tm, d), lambda i: (i, 0)), _const_spec((d, wtot)), _const_spec((1, wtot)),
                _const_spec((2 * C_WIDTH, d)), _const_spec((2 * C_WIDTH, 1))]
    args = [x, lw["w_mix"], lw["b_mix"], lw["w_kvt"], lw["b_kvt"]]
    if pre_norm is not None:
        in_specs += [_const_spec((1, d))] * 2
        args += [a.reshape(1, d) for a in pre_norm]
    n_mix = len(_MIX_WIDTHS)
    aliases = {len(args): n_mix, len(args) + 1: n_mix + 1}
    in_specs += [pl.BlockSpec(memory_space=pl.ANY)] * 2
    args += [kt_stack, vt_stack]
    return pl.pallas_call(
        functools.partial(_inproj_kernel, pre_norm=pre_norm is not None),
        out_shape=tuple(jax.ShapeDtypeStruct((n, wd), F32) for wd in _MIX_WIDTHS) + (stack, stack),
        grid=(n // tm,),
        in_specs=in_specs,
        out_specs=tuple(pl.BlockSpec((tm, wd), lambda i: (i, 0)) for wd in _MIX_WIDTHS) + (stack_spec, stack_spec),
        input_output_aliases=aliases,
        compiler_params=_params("parallel"),
        name="inproj",
    )(*args)


RWKV_RING = 3


def _round_robin(*steps):
    steps = list(steps)
    while steps:
        for s in list(steps):
            try:
                next(s)
            except StopIteration:
                steps.remove(s)


def _rwkv_pipe_kernel(cols_ref, last_ref, s0_ref, mu_ref, w0_ref, wup_ref, a0_ref, aup_ref, kk_ref, ka_ref, hsum_ref,
                      rk_ref, gng_ref, gnb_ref, y_o, last_o, s_o,
                      prev_scr, s_scr, at_r, rt_r, bt_r, kt_r, btw_r, ktw_r, v_r, bon_r, w_r, u0_r, x1_r, mrb_r, wl_r):
    t = pl.program_id(1)
    L = CHUNK
    tt = cols_ref.shape[1]
    gr = min(tt, RWKV_GROUP_CHUNKS * L)
    cpg = gr // L
    ng = tt // gr
    n_pairs = A_HEADS // 2

    @pl.when(t == 0)
    def _():
        prev_scr[...] = last_ref[0]
        s_scr[...] = s0_ref[0]

    ri = lax.broadcasted_iota(jnp.int32, (L, L), 0)
    ci = lax.broadcasted_iota(jnp.int32, (L, L), 1)
    strict = ri > ci
    incl = ri >= ci
    tri = jnp.where(incl, 1.0, 0.0).astype(BF16)
    eye = jnp.where(ri == ci, 1.0, 0.0)
    lane = lax.broadcasted_iota(jnp.int32, (1, LANES), 1)
    head_masks = (jnp.where(lane < A_HEAD_DIM, 1.0, 0.0), jnp.where(lane >= A_HEAD_DIM, 1.0, 0.0))
    inv_d = 1.0 / A_HEAD_DIM
    pair = lambda p: slice(p * LANES, (p + 1) * LANES)
    chunk = lambda c: slice(c * L, (c + 1) * L)
    heads = [(p, hm) for p in range(n_pairs) for hm in head_masks]

    def prologue(g):
        gp = g
        slot = lax.rem(gp, RWKV_RING)
        r0 = pl.multiple_of(gp * gr, gr)
        u = cols_ref[0, pl.ds(r0, gr), :]
        above = cols_ref[0, pl.ds(pl.multiple_of(jnp.maximum(r0 - SUBLANES, 0), SUBLANES), SUBLANES), :]
        before = jnp.where(gp == 0, prev_scr[...], above[SUBLANES - 1:SUBLANES, :])
        row = lax.broadcasted_iota(jnp.int32, u.shape, 0)
        prev = jnp.where(row == 0, before, pltpu.roll(u, 1, 0))
        xs = u + (prev - u) * mu_ref[...]
        yield
        r = xs[:, 0:A_WIDTH]
        k = xs[:, A_WIDTH:2 * A_WIDTH]
        v = xs[:, 2 * A_WIDTH:3 * A_WIDTH]
        lora = xs[:, 3 * A_WIDTH:]
        w_log = -_softplus(-(w0_ref[...] + _mm(jnp.tanh(lora), wup_ref[...]))) - 0.5
        a = _sigmoid(a0_ref[...] + _mm(lora, aup_ref[...]))
        yield
        kk = k * kk_ref[...]
        ss = _mm_exact_lhs(kk * kk, hsum_ref[...])
        kk = kk / jnp.maximum(jnp.sqrt(ss), 1e-12)
        yield
        kh = k * (1.0 + (a - 1.0) * ka_ref[...])
        lw = -jnp.exp(w_log)
        cin = jnp.concatenate([_mm_exact_rhs(tri, lw[chunk(c), :]) for c in range(cpg)], axis=0)
        yield
        ein = jnp.exp(cin)
        einv = jnp.exp(-cin)
        bt = kk * a * einv
        kt = kh * einv
        at_r[slot] = -kk * jnp.exp(cin - lw)
        rt_r[slot] = r * ein
        yield
        bt_r[slot] = bt
        kt_r[slot] = kt
        v_r[slot] = v
        bon_r[slot] = r * kh * rk_ref[...]
        yield
        for c in range(cpg):
            wl = ein[(c + 1) * L - 1:(c + 1) * L, :]
            wl_r[slot, c] = wl
            btw_r[slot, chunk(c), :] = bt[chunk(c), :] * wl
            ktw_r[slot, chunk(c), :] = kt[chunk(c), :] * wl
        yield

    def state_free(g):
        slot = lax.rem(g - 1, RWKV_RING)
        items = [(c, p, hm) for c in range(cpg) for p, hm in heads]
        load = lambda ring: [ring[slot, chunk(c), pair(p)] * hm for c, p, hm in items]
        at_h, rt_h, v_h = load(at_r), load(rt_r), load(v_r)
        lhs = [jnp.concatenate([x, y], axis=0).astype(BF16) for x, y in zip(at_h, rt_h)]
        yield
        sc_b = [_dg(x, bt_r[slot, chunk(c), pair(p)].astype(BF16), NT) for x, (c, p, _) in zip(lhs, items)]
        sc_k = [_dg(x, kt_r[slot, chunk(c), pair(p)].astype(BF16), NT) for x, (c, p, _) in zip(lhs, items)]
        yield
        a_ab = [jnp.where(strict, s[:L], 0.0) for s in sc_b]
        a_ak = [jnp.where(strict, s[:L], 0.0) for s in sc_k]
        m_rb = [jnp.where(incl, s[L:], 0.0) for s in sc_b]
        m_rk = [jnp.where(incl, s[L:], 0.0) for s in sc_k]
        tm = [eye + x for x in a_ab]
        ap = a_ab
        for _ in range(5):
            ap = [_mm(x, x) for x in ap]
            yield
            tm = [x + _mm(x, y) for x, y in zip(tm, ap)]
            yield
        x1 = [_mm(jnp.concatenate([x, y], axis=0), vh) for x, y, vh in zip(a_ak, m_rk, v_h)]
        yield
        wu = [_mm(x, jnp.concatenate([y, z[:L]], axis=1)) for x, y, z in zip(tm, at_h, x1)]
        yield
        for n in range(0, len(items), 2):
            c, p, _ = items[n]
            w_r[slot, chunk(c), pair(p)] = wu[n][:, :LANES] + wu[n + 1][:, :LANES]
            u0_r[slot, chunk(c), pair(p)] = wu[n][:, LANES:] + wu[n + 1][:, LANES:]
            x1_r[slot, chunk(c), pair(p)] = x1[n][L:] + x1[n + 1][L:]
            mrb_r[slot, c, 2 * p] = m_rb[n]
            mrb_r[slot, c, 2 * p + 1] = m_rb[n + 1]
        yield

    def state_step(g):
        g2 = g - 2
        slot = lax.rem(g2, RWKV_RING)
        hm0, hm1 = head_masks
        y_pairs = [[] for _ in range(n_pairs)]
        for c in range(cpg):
            s_pairs = [s_scr[p] for p in range(n_pairs)]
            s_pairs_b = [s.astype(BF16) for s in s_pairs]
            w_h = [w_r[slot, chunk(c), pair(p)] * hm for p, hm in heads]
            rt_h = [rt_r[slot, chunk(c), pair(p)] * hm for p, hm in heads]
            st = [_dg(jnp.concatenate([x, y], axis=0).astype(BF16), s_pairs_b[p], NT)
                  for x, y, (p, _) in zip(w_h, rt_h, heads)]
            yield
            u_h = [u0_r[slot, chunk(c), pair(p)] * hm + s[:L] for s, (p, hm) in zip(st, heads)]
            y_h = [s[L:] + _mm(mrb_r[slot, c, n], x) for n, (s, x) in enumerate(zip(st, u_h))]
            upd = []
            for p in range(n_pairs):
                vp, bp, kp = v_r[slot, chunk(c), pair(p)], btw_r[slot, chunk(c), pair(p)], ktw_r[slot, chunk(c), pair(p)]
                upd.append(_mm(jnp.concatenate([u_h[2 * p], vp * hm0, u_h[2 * p + 1], vp * hm1], axis=0),
                               jnp.concatenate([bp * hm0, kp * hm0, bp * hm1, kp * hm1], axis=0), TN))
            yield
            wl = wl_r[slot, c]
            for p in range(n_pairs):
                s_scr[p] = s_pairs[p] * wl[:, pair(p)] + upd[p]
                y_pairs[p].append(y_h[2 * p] + y_h[2 * p + 1] + x1_r[slot, chunk(c), pair(p)])
            yield
        rows = pl.ds(pl.multiple_of(g2 * gr, gr), gr)
        for p in range(n_pairs):
            y_pair = jnp.concatenate(y_pairs[p], axis=0)
            normed = jnp.zeros(y_pair.shape, F32)
            bonus = jnp.zeros(y_pair.shape, F32)
            for hm in head_masks:
                mu = jnp.sum(y_pair * hm, axis=-1, keepdims=True) * inv_d
                yc = (y_pair - mu) * hm
                var = jnp.sum(yc * yc, axis=-1, keepdims=True) * inv_d
                normed = normed + yc * lax.rsqrt(var + GN_EPS_A)
                bonus = bonus + (jnp.sum(bon_r[slot, :, pair(p)] * hm, axis=-1, keepdims=True)
                                 * (v_r[slot, :, pair(p)] * hm))
            y_o[0, rows, pair(p)] = (normed * gng_ref[:, pair(p)] + gnb_ref[:, pair(p)]) + bonus
            yield

    def trip(g, carry):
        _round_robin(prologue(g), state_free(g), state_step(g))
        return carry

    at = jnp.int32
    if ng == 1:
        for g, step in enumerate((prologue, state_free, state_step)):
            _round_robin(step(at(g)))
    else:
        _round_robin(prologue(at(0)))
        _round_robin(prologue(at(1)), state_free(at(1)))
        lax.fori_loop(2, ng, trip, 0)
        _round_robin(state_free(at(ng)), state_step(at(ng)))
        _round_robin(state_step(at(ng + 1)))
    prev_scr[...] = cols_ref[0, tt - 1:tt, :]
    last_o[0] = cols_ref[0, tt - 1:tt, :]
    s_o[0] = s_scr[...]


def _rwkv_call(cols, last, s0_pairs, lw):
    bsz, t, _ = cols.shape
    tt = _tile(t, RWKV_TILE_ROWS)
    gr = min(tt, RWKV_GROUP_CHUNKS * CHUNK)
    blk = lambda w: pl.BlockSpec((1, tt, w), lambda b, i: (b, i, 0))
    row = pl.BlockSpec((1, 1, A_SHIFT), lambda b, i: (b, 0, 0))
    st = pl.BlockSpec((1, A_HEADS // 2, LANES, LANES), lambda b, i: (b, 0, 0, 0))
    ring = pltpu.VMEM((RWKV_RING, gr, A_WIDTH), F32)
    return pl.pallas_call(
        _rwkv_pipe_kernel,
        out_shape=(jax.ShapeDtypeStruct((bsz, t, A_WIDTH), F32),
                   jax.ShapeDtypeStruct((bsz, 1, A_SHIFT), F32),
                   jax.ShapeDtypeStruct((bsz, A_HEADS // 2, LANES, LANES), F32)),
        grid=(bsz, t // tt),
        in_specs=[blk(A_SHIFT), row, st, _const_spec((1, A_SHIFT)), _const_spec((1, A_WIDTH)),
                  _const_spec((2 * A_LORA, A_WIDTH)), _const_spec((1, A_WIDTH)), _const_spec((2 * A_LORA, A_WIDTH)),
                  _const_spec((1, A_WIDTH)), _const_spec((1, A_WIDTH)), _const_spec((A_WIDTH, A_WIDTH)),
                  _const_spec((1, A_WIDTH)), _const_spec((1, A_WIDTH)), _const_spec((1, A_WIDTH))],
        out_specs=(blk(A_WIDTH), row, st),
        scratch_shapes=[pltpu.VMEM((1, A_SHIFT), F32), pltpu.VMEM((A_HEADS // 2, LANES, LANES), F32)]
        + [ring] * 11
        + [pltpu.VMEM((RWKV_RING, gr // CHUNK, A_HEADS, CHUNK, CHUNK), F32),
           pltpu.VMEM((RWKV_RING, gr // CHUNK, 1, A_WIDTH), F32)],
        compiler_params=_params("parallel", "arbitrary"),
        name="rwkv",
    )(cols, last.reshape(bsz, 1, A_SHIFT), s0_pairs, lw["mu_a"], lw["w0_a"], lw["wup_pad"], lw["a0_a"],
      lw["aup_pad"], lw["k_k"], lw["k_a"], lw["head_sum"], lw["r_k"], lw["gn_a_g"], lw["gn_a_b"])


def _wkv_to_pairs(s):
    bsz = s.shape[0]
    s = s.reshape(bsz, A_HEADS // 2, 2, A_HEAD_DIM, A_HEAD_DIM)
    z = jnp.zeros_like(s[:, :, 0])
    top = jnp.concatenate([s[:, :, 0], z], axis=-1)
    bot = jnp.concatenate([z, s[:, :, 1]], axis=-1)
    return jnp.concatenate([top, bot], axis=-2)


def _wkv_from_pairs(sp):
    bsz = sp.shape[0]
    d = A_HEAD_DIM
    s = jnp.stack([sp[:, :, :d, :d], sp[:, :, d:, d:]], axis=2)
    return s.reshape(bsz, A_HEADS, d, d)


def _mlstm_steps(t, qk_ref, v_ref, if_ref, o_ref, buf0_ref, c0_ref, n0_ref, m0_ref, cw_ref, cb_ref, hng_ref,
                 y_o, buf_o, c_o, n_o, m_o, ext_scr, c_scr, n_scr, m_scr):
    tt = qk_ref.shape[1]
    L = min(tt, LANES)
    assert tt == L
    pad = SUBLANES

    @pl.when(t == 0)
    def _():
        ext_scr[0:pad, :] = buf0_ref[0]
        c_scr[...] = c0_ref[0]
        n_scr[...] = n0_ref[0]
        m_scr[...] = m0_ref[0]

    ext_scr[pad:pad + tt, :] = qk_ref[0]
    ri = lax.broadcasted_iota(jnp.int32, (L, L), 0)
    ci = lax.broadcasted_iota(jnp.int32, (L, L), 1)
    incl = ri >= ci
    eye = ri == ci
    tri = jnp.where(incl, 1.0, 0.0).astype(BF16)
    k_scale = B_HEAD_DIM ** -0.5

    hs = range(B_HEADS)
    sls = [slice(h * B_HEAD_DIM, (h + 1) * B_HEAD_DIM) for h in hs]

    x = ext_scr[...]
    conv = cb_ref[...] + x[pad:, :] * cw_ref[B_CONV - 1:B_CONV, :]
    for tap in range(B_CONV - 1):
        conv = conv + pltpu.roll(x, B_CONV - 1 - tap, 0)[pad:, :] * cw_ref[tap:tap + 1, :]
    yield
    qk = _silu(conv)
    yield
    ifg = if_ref[0]
    bcum = _mm_exact_rhs(tri, -_softplus(-ifg))
    yield
    q_h = [qk[:, sl] for sl in sls]
    k_h = [qk[:, B_WIDTH + h * B_HEAD_DIM:B_WIDTH + (h + 1) * B_HEAD_DIM] * k_scale for h in hs]
    v_h = [v_ref[0, :, sl] for sl in sls]
    icol = [ifg[:, h:h + 1] for h in hs]
    bcol = [bcum[:, B_HEADS + h:B_HEADS + h + 1] for h in hs]
    qk_t = [_mm(q, k, NT) for q, k in zip(q_h, k_h)]
    yield
    dcol = [i_ - b for i_, b in zip(icol, bcol)]
    drow = [jnp.sum(jnp.where(eye, d, 0.0), axis=0, keepdims=True) for d in dcol]
    dmat = [jnp.where(incl, b + d, -jnp.inf) for b, d in zip(bcol, drow)]
    yield
    dmax = [jnp.max(d, axis=-1, keepdims=True) for d in dmat]
    yield
    c_h = [c_scr[h] for h in hs]
    n_h = [n_scr[h] for h in hs]
    m_h = [m_scr[h][:, 0:1] for h in hs]
    qc = [_mm(q, c, NT) for q, c in zip(q_h, c_h)]
    yield
    inter = [b + m for b, m in zip(bcol, m_h)]
    m_t = [jnp.maximum(x_, d) for x_, d in zip(inter, dmax)]
    wmat = [jnp.exp(d - m) * s for d, m, s in zip(dmat, m_t, qk_t)]
    yield
    scale = [jnp.exp(x_ - m) for x_, m in zip(inter, m_t)]
    wv = [_mm(w, v) for w, v in zip(wmat, v_h)]
    yield
    m_new = [m[L - 1:L, :] for m in m_t]
    gcol = [jnp.exp(b[L - 1:L, :] + d - m) for b, d, m in zip(bcol, dcol, m_new)]
    c_sc = [jnp.exp(x_[L - 1:L, :] - m) for x_, m in zip(inter, m_new)]
    gvk = [_mm(g * v, k, TN) for g, v, k in zip(gcol, v_h, k_h)]
    yield
    for h in hs:
        num = wv[h] + scale[h] * qc[h]
        den = (jnp.sum(wmat[h], axis=-1, keepdims=True)
               + scale[h] * jnp.sum(q_h[h] * n_h[h], axis=-1, keepdims=True))
        hid = num / jnp.maximum(jnp.abs(den), jnp.exp(-m_t[h]))
        y_o[0, :, sls[h]] = _sigmoid(o_ref[0, :, sls[h]]) * (_norm_rows(hid, HN_EPS) * hng_ref[:, sls[h]])
        yield
    for h in hs:
        c_scr[h] = c_sc[h] * c_h[h] + gvk[h]
        n_scr[h] = c_sc[h] * n_h[h] + jnp.sum(gcol[h] * k_h[h], axis=0, keepdims=True)
        m_scr[h] = jnp.broadcast_to(m_new[h], (1, LANES))
    last = ext_scr[tt:tt + pad, :]
    ext_scr[0:pad, :] = last
    buf_o[0] = last
    c_o[0] = c_scr[...]
    n_o[0] = n_scr[...]
    m_o[0] = m_scr[...]


SB_SKIP_LOG = -105.0


_N_MLSTM_IN, _N_MLSTM_OUT, _N_SB_SCRATCH = 11, 5, 4


def _sb_mlstm_kernel(*refs, n_past_blocks):
    n_att = 5 if n_past_blocks else 3
    att_in, refs = refs[:n_att], refs[n_att:]
    ml_in, o_ref, refs = refs[:_N_MLSTM_IN], refs[_N_MLSTM_IN], refs[_N_MLSTM_IN + 1:]
    ml_out, refs = refs[:_N_MLSTM_OUT], refs[_N_MLSTM_OUT:]
    (q2_scr, c_scr, acc_scr, pad_scr), ml_scr = refs[:_N_SB_SCRATCH], refs[_N_SB_SCRATCH:]
    if n_past_blocks:
        q_ref, k_ref, v_ref, kp_ref, vp_ref = att_in
    else:
        q_ref, k_ref, v_ref = att_in
    mlstm = _mlstm_steps(pl.program_id(1), *ml_in, *ml_out, *ml_scr)
    qi = pl.program_id(1)
    qb = q_ref.shape[1]
    kb = LANES
    n_pairs = C_HEADS // 2
    n_rows = C_HEADS * qb

    rj = lax.broadcasted_iota(jnp.int32, (2 * kb, 2 * kb), 0) & (kb - 1)
    cj = lax.broadcasted_iota(jnp.int32, (2 * kb, 2 * kb), 1)
    cum2 = jnp.where((cj >= kb) | (rj > cj), 1.0, 0.0).astype(BF16)
    qpos = lax.broadcasted_iota(jnp.int32, (n_rows, kb), 0) & (qb - 1)
    kpos = lax.broadcasted_iota(jnp.int32, (n_rows, kb), 1)
    diag_mask = kpos < qpos
    lane = lax.broadcasted_iota(jnp.int32, (1, LANES), 1)
    lo_lanes = lane < C_HEAD_DIM
    q = q_ref[0] * (C_HEAD_DIM ** -0.5)
    for p in range(n_pairs):
        qs = q[:, p * LANES:(p + 1) * LANES]
        q2_scr[p] = jnp.concatenate([jnp.where(lo_lanes, qs, 0.0), jnp.where(lo_lanes, 0.0, qs)], axis=0).astype(BF16)

    def load_t(ref, slot, p, keys):
        x = ref[0, 0, p * LANES:(p + 1) * LANES, keys]
        if x.shape[1] == kb:
            return x.astype(BF16)
        pad_scr[slot] = jnp.zeros((LANES, kb), F32)
        pad_scr[slot, :, 0:x.shape[1]] = x
        return pad_scr[slot].astype(BF16)

    def half_block(pairs, kr, vr, keys, first, c_max):
        rows = slice(pairs[0] * 2 * qb, (pairs[-1] + 1) * 2 * qb)
        mask = diag_mask[:rows.stop - rows.start]
        z = jnp.concatenate([_dg(q2_scr[p], load_t(kr, 2 * p, p, keys)) for p in pairs], axis=0)
        yield
        sp = _softplus(z)
        lk = jnp.where(mask, -sp, 0.0) if first else -sp
        h1 = lk.astype(BF16)
        h2 = (lk - h1.astype(F32)).astype(BF16)
        yield
        cs = _dg(jnp.concatenate([h1, h2], axis=1), cum2)
        yield
        if first:
            a = jnp.where(mask, jnp.exp((z - sp) + cs[:, :kb]), 0.0)
            c_new = cs[:, kb:]
        else:
            c_old = c_scr[rows]
            a = jnp.exp((z - sp) + cs[:, :kb] + c_old)
            c_new = c_old + cs[:, kb:]
        c_scr[rows] = c_new
        c_max.append(jnp.max(c_new))
        ab = a.astype(BF16)
        yield
        for n, p in enumerate(pairs):
            rs = slice(p * 2 * qb, (p + 1) * 2 * qb)
            upd = _dg(ab[n * 2 * qb:(n + 1) * 2 * qb], load_t(vr, 2 * p + 1, p, keys), NT)
            acc_scr[rs] = upd if first else acc_scr[rs] + upd
        yield

    def block(kr, vr, keys, first, others=()):
        c_max = []
        lead = half_block(tuple(range(n_pairs // 2)), kr, vr, keys, first, c_max)
        lag = half_block(tuple(range(n_pairs // 2, n_pairs)), kr, vr, keys, first, c_max)
        next(lead)
        _round_robin(lead, lag, *others)
        return jnp.maximum(*c_max) > SB_SKIP_LOG

    own_keys = pl.ds(pl.multiple_of(qi * qb, qb), qb) if qb == kb else pl.ds(0, qb)
    go = block(k_ref, v_ref, own_keys, True, (mlstm,))

    def sweep(kr, vr, n_blocks, go):
        def cond(carry):
            j, more = carry
            return jnp.logical_and(j < n_blocks, more)

        def body(carry):
            j, _ = carry
            keys = pl.ds(pl.multiple_of((n_blocks - 1 - j) * kb, kb), kb)
            return j + 1, block(kr, vr, keys, False)

        return lax.while_loop(cond, body, (jnp.int32(0), go))[1]

    if qb == kb:
        go = sweep(k_ref, v_ref, qi, go)
    if n_past_blocks:
        go = sweep(kp_ref, vp_ref, n_past_blocks, go)

    for p in range(n_pairs):
        r0 = p * 2 * qb
        o_ref[0, :, p * LANES:(p + 1) * LANES] = jnp.where(lo_lanes, acc_scr[r0:r0 + qb], acc_scr[r0 + qb:r0 + 2 * qb])


def _sb_mlstm_call(q, kt, vt, layer, kt_past, vt_past, qk, v, ifg, o, buf0, c0, n0, m0, lw):
    bsz, t, _ = q.shape
    qb = min(LANES, t)
    assert t % qb == 0 and (qb == LANES or t == qb)
    n_past = 0 if kt_past is None else kt_past.shape[-1]
    assert n_past % LANES == 0
    blk = lambda w: pl.BlockSpec((1, qb, w), lambda b, i: (b, i, 0))
    full = lambda n: pl.BlockSpec((1, 1, C_WIDTH, n), lambda b, i: (layer, b, 0, 0))
    fix = lambda *s: pl.BlockSpec((1,) + s, lambda b, i: (b,) + (0,) * len(s))
    in_specs = [blk(C_WIDTH), full(t), full(t)]
    args = [q, kt, vt]
    if n_past:
        in_specs += [full(n_past), full(n_past)]
        args += [kt_past, vt_past]
    in_specs += [blk(2 * B_WIDTH), blk(B_WIDTH), blk(LANES), blk(B_WIDTH),
                 fix(SUBLANES, 2 * B_WIDTH), fix(B_HEADS, B_HEAD_DIM, B_HEAD_DIM), fix(B_HEADS, 1, B_HEAD_DIM),
                 fix(B_HEADS, 1, LANES),
                 _const_spec((B_CONV, 2 * B_WIDTH)), _const_spec((1, 2 * B_WIDTH)), _const_spec((1, B_WIDTH))]
    args += [qk, v, ifg, o, buf0, c0, n0, m0, lw["conv_b_w"], lw["conv_b_b"], lw["hn_b_g"]]
    assert len(in_specs) - (5 if n_past else 3) == _N_MLSTM_IN
    return pl.pallas_call(
        functools.partial(_sb_mlstm_kernel, n_past_blocks=n_past // LANES),
        out_shape=(jax.ShapeDtypeStruct((bsz, t, C_WIDTH), F32),
                   jax.ShapeDtypeStruct((bsz, t, B_WIDTH), F32),
                   jax.ShapeDtypeStruct((bsz, SUBLANES, 2 * B_WIDTH), F32),
                   jax.ShapeDtypeStruct((bsz, B_HEADS, B_HEAD_DIM, B_HEAD_DIM), F32),
                   jax.ShapeDtypeStruct((bsz, B_HEADS, 1, B_HEAD_DIM), F32),
                   jax.ShapeDtypeStruct((bsz, B_HEADS, 1, LANES), F32)),
        grid=(bsz, t // qb),
        in_specs=in_specs,
        out_specs=(blk(C_WIDTH), blk(B_WIDTH), fix(SUBLANES, 2 * B_WIDTH), fix(B_HEADS, B_HEAD_DIM, B_HEAD_DIM),
                   fix(B_HEADS, 1, B_HEAD_DIM), fix(B_HEADS, 1, LANES)),
        scratch_shapes=[pltpu.VMEM((C_HEADS // 2, 2 * qb, LANES), BF16),
                        pltpu.VMEM((C_HEADS * qb, LANES), F32), pltpu.VMEM((C_HEADS * qb, LANES), F32),
                        pltpu.VMEM((C_HEADS, LANES, LANES), F32),
                        pltpu.VMEM((qb + SUBLANES, 2 * B_WIDTH), F32),
                        pltpu.VMEM((B_HEADS, B_HEAD_DIM, B_HEAD_DIM), F32),
                        pltpu.VMEM((B_HEADS, 1, B_HEAD_DIM), F32),
                        pltpu.VMEM((B_HEADS, 1, LANES), F32)],
        compiler_params=_params("parallel", "arbitrary"),
        name="sb_mlstm",
    )(*args)


def _merge_kernel(x_ref, ya_ref, yb_ref, yc_ref, p_ref, wz_ref, bz_ref, wg_ref, bg_ref, wbr_ref, wout_ref,
                  lng_ref, lnb_ref, wple_ref, wpg_ref, *refs, pre_norm):
    o_ref = refs[-1]
    width = A_WIDTH
    half_rows = x_ref.shape[0] // 2

    def half(r):
        rows = slice(r * half_rows, (r + 1) * half_rows)
        x = x_ref[rows, :]
        if pre_norm:
            x = _norm_rows(x, LN_EPS) * refs[0][...] + refs[1][...]
        xb = x.astype(BF16)
        mix_pre = jnp.zeros(x.shape, F32)
        for n, y_ref in enumerate((ya_ref, yb_ref, yc_ref)):
            z = _dg(xb, wz_ref[:, n * width:(n + 1) * width]) + bz_ref[:, n * width:(n + 1) * width]
            yield
            ys = y_ref[rows, :] * _silu(z)
            br = _mm(ys, wbr_ref[n])
            yield
            gate = _dg(xb, wg_ref[:, n * D_MODEL:(n + 1) * D_MODEL]) + bg_ref[:, n * D_MODEL:(n + 1) * D_MODEL]
            yield
            mix_pre = mix_pre + _sigmoid(gate) * br
        mix = _mm(mix_pre, wout_ref[...])
        yield
        x1 = _norm_rows(DEEPNORM_ALPHA * x + mix, LN_EPS) * lng_ref[...] + lnb_ref[...]
        yield
        o_ref[rows, :] = x1 + _mm(p_ref[0, rows, :], wple_ref[...]) * _sigmoid(_mm(x1, wpg_ref[...]))
        yield

    lead, lag = half(0), half(1)
    next(lead)
    _round_robin(lead, lag)


def _merge_call(x, ya, yb, yc, p_all, layer, lw, pre_norm=None):
    n, d = x.shape
    tm = _tile(n, MATMUL_TILE_ROWS)
    rows = lambda w: pl.BlockSpec((tm, w), lambda i: (i, 0))
    in_specs = [rows(d), rows(A_WIDTH), rows(B_WIDTH), rows(C_WIDTH),
                pl.BlockSpec((1, tm, D_PLE), lambda i: (layer, i, 0)),
                _const_spec((d, N_BRANCH * A_WIDTH)), _const_spec((1, N_BRANCH * A_WIDTH)),
                _const_spec((d, N_BRANCH * d)), _const_spec((1, N_BRANCH * d)),
                _const_spec((N_BRANCH, A_WIDTH, d)), _const_spec((d, d)),
                _const_spec((1, d)), _const_spec((1, d)), _const_spec((D_PLE, d)), _const_spec((d, d))]
    args = [x, ya, yb, yc, p_all, lw["w_z"], lw["b_z"], lw["w_gates"], lw["b_gates"], lw["w_branch"], lw["w_out"],
            lw["ln_g"], lw["ln_b"], lw["w_ple"], lw["w_ple_gate"]]
    if pre_norm is not None:
        in_specs += [_const_spec((1, d))] * 2
        args += [a.reshape(1, d) for a in pre_norm]
    return pl.pallas_call(
        functools.partial(_merge_kernel, pre_norm=pre_norm is not None),
        out_shape=jax.ShapeDtypeStruct((n, d), F32),
        grid=(n // tm,),
        in_specs=in_specs,
        out_specs=rows(d),
        compiler_params=_params("parallel"),
        name="merge",
    )(*args)


def _cols(w, *ranges):
    return jnp.concatenate([w[..., a:b] for a, b in ranges], axis=-1)


def _layer_params(i, w_in, b_in, prm):
    w = w_in[i]
    b = b_in[i]
    mix_ranges = ((_OFF_A_COLS, _OFF_A_Z), (_OFF_B_QK, _OFF_B_V), (_OFF_B_V, _OFF_B_I), (_OFF_B_O, _OFF_B_Z),
                  (_OFF_C_Q, _OFF_C_K), (_OFF_B_I, _OFF_B_O))
    pad = LANES - 2 * B_HEADS
    w_mix = jnp.pad(_cols(w, *mix_ranges), ((0, 0), (0, pad)))
    b_mix = jnp.pad(_cols(b, *mix_ranges), ((0, pad),))
    z_ranges = ((_OFF_A_Z, _OFF_B_QK), (_OFF_B_Z, _OFF_C_Q), (_OFF_C_Z, _OFF_GATES))
    zeros_lora = jnp.zeros((A_LORA, A_WIDTH), F32)
    hid = jnp.arange(A_WIDTH) // A_HEAD_DIM
    row = lambda name: prm[name][i].reshape(1, -1)
    return {
        "w_mix": w_mix.astype(BF16), "b_mix": b_mix.reshape(1, -1),
        "w_kvt": w[:, _OFF_C_K:_OFF_C_Z].T.astype(BF16), "b_kvt": b[_OFF_C_K:_OFF_C_Z].reshape(-1, 1),
        "w_z": _cols(w, *z_ranges).astype(BF16), "b_z": _cols(b, *z_ranges).reshape(1, -1),
        "w_gates": w[:, _OFF_GATES:].astype(BF16), "b_gates": b[_OFF_GATES:].reshape(1, -1),
        "mu_a": row("mu_a"), "w0_a": row("w0_a"), "a0_a": row("a0_a"), "k_k": row("k_k"), "k_a": row("k_a"),
        "wup_pad": jnp.concatenate([prm["w_decay_up"][i], zeros_lora], axis=0).astype(BF16),
        "aup_pad": jnp.concatenate([zeros_lora, prm["w_iclr_up"][i]], axis=0).astype(BF16),
        "head_sum": (hid[:, None] == hid[None, :]).astype(BF16),
        "r_k": row("r_k"), "gn_a_g": row("gn_a_g"), "gn_a_b": row("gn_a_b"),
        "conv_b_w": prm["conv_b_w"][i], "conv_b_b": row("conv_b_b"), "hn_b_g": row("hn_b_g"),
        "w_branch": prm["w_branch"][i].astype(BF16), "w_out": prm["w_out"][i].astype(BF16),
        "ln_g": row("ln_g"), "ln_b": row("ln_b"),
        "w_ple": prm["w_ple"][i].astype(BF16), "w_ple_gate": prm["w_ple_gate"][i].astype(BF16),
    }


def _run_group(x, p, init, ln_in_g, ln_in_b, layers):
    bsz, t, d = x.shape
    n = bsz * t
    shift0, wkv0, conv0, c0, n0, m0, k_past, v_past = init
    xf = x.reshape(n, d)
    new = [[] for _ in range(6)]
    kt = jnp.zeros((DEPTH, bsz, C_WIDTH, t), F32)
    vt = jnp.zeros((DEPTH, bsz, C_WIDTH, t), F32)
    channel_major = lambda a: jnp.transpose(a, (0, 1, 3, 4, 2)).reshape(a.shape[0], a.shape[1], C_WIDTH, a.shape[2])
    if k_past is not None:
        k_past, v_past = channel_major(k_past), channel_major(v_past)
    for i, lw in enumerate(layers):
        pre_norm = (ln_in_g, ln_in_b) if i == 0 else None
        a_cols, b_qk, b_v, b_o, c_q, b_if, kt, vt = _inproj_call(xf, lw, i, bsz, t, kt, vt, pre_norm)
        seq = lambda a: a.reshape(bsz, t, a.shape[-1])
        ya, last, s_pairs = _rwkv_call(seq(a_cols), shift0[i], _wkv_to_pairs(wkv0[i]), lw)
        buf0 = jnp.pad(conv0[i], ((0, 0), (SUBLANES - (B_CONV - 1), 0), (0, 0)))
        m0b = jnp.broadcast_to(m0[i][:, :, None, None], (bsz, B_HEADS, 1, LANES))
        yc, yb, buf, c1, n1, m1 = _sb_mlstm_call(seq(c_q), kt, vt, i, k_past, v_past,
                                                 seq(b_qk), seq(b_v), seq(b_if), seq(b_o), buf0, c0[i],
                                                 n0[i][:, :, None, :], m0b, lw)
        xf = _merge_call(xf, ya.reshape(n, A_WIDTH), yb.reshape(n, B_WIDTH), yc.reshape(n, C_WIDTH),
                         p.reshape(DEPTH, n, D_PLE), i, lw, pre_norm)
        states = (last[:, 0], _wkv_from_pairs(s_pairs), buf[:, SUBLANES - (B_CONV - 1):], c1, n1[:, :, 0],
                  m1[:, :, 0, 0])
        for lst, s in zip(new, states):
            lst.append(s)
    token_major = lambda a: jnp.transpose(a.reshape(DEPTH, bsz, C_HEADS, C_HEAD_DIM, t), (0, 1, 4, 2, 3))
    return xf.reshape(bsz, t, d), [jnp.stack(lst) for lst in new] + [token_major(kt), token_major(vt)]


def kernel(x_prompt, x_sample, state_shift_a, state_wkv, state_conv_b, state_mlstm_c, state_mlstm_n, state_mlstm_m, cache_sb_k, cache_sb_v, p_prompt, p_sample, ln_in_g, ln_in_b, w_in, b_in, mu_a, w0_a, w_decay_up, a0_a, w_iclr_up, k_k, k_a, r_k, gn_a_g, gn_a_b, conv_b_w, conv_b_b, hn_b_g, w_branch, w_out, ln_g, ln_b, w_ple, w_ple_gate):
    prm = dict(mu_a=mu_a, w0_a=w0_a, w_decay_up=w_decay_up, a0_a=a0_a, w_iclr_up=w_iclr_up, k_k=k_k, k_a=k_a,
               r_k=r_k, gn_a_g=gn_a_g, gn_a_b=gn_a_b, conv_b_w=conv_b_w, conv_b_b=conv_b_b, hn_b_g=hn_b_g,
               w_branch=w_branch, w_out=w_out, ln_g=ln_g, ln_b=ln_b, w_ple=w_ple, w_ple_gate=w_ple_gate)
    layers = [_layer_params(i, w_in, b_in, prm) for i in range(DEPTH)]
    bp = x_prompt.shape[0]
    zeros = lambda *s: jnp.zeros((DEPTH, bp) + s, F32)
    init_prompt = (zeros(A_SHIFT), zeros(A_HEADS, A_HEAD_DIM, A_HEAD_DIM), zeros(B_CONV - 1, 2 * B_WIDTH),
                   zeros(B_HEADS, B_HEAD_DIM, B_HEAD_DIM), zeros(B_HEADS, B_HEAD_DIM), zeros(B_HEADS), None, None)
    init_sample = (state_shift_a, state_wkv, state_conv_b, state_mlstm_c, state_mlstm_n, state_mlstm_m,
                   cache_sb_k, cache_sb_v)
    y_prompt, sp = _run_group(x_prompt, p_prompt, init_prompt, ln_in_g, ln_in_b, layers)
    y_sample, ss = _run_group(x_sample, p_sample, init_sample, ln_in_g, ln_in_b, layers)
    return (y_prompt, y_sample, sp[0], ss[0], sp[1], ss[1], sp[2], ss[2], sp[3], ss[3],
            sp[4], ss[4], sp[5], ss[5], sp[6], ss[6], sp[7], ss[7])
```
